```python
import math, functools
import jax, jax.numpy as jnp
from jax import lax
import numpy as np

D_MODEL = 2048
BATCH = 4
SEQ = 2048
DEPTH = 2

GRID_W = 64
CTX_LEN = 256
EPS = 1e-6
ROPE_BASE = 10000.0
Q_BLOCK = 128
D_FF = 4 * D_MODEL

HEAD_DIM = 128
MIX_HEADS = D_MODEL // HEAD_DIM
A_HEADS = MIX_HEADS // 2
A_SUB = HEAD_DIM // 2
B_HEADS = MIX_HEADS - A_HEADS
B_DK = HEAD_DIM
B_DV = HEAD_DIM
CONV_K = 5
CHUNK = 64
A_QK = A_HEADS * 2 * A_SUB
A_V = A_HEADS * HEAD_DIM
B_QK = B_HEADS * B_DK
B_V = B_HEADS * B_DV
EVEN_SPLITS = (A_QK, A_QK, A_V, B_QK, B_QK, B_V, B_V, B_HEADS, B_HEADS, B_HEADS, B_HEADS)
EVEN_IN = sum(EVEN_SPLITS)

QK_NOPE = 128
QK_ROPE = 64
V_HEAD = 128
C_HEADS = D_MODEL // V_HEAD
Q_LORA = 512
KV_LORA = 512
ODD_IN = Q_LORA + KV_LORA + QK_ROPE

kernel_name = 'hybrid_diffattn_gdn_mla_prefix_dit'


def rms_norm(x, gain):
    xf = x.astype(jnp.float32)
    y = xf * lax.rsqrt(jnp.mean(xf * xf, axis=-1, keepdims=True) + EPS)
    return (y * gain.astype(jnp.float32)).astype(x.dtype)


def l2_norm(x):
    xf = x.astype(jnp.float32)
    return (xf * lax.rsqrt(jnp.sum(xf * xf, axis=-1, keepdims=True) + EPS)).astype(x.dtype)


def split_heads(t, n_heads):
    b, l, hd = t.shape
    return t.reshape(b, l, n_heads, hd // n_heads).transpose(0, 2, 1, 3)


def merge_heads(t):
    b, h, l, d = t.shape
    return t.transpose(0, 2, 1, 3).reshape(b, l, h * d)


def axial_rope_tables(n_tokens, rot_dim):
    rows = n_tokens // GRID_W
    t = jnp.arange(rows * GRID_W)
    row = (t // GRID_W).astype(jnp.float32)
    col = (t % GRID_W).astype(jnp.float32)
    n_freq = rot_dim // 4
    inv_freq = ROPE_BASE ** (-jnp.arange(n_freq, dtype=jnp.float32) / n_freq)
    ang = jnp.concatenate([row[:, None] * inv_freq, col[:, None] * inv_freq], axis=-1)
    return jnp.cos(ang), jnp.sin(ang)


def apply_rope(x, cos, sin):
    half = x.shape[-1] // 2
    x1, x2 = x[..., :half], x[..., half:]
    cos = cos.astype(x.dtype)
    sin = sin.astype(x.dtype)
    return jnp.concatenate([x1 * cos - x2 * sin, x2 * cos + x1 * sin], axis=-1)


def sweep_query_blocks(fn, *qs):
    b, h, s, _ = qs[0].shape
    nb = s // Q_BLOCK
    blocks = tuple(jnp.moveaxis(q.reshape(b, h, nb, Q_BLOCK, q.shape[-1]), 2, 0) for q in qs)
    out = lax.map(lambda qb: fn(*qb), blocks)
    return jnp.moveaxis(out, 0, 2).reshape(b, h, s, out.shape[-1])


def sq_relu_mlp(h, w1, w2):
    return jnp.square(jax.nn.relu(h @ w1)) @ w2


def diff_qk(t, gain):
    b, l, _ = t.shape
    t = rms_norm(t.reshape(b, l, A_HEADS, 2, A_SUB), gain).transpose(0, 2, 3, 1, 4)
    return t[:, :, 0], t[:, :, 1]


def diff_attn_core(q1, q2, k1, k2, v, lam):
    scale = A_SUB ** -0.5
    p1 = jax.nn.softmax(jnp.einsum('bhqd,bhkd->bhqk', q1, k1).astype(jnp.float32) * scale, axis=-1)
    p2 = jax.nn.softmax(jnp.einsum('bhqd,bhkd->bhqk', q2, k2).astype(jnp.float32) * scale, axis=-1)
    return jnp.einsum('bhqk,bhkd->bhqd', (p1 - lam * p2).astype(v.dtype), v)


def short_conv(x, w):
    y = lax.conv_general_dilated(x, w[:, None, :], window_strides=(1,),
                                 padding=[(CONV_K // 2, CONV_K // 2)],
                                 dimension_numbers=('NWC', 'WIO', 'NWC'),
                                 feature_group_count=x.shape[-1])
    return jax.nn.silu(y)


def gdn_gates(a, bt, alog, dtb):
    f32 = jnp.float32
    g = -jnp.exp(alog.astype(f32)) * jax.nn.softplus(a.astype(f32) + dtb.astype(f32))
    beta = jax.nn.sigmoid(bt.astype(f32))
    return jnp.swapaxes(beta, 1, 2), jnp.swapaxes(g, 1, 2)


def unit_lower_inverse(a):
    n = a.shape[-1]
    p = -a
    t = jnp.eye(n, dtype=a.dtype) + p
    for _ in range(int(math.log2(n)) - 1):
        p = p @ p
        t = t + t @ p
    return t


def gdn_chunked(q, k, v, beta, g, state):
    out_dtype = v.dtype
    f32 = jnp.float32
    q, k, v, beta, g, state = (t.astype(f32) for t in (q, k, v, beta, g, state))
    b, h, l, dk = q.shape
    dv = v.shape[-1]
    n = l // CHUNK
    ch = lambda t: t.reshape(b, h, n, CHUNK, *t.shape[3:])
    q, k, v, beta, g = ch(q), ch(k), ch(v), ch(beta), ch(g)
    G = jnp.cumsum(g, axis=-1)
    idx = jnp.arange(CHUNK)
    incl = idx[:, None] >= idx[None, :]
    strict = idx[:, None] > idx[None, :]
    gamma = jnp.exp(jnp.where(incl, G[..., :, None] - G[..., None, :], -jnp.inf))
    a_mat = jnp.where(strict, beta[..., :, None] * jnp.einsum('bhncd,bhned->bhnce', k, k) * gamma, 0.0)
    t_inv = unit_lower_inverse(a_mat)
    w = t_inv @ (k * (beta * jnp.exp(G))[..., None])
    u = t_inv @ (v * beta[..., None])
    qk = jnp.einsum('bhncd,bhned->bhnce', q, k) * gamma
    q_dec = q * jnp.exp(G)[..., None]
    k_dec = k * jnp.exp(G[..., -1:] - G)[..., None]
    g_tot = jnp.exp(G[..., -1])

    def step(s, xs):
        w_i, u_i, qk_i, qd_i, kd_i, gt_i = xs
        v_new = u_i - w_i @ s
        o_i = qd_i @ s + qk_i @ v_new
        s = gt_i[..., None, None] * s + jnp.swapaxes(kd_i, -1, -2) @ v_new
        return s, o_i

    xs = tuple(jnp.moveaxis(t, 2, 0) for t in (w, u, qk, q_dec, k_dec, g_tot))
    state, o = lax.scan(step, state, xs)
    o = jnp.moveaxis(o, 0, 2).reshape(b, h, l, dv)
    return o.astype(out_dtype), state


def gdn_bidirectional(q, k, v, beta_f, g_f, beta_b, g_b, s0_f, s0_b):
    o_f, s_f = gdn_chunked(q, k, v, beta_f, g_f, s0_f)
    rev = lambda t: jnp.flip(t, axis=2)
    o_b, s_b = gdn_chunked(rev(q), rev(k), rev(v), rev(beta_b), rev(g_b), s0_b)
    return o_f + rev(o_b), s_f, s_b


def even_mixer(h, hc, want_ctx, P, cos, sin, lam_init):
    cuts = [int(s) for s in np.cumsum(EVEN_SPLITS)[:-1]]
    aq, ak, av, bq, bk, bv, bz, af, ab, bf, bb = jnp.split(h @ P['w_in'], cuts, axis=-1)
    caq, cak, cav, cbq, cbk, cbv, cbz, caf, cab, cbf, cbb = jnp.split(hc @ P['w_in'], cuts, axis=-1)
    f32 = jnp.float32
    lam = (jnp.exp(jnp.sum(P['lam_q1'].astype(f32) * P['lam_k1'].astype(f32)))
           - jnp.exp(jnp.sum(P['lam_q2'].astype(f32) * P['lam_k2'].astype(f32))) + lam_init)

    q1, q2 = diff_qk(aq, P['q_norm'])
    q1, q2 = apply_rope(q1, cos, sin), apply_rope(q2, cos, sin)
    k1, k2 = diff_qk(ak, P['k_norm'])
    k1, k2 = apply_rope(k1, cos, sin), apply_rope(k2, cos, sin)
    ck1, ck2 = diff_qk(cak, P['k_norm'])
    cv = split_heads(cav, A_HEADS)
    k1_all = jnp.concatenate([ck1, k1], axis=2)
    k2_all = jnp.concatenate([ck2, k2], axis=2)
    v_all = jnp.concatenate([cv, split_heads(av, A_HEADS)], axis=2)
    o_a = sweep_query_blocks(lambda qa, qb: diff_attn_core(qa, qb, k1_all, k2_all, v_all, lam), q1, q2)
    finish_a = lambda o: merge_heads(rms_norm(o, P['subln']) * (1.0 - lam_init))

    def gdn_prepare(xq, xk, xv, a_f, a_b, b_f, b_b):
        qkv = short_conv(jnp.concatenate([xq, xk, xv], axis=-1), P['conv'])
        q, k, v = jnp.split(qkv, [B_QK, 2 * B_QK], axis=-1)
        q = l2_norm(split_heads(q, B_HEADS)) * (B_DK ** -0.5)
        k = l2_norm(split_heads(k, B_HEADS))
        v = split_heads(v, B_HEADS)
        beta_f, g_f = gdn_gates(a_f, b_f, P['alog_f'], P['dtb_f'])
        beta_b, g_b = gdn_gates(a_b, b_b, P['alog_b'], P['dtb_b'])
        return q, k, v, beta_f, g_f, beta_b, g_b

    zero = jnp.zeros((h.shape[0], B_HEADS, B_DK, B_DV), jnp.float32)
    oc_b, s_f, s_b = gdn_bidirectional(*gdn_prepare(cbq, cbk, cbv, caf, cab, cbf, cbb), zero, zero)
    o_b, _, _ = gdn_bidirectional(*gdn_prepare(bq, bk, bv, af, ab, bf, bb), s_f, s_b)
    finish_b = lambda o, z: merge_heads(rms_norm(o, P['o_norm']) * jax.nn.silu(split_heads(z, B_HEADS)))

    y = jnp.concatenate([finish_a(o_a), finish_b(o_b, bz)], axis=-1) @ P['w_out']
    yc = None
    if want_ctx:
        cq1, cq2 = diff_qk(caq, P['q_norm'])
        oc_a = diff_attn_core(cq1, cq2, ck1, ck2, cv, lam)
        yc = jnp.concatenate([finish_a(oc_a), finish_b(oc_b, cbz)], axis=-1) @ P['w_out']
    return y, yc


def mla_core(q_nope, q_rope, k_nope, k_rope, v):
    s = (jnp.einsum('bhqd,bhkd->bhqk', q_nope, k_nope)
         + jnp.einsum('bhqr,bkr->bhqk', q_rope, k_rope))
    p = jax.nn.softmax(s.astype(jnp.float32) * ((QK_NOPE + QK_ROPE) ** -0.5), axis=-1)
    return jnp.einsum('bhqk,bhkd->bhqd', p.astype(v.dtype), v)


def odd_mixer(h, hc, want_ctx, P, cos, sin):
    def keys(proj, rope):
        ckv = proj[..., Q_LORA:Q_LORA + KV_LORA]
        kv = split_heads(rms_norm(ckv, P['kva_norm']) @ P['w_ukv'], C_HEADS)
        k_nope = rms_norm(kv[..., :QK_NOPE], P['kn_norm'])
        v = kv[..., QK_NOPE:]
        k_rope = rms_norm(proj[..., Q_LORA + KV_LORA:], P['kr_norm'])
        if rope:
            k_rope = apply_rope(k_rope, cos, sin)
        return k_nope, k_rope, v

    def queries(proj, rope):
        q = split_heads(rms_norm(proj[..., :Q_LORA], P['qa_norm']) @ P['w_uq'], C_HEADS)
        q_nope = rms_norm(q[..., :QK_NOPE], P['qn_norm'])
        q_rope = rms_norm(q[..., QK_NOPE:], P['qr_norm'])
        if rope:
            q_rope = apply_rope(q_rope, cos, sin)
        return q_nope, q_rope

    proj = h @ P['w_in']
    proj_c = hc @ P['w_in']
    ck_nope, ck_rope, cv = keys(proj_c, False)
    k_nope, k_rope, v = keys(proj, True)
    k_nope_all = jnp.concatenate([ck_nope, k_nope], axis=2)
    k_rope_all = jnp.concatenate([ck_rope, k_rope], axis=1)
    v_all = jnp.concatenate([cv, v], axis=2)
    q_nope, q_rope = queries(proj, True)
    o = sweep_query_blocks(lambda qn, qr: mla_core(qn, qr, k_nope_all, k_rope_all, v_all), q_nope, q_rope)
    y = merge_heads(o) @ P['w_out']
    yc = None
    if want_ctx:
        cq_nope, cq_rope = queries(proj_c, False)
        yc = merge_heads(mla_core(cq_nope, cq_rope, ck_nope, ck_rope, cv)) @ P['w_out']
    return y, yc


def residual_block(x, xc, mod, mod_c, n1, n2, w1, w2, mixer, want_ctx):
    sh1, sc1, g1, sh2, sc2, g2 = jnp.split(mod[:, None, :], 6, axis=-1)
    csh1, csc1, cg1, csh2, csc2, cg2 = jnp.split(mod_c[:, None, :], 6, axis=-1)
    y, yc = mixer(rms_norm(x, n1) * (1.0 + sc1) + sh1, rms_norm(xc, n1) * (1.0 + csc1) + csh1, want_ctx)
    x = x + g1 * y
    x = x + g2 * sq_relu_mlp(rms_norm(x, n2) * (1.0 + sc2) + sh2, w1, w2)
    if want_ctx:
        xc = xc + cg1 * yc
        xc = xc + cg2 * sq_relu_mlp(rms_norm(xc, n2) * (1.0 + csc2) + csh2, w1, w2)
    return x, xc


def setup_inputs(seed: int = 0) -> dict:
    key = jax.random.key(seed)
    ks = iter(jax.random.split(key, 48))
    nrm = lambda shape, std: jax.random.normal(next(ks), shape, jnp.float32) * std
    gain = lambda shape: 1.0 + nrm(shape, 0.05)
    d = D_MODEL
    ne, no = (DEPTH + 1) // 2, DEPTH // 2
    col_scale = jnp.concatenate([
        jnp.full((EVEN_IN - 4 * B_HEADS,), d ** -0.5, jnp.float32),
        jnp.full((2 * B_HEADS,), 0.1 * d ** -0.5, jnp.float32),
        jnp.full((2 * B_HEADS,), d ** -0.5, jnp.float32)])

    def alog():
        return jnp.log(jax.random.uniform(next(ks), (ne, B_HEADS), jnp.float32, 1.0, 16.0))

    def dt_bias():
        dt = jnp.exp(jax.random.uniform(next(ks), (ne, B_HEADS), jnp.float32, math.log(1e-3), math.log(1e-1)))
        return dt + jnp.log(-jnp.expm1(-dt))

    return {
        'x': nrm((BATCH, SEQ, d), 1.0),
        'c': nrm((BATCH, d), 1.0),
        'ctx': nrm((BATCH, CTX_LEN, d), 1.0),
        'c_ctx': nrm((d,), 1.0),
        'ada_w': nrm((DEPTH, d, 6 * d), 0.5 * d ** -0.5),
        'ada_b': nrm((DEPTH, 6 * d), 0.02),
        'norm1': gain((DEPTH, d)),
        'norm2': gain((DEPTH, d)),
        'mlp_w1': nrm((DEPTH, d, D_FF), d ** -0.5),
        'mlp_w2': nrm((DEPTH, D_FF, d), D_FF ** -0.5),
        'e_w_in': nrm((ne, d, EVEN_IN), 1.0) * col_scale,
        'e_q_norm': gain((ne, A_SUB)),
        'e_k_norm': gain((ne, A_SUB)),
        'e_lam_q1': nrm((ne, A_SUB), 0.1),
        'e_lam_k1': nrm((ne, A_SUB), 0.1),
        'e_lam_q2': nrm((ne, A_SUB), 0.1),
        'e_lam_k2': nrm((ne, A_SUB), 0.1),
        'e_subln': gain((ne, HEAD_DIM)),
        'e_conv': nrm((ne, CONV_K, 2 * B_QK + B_V), CONV_K ** -0.5),
        'e_alog_f': alog(),
        'e_alog_b': alog(),
        'e_dtb_f': dt_bias(),
        'e_dtb_b': dt_bias(),
        'e_o_norm': gain((ne, B_DV)),
        'e_w_out': nrm((ne, A_V + B_V, d), (A_V + B_V) ** -0.5),
        'o_w_in': nrm((no, d, ODD_IN), d ** -0.5),
        'o_qa_norm': gain((no, Q_LORA)),
        'o_w_uq': nrm((no, Q_LORA, C_HEADS * (QK_NOPE + QK_ROPE)), Q_LORA ** -0.5),
        'o_kva_norm': gain((no, KV_LORA)),
        'o_w_ukv': nrm((no, KV_LORA, C_HEADS * (QK_NOPE + V_HEAD)), KV_LORA ** -0.5),
        'o_qn_norm': gain((no, QK_NOPE)),
        'o_qr_norm': gain((no, QK_ROPE)),
        'o_kn_norm': gain((no, QK_NOPE)),
        'o_kr_norm': gain((no, QK_ROPE)),
        'o_w_out': nrm((no, C_HEADS * V_HEAD, d), (C_HEADS * V_HEAD) ** -0.5),
    }


def reference(x, c, ctx, c_ctx, ada_w, ada_b, norm1, norm2, mlp_w1, mlp_w2,
              e_w_in, e_q_norm, e_k_norm, e_lam_q1, e_lam_k1, e_lam_q2, e_lam_k2, e_subln, e_conv,
              e_alog_f, e_alog_b, e_dtb_f, e_dtb_b, e_o_norm, e_w_out,
              o_w_in, o_qa_norm, o_w_uq, o_kva_norm, o_w_ukv, o_qn_norm, o_qr_norm, o_kn_norm, o_kr_norm,
              o_w_out):
    n_lat = x.shape[1]
    cos_a, sin_a = axial_rope_tables(n_lat, A_SUB)
    cos_c, sin_c = axial_rope_tables(n_lat, QK_ROPE)
    silu_c = jax.nn.silu(c)
    silu_cc = jax.nn.silu(c_ctx)[None, :]
    xc = ctx
    for i in range(DEPTH):
        want_ctx = i < DEPTH - 1
        j = i // 2
        mod = silu_c @ ada_w[i] + ada_b[i]
        mod_c = silu_cc @ ada_w[i] + ada_b[i]
        if i % 2 == 0:
            P = dict(w_in=e_w_in[j], q_norm=e_q_norm[j], k_norm=e_k_norm[j],
                     lam_q1=e_lam_q1[j], lam_k1=e_lam_k1[j], lam_q2=e_lam_q2[j], lam_k2=e_lam_k2[j],
                     subln=e_subln[j], conv=e_conv[j], alog_f=e_alog_f[j], alog_b=e_alog_b[j],
                     dtb_f=e_dtb_f[j], dtb_b=e_dtb_b[j], o_norm=e_o_norm[j], w_out=e_w_out[j])
            mixer = functools.partial(even_mixer, P=P, cos=cos_a, sin=sin_a,
                                      lam_init=0.8 - 0.6 * math.exp(-0.3 * i))
        else:
            P = dict(w_in=o_w_in[j], qa_norm=o_qa_norm[j], w_uq=o_w_uq[j], kva_norm=o_kva_norm[j],
                     w_ukv=o_w_ukv[j], qn_norm=o_qn_norm[j], qr_norm=o_qr_norm[j],
                     kn_norm=o_kn_norm[j], kr_norm=o_kr_norm[j], w_out=o_w_out[j])
            mixer = functools.partial(odd_mixer, P=P, cos=cos_c, sin=sin_c)
        x, xc = residual_block(x, xc, mod, mod_c, norm1[i], norm2[i], mlp_w1[i], mlp_w2[i], mixer, want_ctx)
    return x
```

```python
import functools
import math

import numpy as np
import jax
import jax.numpy as jnp
from jax import lax
from jax.experimental import pallas as pl
from jax.experimental.pallas import tpu as pltpu

EPS = 1e-6
ROPE_BASE = 10000.0
GRID_W = 64
HEAD_DIM = 128
A_SUB = 64
CHUNK = 64
CONV_K = 5
QK_NOPE = 128
QK_ROPE = 64
V_HEAD = 128
Q_LORA = 512
KV_LORA = 512
LANES = 128
CONV_PAD = 8
VMEM_LIMIT = 56 * 1024 * 1024

F32 = jnp.float32
BF16 = jnp.bfloat16
NT_DIMS = (((1,), (1,)), ((), ()))
TN_DIMS = (((0,), (0,)), ((), ()))


def _params(sem):
    return pltpu.CompilerParams(dimension_semantics=sem, vmem_limit_bytes=VMEM_LIMIT)


def _dot(a, b):
    return jnp.dot(a.astype(BF16), b.astype(BF16), preferred_element_type=F32)


def _dot_nt(a, b):
    return lax.dot_general(a.astype(BF16), b.astype(BF16), NT_DIMS, preferred_element_type=F32)


def _dot_tn(a, b):
    return lax.dot_general(a.astype(BF16), b.astype(BF16), TN_DIMS, preferred_element_type=F32)


def _rms(x, gain):
    return x * lax.rsqrt(jnp.mean(x * x, axis=-1, keepdims=True) + EPS) * gain


def _pick_tile(n, cap):
    t = min(n, cap)
    while n % t or t % 8:
        t -= 8
    return t


def _mod_kernel(s_ref, w_ref, b_ref, o_ref):
    s = s_ref[...]
    s = s * jax.nn.sigmoid(s)
    o_ref[...] = _dot(s, w_ref[...]) + b_ref[...]


def _adaln_mod(svec, ada_w, ada_b):
    depth, d, n = ada_w.shape
    tn = _pick_tile(n, 1024)
    return pl.pallas_call(
        _mod_kernel,
        out_shape=jax.ShapeDtypeStruct((depth, 8, n), F32),
        grid=(depth, n // tn),
        in_specs=[
            pl.BlockSpec((8, d), lambda l, j: (0, 0)),
            pl.BlockSpec((None, d, tn), lambda l, j: (l, 0, j)),
            pl.BlockSpec((None, 1, tn), lambda l, j: (l, 0, j)),
        ],
        out_specs=pl.BlockSpec((None, 8, tn), lambda l, j: (l, 0, j)),
        compiler_params=_params(("parallel", "parallel")),
        name="adaln_mod",
    )(svec, ada_w, ada_b.reshape(depth, 1, n))


def _norm_matmul_kernel(*refs, tm, n_lat, shift_row, scale_row, modulate, relu2):
    if modulate:
        x_ref, g_ref, mb_ref, mc_ref, w_ref, o_ref, h_scr = refs
    else:
        x_ref, g_ref, w_ref, o_ref, h_scr = refs

    @pl.when(pl.program_id(2) == 0)
    def _():
        y = _rms(x_ref[...], g_ref[...])
        if modulate:
            t = pl.program_id(1) * tm + lax.broadcasted_iota(jnp.int32, (tm, 1), 0)
            is_ctx = t >= n_lat
            sc = jnp.where(is_ctx, mc_ref[scale_row:scale_row + 1, :], mb_ref[scale_row:scale_row + 1, :])
            sh = jnp.where(is_ctx, mc_ref[shift_row:shift_row + 1, :], mb_ref[shift_row:shift_row + 1, :])
            y = y * (1.0 + sc) + sh
        h_scr[...] = y.astype(BF16)

    acc = jnp.dot(h_scr[...], w_ref[...], preferred_element_type=F32)
    if relu2:
        acc = jnp.square(jnp.maximum(acc, 0.0))
    o_ref[...] = acc.astype(o_ref.dtype)


def _norm_matmul(x, col_block, k, gain, w, *, mod=None, which=0, n_lat=None, relu2=False,
                 out_dtype=F32, tm_cap=768, tn_cap=512, name="norm_matmul"):
    bsz, rows, _ = x.shape
    n = w.shape[1]
    tm = _pick_tile(rows, tm_cap)
    tn = _pick_tile(n, tn_cap)
    modulate = mod is not None
    if n_lat is None:
        n_lat = rows
    in_specs = [
        pl.BlockSpec((None, tm, k), lambda b, i, j: (b, i, col_block)),
        pl.BlockSpec((1, k), lambda b, i, j: (0, 0)),
    ]
    args = [x, gain.reshape(1, k)]
    if modulate:
        in_specs += [
            pl.BlockSpec((None, 6, k), lambda b, i, j: (b, 0, 0)),
            pl.BlockSpec((None, 6, k), lambda b, i, j: (bsz, 0, 0)),
        ]
        args += [mod, mod]
    in_specs.append(pl.BlockSpec((k, tn), lambda b, i, j: (0, j)))
    args.append(w)
    kern = functools.partial(_norm_matmul_kernel, tm=tm, n_lat=n_lat, shift_row=3 * which,
                             scale_row=3 * which + 1, modulate=modulate, relu2=relu2)
    return pl.pallas_call(
        kern,
        out_shape=jax.ShapeDtypeStruct((bsz, rows, n), out_dtype),
        grid=(bsz, rows // tm, n // tn),
        in_specs=in_specs,
        out_specs=pl.BlockSpec((None, tm, tn), lambda b, i, j: (b, i, j)),
        scratch_shapes=[pltpu.VMEM((tm, k), BF16)],
        compiler_params=_params(("parallel", "parallel", "arbitrary")),
        name=name,
    )(*args)


def _gate(mb_ref, mc_ref, gate_row, i, tm, n_lat):
    t = i * tm + lax.broadcasted_iota(jnp.int32, (tm, 1), 0)
    return jnp.where(t >= n_lat, mc_ref[gate_row:gate_row + 1, :], mb_ref[gate_row:gate_row + 1, :])


def _out_proj_kernel(a1_ref, a2_ref, w_ref, r_ref, mb_ref, mc_ref, o_ref, *, tm, n_lat, k1, gate_row):
    acc = jnp.dot(a1_ref[...], w_ref[:k1, :], preferred_element_type=F32)
    acc += jnp.dot(a2_ref[...], w_ref[k1:, :], preferred_element_type=F32)
    g = _gate(mb_ref, mc_ref, gate_row, pl.program_id(1), tm, n_lat)
    o_ref[...] = r_ref[...] + g * acc


def _out_proj(a1, c1, a2, c2, w, res, mod, *, gate_row, n_lat, rows, tm_cap=768, tn_cap=512,
              name="out_proj"):
    bsz = res.shape[0]
    d = w.shape[1]
    k1 = w.shape[0] // 2
    tm = _pick_tile(rows, tm_cap)
    tn = _pick_tile(d, tn_cap)
    kern = functools.partial(_out_proj_kernel, tm=tm, n_lat=n_lat, k1=k1, gate_row=gate_row)
    return pl.pallas_call(
        kern,
        out_shape=jax.ShapeDtypeStruct((bsz, rows, d), F32),
        grid=(bsz, rows // tm, d // tn),
        in_specs=[
            pl.BlockSpec((None, tm, k1), lambda b, i, j: (b, i, c1)),
            pl.BlockSpec((None, tm, k1), lambda b, i, j: (b, i, c2)),
            pl.BlockSpec((2 * k1, tn), lambda b, i, j: (0, j)),
            pl.BlockSpec((None, tm, tn), lambda b, i, j: (b, i, j)),
            pl.BlockSpec((None, 6, tn), lambda b, i, j: (b, 0, j)),
            pl.BlockSpec((None, 6, tn), lambda b, i, j: (bsz, 0, j)),
        ],
        out_specs=pl.BlockSpec((None, tm, tn), lambda b, i, j: (b, i, j)),
        compiler_params=_params(("parallel", "parallel", "arbitrary")),
        name=name,
    )(a1, a2, w, res, mod, mod)


def _down_proj_kernel(a_ref, w_ref, r_ref, mb_ref, mc_ref, o_ref, acc_scr, *, tm, n_lat, gate_row):
    kk = pl.program_id(3)

    @pl.when(kk == 0)
    def _():
        acc_scr[...] = jnp.zeros_like(acc_scr)

    acc_scr[...] += jnp.dot(a_ref[...], w_ref[...], preferred_element_type=F32)

    @pl.when(kk == pl.num_programs(3) - 1)
    def _():
        g = _gate(mb_ref, mc_ref, gate_row, pl.program_id(1), tm, n_lat)
        o_ref[...] = r_ref[...] + g * acc_scr[...]


def _down_proj(a, w, res, mod, *, gate_row, n_lat, tm_cap=768, tn_cap=512, tk=2048, name="down_proj"):
    bsz, rows, kdim = a.shape
    d = w.shape[1]
    tm = _pick_tile(rows, tm_cap)
    tn = _pick_tile(d, tn_cap)
    kern = functools.partial(_down_proj_kernel, tm=tm, n_lat=n_lat, gate_row=gate_row)
    return pl.pallas_call(
        kern,
        out_shape=jax.ShapeDtypeStruct((bsz, rows, d), F32),
        grid=(bsz, rows // tm, d // tn, kdim // tk),
        in_specs=[
            pl.BlockSpec((None, tm, tk), lambda b, i, j, k: (b, i, k)),
            pl.BlockSpec((tk, tn), lambda b, i, j, k: (k, j)),
            pl.BlockSpec((None, tm, tn), lambda b, i, j, k: (b, i, j)),
            pl.BlockSpec((None, 6, tn), lambda b, i, j, k: (b, 0, j)),
            pl.BlockSpec((None, 6, tn), lambda b, i, j, k: (bsz, 0, j)),
        ],
        out_specs=pl.BlockSpec((None, tm, tn), lambda b, i, j, k: (b, i, j)),
        scratch_shapes=[pltpu.VMEM((tm, tn), F32)],
        compiler_params=_params(("parallel", "parallel", "parallel", "arbitrary")),
        name=name,
    )(a, w, res, mod, mod)


def _rope_tables(n_lat, n_ctx):
    t = np.arange(n_lat)
    row = (t // GRID_W).astype(np.float64)
    col = (t % GRID_W).astype(np.float64)
    n_freq = A_SUB // 4
    inv_freq = ROPE_BASE ** (-np.arange(n_freq, dtype=np.float64) / n_freq)
    ang = np.concatenate([row[:, None] * inv_freq, col[:, None] * inv_freq], axis=-1)
    cos = np.concatenate([np.cos(ang), np.ones((n_ctx, A_SUB // 2))], axis=0)
    sin = np.concatenate([np.sin(ang), np.zeros((n_ctx, A_SUB // 2))], axis=0)
    cos128 = np.tile(cos, (1, 4))
    sin128 = np.tile(np.concatenate([-sin, sin], axis=1), (1, 2))
    return jnp.asarray(cos128, F32), jnp.asarray(sin128, F32)


def _rope128(x, cos, sin):
    lane = lax.broadcasted_iota(jnp.int32, x.shape, 1)
    first = (lane % A_SUB) < (A_SUB // 2)
    partner = jnp.where(first, pltpu.roll(x, LANES - A_SUB // 2, 1), pltpu.roll(x, A_SUB // 2, 1))
    return x * cos + partner * sin


def _group_rms(x, gain):
    lane = lax.broadcasted_iota(jnp.int32, x.shape, 1)
    lo = lane < A_SUB
    xx = x * x
    s_lo = jnp.sum(jnp.where(lo, xx, 0.0), axis=-1, keepdims=True)
    s_hi = jnp.sum(jnp.where(lo, 0.0, xx), axis=-1, keepdims=True)
    inv = jnp.where(lo, lax.rsqrt(s_lo / A_SUB + EPS), lax.rsqrt(s_hi / A_SUB + EPS))
    return x * inv * gain


def _softmax_parts(s):
    m = jnp.max(s, axis=-1, keepdims=True)
    e = jnp.exp(s - m)
    return e, jnp.sum(e, axis=-1, keepdims=True)


def _diff_attn_kernel(lam_ref, qg_ref, kg_ref, sg_ref, cq_ref, sq_ref, ck_ref, sk_ref,
                      q_ref, k_ref, v_ref, o_ref, kb_scr, vb_scr, *, n_lat, lam_init, tq):
    qi = pl.program_id(2)

    @pl.when(qi == 0)
    def _():
        k = _rope128(_group_rms(k_ref[...], kg_ref[...]), ck_ref[...], sk_ref[...])
        kb_scr[...] = k.astype(BF16)
        vb_scr[...] = v_ref[...].astype(BF16)

    lp = lam_ref[...]
    lam = (jnp.exp(jnp.sum(lp[0:1] * lp[1:2], axis=-1, keepdims=True))
           - jnp.exp(jnp.sum(lp[2:3] * lp[3:4], axis=-1, keepdims=True)) + lam_init)

    q = _rope128(_group_rms(q_ref[...], qg_ref[...]), cq_ref[...], sq_ref[...]) * (A_SUB ** -0.5)
    lane = lax.broadcasted_iota(jnp.int32, q.shape, 1)
    q1 = jnp.where(lane < A_SUB, q, 0.0).astype(BF16)
    q2 = jnp.where(lane < A_SUB, 0.0, q).astype(BF16)

    def attend(kb, vb):
        e1, l1 = _softmax_parts(lax.dot_general(q1, kb, NT_DIMS, preferred_element_type=F32))
        e2, l2 = _softmax_parts(lax.dot_general(q2, kb, NT_DIMS, preferred_element_type=F32))
        p = e1 * (1.0 / l1) - e2 * (lam / l2)
        o = jnp.dot(p.astype(BF16), vb, preferred_element_type=F32)
        o_ref[...] = (_rms(o, sg_ref[...]) * (1.0 - lam_init)).astype(o_ref.dtype)

    @pl.when(qi * tq < n_lat)
    def _():
        attend(kb_scr[...], vb_scr[...])

    @pl.when(qi * tq >= n_lat)
    def _():
        attend(kb_scr[n_lat:, :], vb_scr[n_lat:, :])


def _diff_attn(proj, lam_p, q_gain, k_gain, sub_gain, cos, sin, *, n_heads, n_lat, lam_init, tq=256):
    bsz, rows, _ = proj.shape
    kern = functools.partial(_diff_attn_kernel, n_lat=n_lat, lam_init=lam_init, tq=tq)
    vec = lambda: pl.BlockSpec((1, LANES), lambda b, h, i: (0, 0))
    return pl.pallas_call(
        kern,
        out_shape=jax.ShapeDtypeStruct((bsz, rows, n_heads * HEAD_DIM), BF16),
        grid=(bsz, n_heads, rows // tq),
        in_specs=[
            pl.BlockSpec((4, A_SUB), lambda b, h, i: (0, 0)),
            vec(), vec(), vec(),
            pl.BlockSpec((tq, LANES), lambda b, h, i: (i, 0)),
            pl.BlockSpec((tq, LANES), lambda b, h, i: (i, 0)),
            pl.BlockSpec((rows, LANES), lambda b, h, i: (0, 0)),
            pl.BlockSpec((rows, LANES), lambda b, h, i: (0, 0)),
            pl.BlockSpec((None, tq, HEAD_DIM), lambda b, h, i: (b, i, h)),
            pl.BlockSpec((None, rows, HEAD_DIM), lambda b, h, i: (b, 0, n_heads + h)),
            pl.BlockSpec((None, rows, HEAD_DIM), lambda b, h, i: (b, 0, 2 * n_heads + h)),
        ],
        out_specs=pl.BlockSpec((None, tq, HEAD_DIM), lambda b, h, i: (b, i, h)),
        scratch_shapes=[pltpu.VMEM((rows, HEAD_DIM), BF16), pltpu.VMEM((rows, HEAD_DIM), BF16)],
        compiler_params=_params(("parallel", "parallel", "arbitrary")),
        name="diff_attn",
    )(lam_p, q_gain, k_gain, sub_gain, cos, sin, cos, sin, proj, proj, proj)


def _col(row_vec, eye):
    return jnp.sum(jnp.where(eye, row_vec, 0.0), axis=-1, keepdims=True)


def _gdn_kernel(prm_ref, gates_ref, cwq_ref, cwk_ref, cwv_ref, og_ref, xq_ref, xk_ref, xv_ref, z_ref,
                o_ref, pad_scr, q_scr, k_scr, v_scr, w_scr, u_scr, qd_scr, kd_scr, qk_scr, gt_scr, oacc_scr,
                *, rows, n_lat, tile):
    h = pl.program_id(1)
    n_chunks = rows // CHUNK
    lat_chunks = n_lat // CHUNK
    n_tiles = rows // tile

    pad_scr[0:CONV_PAD, :] = jnp.zeros((CONV_PAD, HEAD_DIM), F32)
    pad_scr[CONV_PAD + rows:, :] = jnp.zeros((CONV_PAD, HEAD_DIM), F32)
    for x_ref, cw_ref, dst, kind in ((xq_ref, cwq_ref, q_scr, "q"), (xk_ref, cwk_ref, k_scr, "k"),
                                     (xv_ref, cwv_ref, v_scr, "v")):
        pad_scr[CONV_PAD:CONV_PAD + rows, :] = x_ref[...]
        cw = cw_ref[...]
        for r in range(n_tiles):
            r0 = r * tile
            t = r0 + lax.broadcasted_iota(jnp.int32, (tile, 1), 0)
            acc = jnp.zeros((tile, HEAD_DIM), F32)
            for j in range(CONV_K):
                dd = j - CONV_K // 2
                xs = pad_scr[CONV_PAD + r0 + dd:CONV_PAD + r0 + dd + tile, :]
                if dd > 0 and r0 < n_lat <= r0 + tile + dd and n_lat < rows:
                    xs = jnp.where((t < n_lat) & (t + dd >= n_lat), 0.0, xs)
                if dd < 0 and r0 + dd < n_lat <= r0 + tile and n_lat < rows:
                    xs = jnp.where((t >= n_lat) & (t + dd < n_lat), 0.0, xs)
                acc = acc + xs * cw[j:j + 1, :]
            y = acc * jax.nn.sigmoid(acc)
            if kind != "v":
                y = y * lax.rsqrt(jnp.sum(y * y, axis=-1, keepdims=True) + EPS)
            if kind == "q":
                y = y * (HEAD_DIM ** -0.5)
            dst[r0:r0 + tile, :] = y

    ii = lax.broadcasted_iota(jnp.int32, (CHUNK, CHUNK), 0)
    jj = lax.broadcasted_iota(jnp.int32, (CHUNK, CHUNK), 1)
    eye = ii == jj
    eye_f = jnp.where(eye, 1.0, 0.0)

    def local_body(c, carry):
        r0 = pl.multiple_of(c * CHUNK, CHUNK)
        kc = k_scr[pl.ds(r0, CHUNK), :]
        vc = v_scr[pl.ds(r0, CHUNK), :]
        qc = q_scr[pl.ds(r0, CHUNK), :]
        kk = _dot_nt(kc, kc)
        qk = _dot_nt(qc, kc)
        for d in range(2):
            alog = jnp.full((1, CHUNK), prm_ref[d, h], F32)
            dtb = jnp.full((1, CHUNK), prm_ref[2 + d, h], F32)
            g = -jnp.exp(alog) * jax.nn.softplus(gates_ref[d, c] + dtb)
            beta = jax.nn.sigmoid(gates_ref[2 + d, c])
            incl = (ii <= jj) if d == 0 else (ii >= jj)
            big_g = jnp.dot(g, jnp.where(incl, 1.0, 0.0), precision=lax.Precision.HIGHEST,
                            preferred_element_type=F32)
            g_col = _col(big_g, eye)
            beta_col = _col(beta, eye)
            after = (ii >= jj) if d == 0 else (ii <= jj)
            strict = (ii > jj) if d == 0 else (ii < jj)
            gamma = jnp.exp(jnp.where(after, g_col - big_g, -jnp.inf))
            a_mat = jnp.where(strict, beta_col * kk * gamma, 0.0)
            p = -a_mat
            t_inv = eye_f + p
            for _ in range(int(math.log2(CHUNK)) - 1):
                p = _dot(p, p)
                t_inv = t_inv + _dot(t_inv, p)
            e_col = jnp.exp(g_col)
            g_end = big_g[:, CHUNK - 1:CHUNK] if d == 0 else big_g[:, 0:1]
            w_scr[d, pl.ds(r0, CHUNK), :] = _dot(t_inv, kc * (beta_col * e_col)).astype(BF16)
            u_scr[d, pl.ds(r0, CHUNK), :] = _dot(t_inv, vc * beta_col)
            qd_scr[d, pl.ds(r0, CHUNK), :] = (qc * e_col).astype(BF16)
            kd_scr[d, pl.ds(r0, CHUNK), :] = (kc * jnp.exp(g_end - g_col)).astype(BF16)
            qk_scr[d, pl.ds(r0, CHUNK), :] = (qk * gamma).astype(BF16)
            gt_scr[d, c] = jnp.broadcast_to(jnp.exp(g_end), (1, HEAD_DIM))
        return carry

    lax.fori_loop(0, n_chunks, local_body, 0)

    def scan_body(s, states):
        new_states = []
        for d in range(2):
            c = lax.rem(s + lat_chunks, n_chunks) if d == 0 else n_chunks - 1 - s
            r0 = pl.multiple_of(c * CHUNK, CHUNK)
            st = states[d]
            sb = st.astype(BF16)
            v_new = u_scr[d, pl.ds(r0, CHUNK), :] - jnp.dot(w_scr[d, pl.ds(r0, CHUNK), :], sb,
                                                            preferred_element_type=F32)
            vb = v_new.astype(BF16)
            o = (jnp.dot(qd_scr[d, pl.ds(r0, CHUNK), :], sb, preferred_element_type=F32)
                 + jnp.dot(qk_scr[d, pl.ds(r0, CHUNK), :], vb, preferred_element_type=F32))
            oacc_scr[d, pl.ds(r0, CHUNK), :] = o
            new_states.append(gt_scr[d, c] * st + lax.dot_general(
                kd_scr[d, pl.ds(r0, CHUNK), :], vb, TN_DIMS, preferred_element_type=F32))
        return tuple(new_states)

    zero = jnp.zeros((HEAD_DIM, HEAD_DIM), F32)
    lax.fori_loop(0, n_chunks, scan_body, (zero, zero))

    for r in range(n_tiles):
        r0 = r * tile
        o = oacc_scr[0, r0:r0 + tile, :] + oacc_scr[1, r0:r0 + tile, :]
        z = z_ref[r0:r0 + tile, :]
        o_ref[r0:r0 + tile, :] = (_rms(o, og_ref[...]) * (z * jax.nn.sigmoid(z))).astype(o_ref.dtype)


def _gdn(proj, gates, prm, conv_w, o_gain, *, n_heads, col0, n_lat, tile=256):
    bsz, rows, _ = proj.shape
    n_chunks = rows // CHUNK
    kern = functools.partial(_gdn_kernel, rows=rows, n_lat=n_lat, tile=tile)
    seq = lambda off: pl.BlockSpec((None, rows, HEAD_DIM), lambda b, h: (b, 0, col0 + off * n_heads + h))
    cw = lambda off: pl.BlockSpec((CONV_K, HEAD_DIM), lambda b, h: (0, off * n_heads + h))
    return pl.pallas_call(
        kern,
        out_shape=jax.ShapeDtypeStruct((bsz, rows, n_heads * HEAD_DIM), BF16),
        grid=(bsz, n_heads),
        in_specs=[
            pl.BlockSpec(memory_space=pltpu.SMEM),
            pl.BlockSpec((None, 4, None, n_chunks, 1, CHUNK), lambda b, h: (b, 0, h, 0, 0, 0)),
            cw(0), cw(1), cw(2),
            pl.BlockSpec((1, HEAD_DIM), lambda b, h: (0, 0)),
            seq(0), seq(1), seq(2), seq(3),
        ],
        out_specs=pl.BlockSpec((None, rows, HEAD_DIM), lambda b, h: (b, 0, h)),
        scratch_shapes=[
            pltpu.VMEM((rows + 2 * CONV_PAD, HEAD_DIM), F32),
            pltpu.VMEM((rows, HEAD_DIM), F32),
            pltpu.VMEM((rows, HEAD_DIM), F32),
            pltpu.VMEM((rows, HEAD_DIM), F32),
            pltpu.VMEM((2, rows, HEAD_DIM), BF16),
            pltpu.VMEM((2, rows, HEAD_DIM), F32),
            pltpu.VMEM((2, rows, HEAD_DIM), BF16),
            pltpu.VMEM((2, rows, HEAD_DIM), BF16),
            pltpu.VMEM((2, rows, CHUNK), BF16),
            pltpu.VMEM((2, n_chunks, 1, HEAD_DIM), F32),
            pltpu.VMEM((2, rows, HEAD_DIM), F32),
        ],
        compiler_params=_params(("parallel", "arbitrary")),
        name="gdn",
    )(prm, gates, conv_w, conv_w, conv_w, o_gain, proj, proj, proj, proj)


def _mla_kernel(qng_ref, qrg_ref, kng_ref, krg_ref, cq_ref, sq_ref, ck_ref, sk_ref,
                qn_ref, qr_ref, kn_ref, v_ref, kr_ref, o_ref, kb_scr, vb_scr, *, scale):
    h = pl.program_id(1)

    @pl.when(pl.program_id(2) == 0)
    def _():
        kb_scr[:, :QK_NOPE] = _rms(kn_ref[...], kng_ref[...]).astype(BF16)
        kr = _rope128(_rms(kr_ref[...], krg_ref[...]), ck_ref[...], sk_ref[...])
        kb_scr[:, QK_NOPE:] = kr.astype(BF16)
        vb_scr[...] = v_ref[...].astype(BF16)

    qn = _rms(qn_ref[...], qng_ref[...])
    qr = _rope128(_group_rms(qr_ref[...], qrg_ref[...]), cq_ref[...], sq_ref[...])
    lane = lax.broadcasted_iota(jnp.int32, qr.shape, 1)
    mine = (lane // QK_ROPE) == (h % 2)
    q = jnp.concatenate([qn, jnp.where(mine, qr, 0.0)], axis=-1).astype(BF16)
    s = lax.dot_general(q, kb_scr[...], NT_DIMS, preferred_element_type=F32) * scale
    e, l = _softmax_parts(s)
    p = e * (1.0 / l)
    o_ref[...] = jnp.dot(p.astype(BF16), vb_scr[...], preferred_element_type=F32).astype(o_ref.dtype)


def _mla_attn(q, kv, proj, qn_gain, qr_gain, kn_gain, kr_gain, cos, sin, *, n_heads, n_lat, kr_block, tq=256):
    bsz, rows, _ = kv.shape
    kern = functools.partial(_mla_kernel, scale=(QK_NOPE + QK_ROPE) ** -0.5)
    vec = lambda: pl.BlockSpec((1, LANES), lambda b, h, i: (0, 0))
    return pl.pallas_call(
        kern,
        out_shape=jax.ShapeDtypeStruct((bsz, n_lat, n_heads * V_HEAD), BF16),
        grid=(bsz, n_heads, n_lat // tq),
        in_specs=[
            vec(), vec(), vec(), vec(),
            pl.BlockSpec((tq, LANES), lambda b, h, i: (i, 0)),
            pl.BlockSpec((tq, LANES), lambda b, h, i: (i, 0)),
            pl.BlockSpec((rows, LANES), lambda b, h, i: (0, 0)),
            pl.BlockSpec((rows, LANES), lambda b, h, i: (0, 0)),
            pl.BlockSpec((None, tq, QK_NOPE), lambda b, h, i: (b, i, h)),
            pl.BlockSpec((None, tq, LANES), lambda b, h, i: (b, i, n_heads + h // 2)),
            pl.BlockSpec((None, rows, QK_NOPE), lambda b, h, i: (b, 0, 2 * h)),
            pl.BlockSpec((None, rows, V_HEAD), lambda b, h, i: (b, 0, 2 * h + 1)),
            pl.BlockSpec((None, rows, LANES), lambda b, h, i: (b, 0, kr_block)),
        ],
        out_specs=pl.BlockSpec((None, tq, V_HEAD), lambda b, h, i: (b, i, h)),
        scratch_shapes=[pltpu.VMEM((rows, QK_NOPE + LANES), BF16), pltpu.VMEM((rows, V_HEAD), BF16)],
        compiler_params=_params(("parallel", "parallel", "arbitrary")),
        name="mla_attn",
    )(qn_gain, qr_gain, kn_gain, kr_gain, cos, sin, cos, sin, q, q, kv, kv, proj)


def _pad_cols(w, n):
    return jnp.pad(w, ((0, 0), (0, n - w.shape[1])))


def kernel(x, c, ctx, c_ctx, ada_w, ada_b, norm1, norm2, mlp_w1, mlp_w2, e_w_in, e_q_norm, e_k_norm, e_lam_q1, e_lam_k1, e_lam_q2, e_lam_k2, e_subln, e_conv, e_alog_f, e_alog_b, e_dtb_f, e_dtb_b, e_o_norm, e_w_out, o_w_in, o_qa_norm, o_w_uq, o_kva_norm, o_w_ukv, o_qn_norm, o_qr_norm, o_kn_norm, o_kr_norm, o_w_out):
    bsz, n_lat, d = x.shape
    n_ctx = ctx.shape[1]
    rows = n_lat + n_ctx
    depth = ada_w.shape[0]
    assert depth == 2 and bsz < 8
    mix_heads = d // HEAD_DIM
    a_heads = mix_heads // 2
    b_heads = mix_heads - a_heads
    c_heads = d // V_HEAD
    dup = lambda g: jnp.tile(g.reshape(1, -1), (1, 2))

    svec = jnp.concatenate([c, c_ctx[None, :], jnp.zeros((8 - bsz - 1, d), F32)], axis=0)
    mod = _adaln_mod(svec, ada_w, ada_b).reshape(depth, 8, 6, d)

    cos, sin = _rope_tables(n_lat, n_ctx)
    xs = jnp.concatenate([x, ctx], axis=1)

    a_qk = a_heads * HEAD_DIM
    main = 7 * a_qk
    n_gate = 4 * b_heads
    w_in0 = _pad_cols(e_w_in[0], -(-(main + n_gate) // 512) * 512)
    proj = _norm_matmul(xs, 0, d, norm1[0], w_in0.astype(BF16), mod=mod[0], which=0, n_lat=n_lat,
                        name="in_proj0")
    lam_p = jnp.concatenate([e_lam_q1, e_lam_k1, e_lam_q2, e_lam_k2], axis=0)
    o_a = _diff_attn(proj, lam_p, dup(e_q_norm[0]), dup(e_k_norm[0]), e_subln[0].reshape(1, -1), cos, sin,
                     n_heads=a_heads, n_lat=n_lat, lam_init=0.8 - 0.6 * math.exp(-0.3 * 0))
    gates = proj[:, :, main:main + n_gate].reshape(bsz, rows // CHUNK, CHUNK, 4, b_heads)
    gates = gates.transpose(0, 3, 4, 1, 2).reshape(bsz, 4, b_heads, rows // CHUNK, 1, CHUNK)
    prm = jnp.concatenate([e_alog_f, e_alog_b, e_dtb_f, e_dtb_b], axis=0)
    o_b = _gdn(proj, gates, prm, e_conv[0], e_o_norm[0].reshape(1, -1), n_heads=b_heads,
               col0=3 * a_heads, n_lat=n_lat)
    xs = _out_proj(o_a, 0, o_b, 0, e_w_out[0].astype(BF16), xs, mod[0], gate_row=2, n_lat=n_lat, rows=rows,
                   name="out_proj0")
    hid = _norm_matmul(xs, 0, d, norm2[0], mlp_w1[0].astype(BF16), mod=mod[0], which=1, n_lat=n_lat,
                       relu2=True, out_dtype=BF16, name="mlp_up0")
    xs = _down_proj(hid, mlp_w2[0].astype(BF16), xs, mod[0], gate_row=5, n_lat=n_lat, name="mlp_down0")

    w_in1 = jnp.concatenate([o_w_in[0], o_w_in[0][:, Q_LORA + KV_LORA:]], axis=1)
    proj = _norm_matmul(xs, 0, d, norm1[1], w_in1.astype(BF16), mod=mod[1], which=0, n_lat=n_lat,
                        tn_cap=w_in1.shape[1], name="in_proj1")
    w_uq = o_w_uq[0].reshape(Q_LORA, c_heads, QK_NOPE + QK_ROPE)
    w_uq = jnp.concatenate([w_uq[:, :, :QK_NOPE].reshape(Q_LORA, -1), w_uq[:, :, QK_NOPE:].reshape(Q_LORA, -1)],
                           axis=1)
    q = _norm_matmul(proj, 0, Q_LORA, o_qa_norm[0], w_uq.astype(BF16), tn_cap=1024, name="q_up")
    kv = _norm_matmul(proj, 1, KV_LORA, o_kva_norm[0], o_w_ukv[0].astype(BF16), tn_cap=1024, name="kv_up")
    o_c = _mla_attn(q, kv, proj, o_qn_norm[0].reshape(1, -1), dup(o_qr_norm[0]), o_kn_norm[0].reshape(1, -1),
                    dup(o_kr_norm[0]), cos, sin, n_heads=c_heads, n_lat=n_lat,
                    kr_block=(Q_LORA + KV_LORA) // LANES)
    xl = _out_proj(o_c, 0, o_c, 1, o_w_out[0].astype(BF16), xs, mod[1], gate_row=2, n_lat=n_lat, rows=n_lat,
                   tm_cap=1024, name="out_proj1")
    hid = _norm_matmul(xl, 0, d, norm2[1], mlp_w1[1].astype(BF16), mod=mod[1], which=1, n_lat=n_lat,
                       relu2=True, out_dtype=BF16, tm_cap=1024, name="mlp_up1")
    return _down_proj(hid, mlp_w2[1].astype(BF16), xl, mod[1], gate_row=5, n_lat=n_lat, tm_cap=1024,
                      name="mlp_down1")
```

```python
import functools
import math

import numpy as np
import jax
import jax.numpy as jnp
from jax import lax
from jax.experimental import pallas as pl
from jax.experimental.pallas import tpu as pltpu

EPS = 1e-6
ROPE_BASE = 10000.0
GRID_W = 64
HEAD_DIM = 128
A_SUB = 64
CHUNK = 64
CONV_K = 5
QK_NOPE = 128
QK_ROPE = 64
V_HEAD = 128
Q_LORA = 512
KV_LORA = 512
LANES = 128
CONV_PAD = 8
KEY_TILE = 256
Q_SUB = 128
GROUP = 256
LOG2E = math.log2(math.e)
VMEM_LIMIT = 56 * 1024 * 1024

F32 = jnp.float32
BF16 = jnp.bfloat16
NT_DIMS = (((1,), (1,)), ((), ()))
TN_DIMS = (((0,), (0,)), ((), ()))


def _params(sem):
    return pltpu.CompilerParams(dimension_semantics=sem, vmem_limit_bytes=VMEM_LIMIT)


def _dot(a, b):
    return jnp.dot(a.astype(BF16), b.astype(BF16), preferred_element_type=F32)


def _dot_nt(a, b):
    return lax.dot_general(a.astype(BF16), b.astype(BF16), NT_DIMS, preferred_element_type=F32)


def _dot_tn(a, b):
    return lax.dot_general(a.astype(BF16), b.astype(BF16), TN_DIMS, preferred_element_type=F32)


def _rms(x, gain):
    return x * lax.rsqrt(jnp.mean(x * x, axis=-1, keepdims=True) + EPS) * gain


def _pick_tile(n, cap):
    t = min(n, cap)
    while n % t or t % 8:
        t -= 8
    return t


def _mod_kernel(s_ref, w_ref, b_ref, o_ref):
    s = s_ref[...]
    s = s * jax.nn.sigmoid(s)
    o_ref[...] = _dot(s, w_ref[...]) + b_ref[...]


def _adaln_mod(svec, ada_w, ada_b):
    depth, d, n = ada_w.shape
    tn = _pick_tile(n, 1024)
    return pl.pallas_call(
        _mod_kernel,
        out_shape=jax.ShapeDtypeStruct((depth, 8, n), F32),
        grid=(depth, n // tn),
        in_specs=[
            pl.BlockSpec((8, d), lambda l, j: (0, 0)),
            pl.BlockSpec((None, d, tn), lambda l, j: (l, 0, j)),
            pl.BlockSpec((None, 1, tn), lambda l, j: (l, 0, j)),
        ],
        out_specs=pl.BlockSpec((None, 8, tn), lambda l, j: (l, 0, j)),
        compiler_params=_params(("parallel", "parallel")),
        name="adaln_mod",
    )(svec, ada_w, ada_b.reshape(depth, 1, n))


def _norm_matmul_kernel(*refs, tm, n_lat, shift_row, scale_row, modulate, relu2, side):
    refs = list(refs)
    x_ref, g_ref = refs[:2]
    mb_ref, mc_ref = refs[2:4] if modulate else (None, None)
    rest = refs[4:] if modulate else refs[2:]
    if side:
        w_ref, ws_ref, o_ref, os_ref, h_scr = rest
    else:
        w_ref, o_ref, h_scr = rest

    @pl.when(pl.program_id(2) == 0)
    def _():
        y = _rms(x_ref[...], g_ref[...])
        if modulate:
            t = pl.program_id(1) * tm + lax.broadcasted_iota(jnp.int32, (tm, 1), 0)
            is_ctx = t >= n_lat
            sc = jnp.where(is_ctx, mc_ref[scale_row:scale_row + 1, :], mb_ref[scale_row:scale_row + 1, :])
            sh = jnp.where(is_ctx, mc_ref[shift_row:shift_row + 1, :], mb_ref[shift_row:shift_row + 1, :])
            y = y * (1.0 + sc) + sh
        h_scr[...] = y.astype(BF16)
        if side:
            os_ref[...] = jnp.dot(h_scr[...], ws_ref[...], preferred_element_type=F32)

    acc = jnp.dot(h_scr[...], w_ref[...], preferred_element_type=F32)
    if relu2:
        acc = jnp.square(jnp.maximum(acc, 0.0))
    o_ref[...] = acc.astype(o_ref.dtype)


def _norm_matmul(x, col_block, k, gain, w, *, mod=None, which=0, n_lat=None, relu2=False, w_side=None,
                 out_dtype=F32, tm_cap=768, tn_cap=512, name="norm_matmul"):
    bsz, rows, _ = x.shape
    n = w.shape[1]
    tm = _pick_tile(rows, tm_cap)
    tn = _pick_tile(n, tn_cap)
    modulate = mod is not None
    side = w_side is not None
    if n_lat is None:
        n_lat = rows
    in_specs = [
        pl.BlockSpec((None, tm, k), lambda b, i, j: (b, i, col_block)),
        pl.BlockSpec((1, k), lambda b, i, j: (0, 0)),
    ]
    args = [x, gain.reshape(1, k)]
    if modulate:
        in_specs += [
            pl.BlockSpec((None, 6, k), lambda b, i, j: (b, 0, 0)),
            pl.BlockSpec((None, 6, k), lambda b, i, j: (bsz, 0, 0)),
        ]
        args += [mod, mod]
    in_specs.append(pl.BlockSpec((k, tn), lambda b, i, j: (0, j)))
    args.append(w)
    out_shape = [jax.ShapeDtypeStruct((bsz, rows, n), out_dtype)]
    out_specs = [pl.BlockSpec((None, tm, tn), lambda b, i, j: (b, i, j))]
    if side:
        in_specs.append(pl.BlockSpec((k, LANES), lambda b, i, j: (0, 0)))
        args.append(w_side)
        out_shape.append(jax.ShapeDtypeStruct((bsz, rows, LANES), F32))
        out_specs.append(pl.BlockSpec((None, tm, LANES), lambda b, i, j: (b, i, 0)))
    kern = functools.partial(_norm_matmul_kernel, tm=tm, n_lat=n_lat, shift_row=3 * which,
                             scale_row=3 * which + 1, modulate=modulate, relu2=relu2, side=side)
    out = pl.pallas_call(
        kern,
        out_shape=out_shape,
        grid=(bsz, rows // tm, n // tn),
        in_specs=in_specs,
        out_specs=out_specs,
        scratch_shapes=[pltpu.VMEM((tm, k), BF16)],
        compiler_params=_params(("parallel", "parallel", "arbitrary")),
        name=name,
    )(*args)
    return out if side else out[0]


def _gate(mb_ref, mc_ref, gate_row, i, tm, n_lat):
    t = i * tm + lax.broadcasted_iota(jnp.int32, (tm, 1), 0)
    return jnp.where(t >= n_lat, mc_ref[gate_row:gate_row + 1, :], mb_ref[gate_row:gate_row + 1, :])


def _out_proj_kernel(a1_ref, a2_ref, w_ref, r_ref, mb_ref, mc_ref, o_ref, *, tm, n_lat, k1, gate_row):
    acc = jnp.dot(a1_ref[...], w_ref[:k1, :], preferred_element_type=F32)
    acc += jnp.dot(a2_ref[...], w_ref[k1:, :], preferred_element_type=F32)
    g = _gate(mb_ref, mc_ref, gate_row, pl.program_id(1), tm, n_lat)
    o_ref[...] = r_ref[...] + g * acc


def _out_proj(a1, c1, a2, c2, w, res, mod, *, gate_row, n_lat, rows, tm_cap=768, tn_cap=512,
              name="out_proj"):
    bsz = res.shape[0]
    d = w.shape[1]
    k1 = w.shape[0] // 2
    tm = _pick_tile(rows, tm_cap)
    tn = _pick_tile(d, tn_cap)
    kern = functools.partial(_out_proj_kernel, tm=tm, n_lat=n_lat, k1=k1, gate_row=gate_row)
    return pl.pallas_call(
        kern,
        out_shape=jax.ShapeDtypeStruct((bsz, rows, d), F32),
        grid=(bsz, rows // tm, d // tn),
        in_specs=[
            pl.BlockSpec((None, tm, k1), lambda b, i, j: (b, i, c1)),
            pl.BlockSpec((None, tm, k1), lambda b, i, j: (b, i, c2)),
            pl.BlockSpec((2 * k1, tn), lambda b, i, j: (0, j)),
            pl.BlockSpec((None, tm, tn), lambda b, i, j: (b, i, j)),
            pl.BlockSpec((None, 6, tn), lambda b, i, j: (b, 0, j)),
            pl.BlockSpec((None, 6, tn), lambda b, i, j: (bsz, 0, j)),
        ],
        out_specs=pl.BlockSpec((None, tm, tn), lambda b, i, j: (b, i, j)),
        compiler_params=_params(("parallel", "parallel", "arbitrary")),
        name=name,
    )(a1, a2, w, res, mod, mod)


def _down_proj_kernel(a_ref, w_ref, r_ref, mb_ref, mc_ref, o_ref, acc_scr, *, tm, n_lat, gate_row):
    kk = pl.program_id(3)

    @pl.when(kk == 0)
    def _():
        acc_scr[...] = jnp.zeros_like(acc_scr)

    acc_scr[...] += jnp.dot(a_ref[...], w_ref[...], preferred_element_type=F32)

    @pl.when(kk == pl.num_programs(3) - 1)
    def _():
        g = _gate(mb_ref, mc_ref, gate_row, pl.program_id(1), tm, n_lat)
        o_ref[...] = r_ref[...] + g * acc_scr[...]


def _down_proj(a, w, res, mod, *, gate_row, n_lat, tm_cap=768, tn_cap=512, tk=2048, name="down_proj"):
    bsz, rows, kdim = a.shape
    d = w.shape[1]
    tm = _pick_tile(rows, tm_cap)
    tn = _pick_tile(d, tn_cap)
    kern = functools.partial(_down_proj_kernel, tm=tm, n_lat=n_lat, gate_row=gate_row)
    return pl.pallas_call(
        kern,
        out_shape=jax.ShapeDtypeStruct((bsz, rows, d), F32),
        grid=(bsz, rows // tm, d // tn, kdim // tk),
        in_specs=[
            pl.BlockSpec((None, tm, tk), lambda b, i, j, k: (b, i, k)),
            pl.BlockSpec((tk, tn), lambda b, i, j, k: (k, j)),
            pl.BlockSpec((None, tm, tn), lambda b, i, j, k: (b, i, j)),
            pl.BlockSpec((None, 6, tn), lambda b, i, j, k: (b, 0, j)),
            pl.BlockSpec((None, 6, tn), lambda b, i, j, k: (bsz, 0, j)),
        ],
        out_specs=pl.BlockSpec((None, tm, tn), lambda b, i, j, k: (b, i, j)),
        scratch_shapes=[pltpu.VMEM((tm, tn), F32)],
        compiler_params=_params(("parallel", "parallel", "parallel", "arbitrary")),
        name=name,
    )(a, w, res, mod, mod)


def _rope_tables(n_lat, n_ctx):
    t = np.arange(n_lat)
    row = (t // GRID_W).astype(np.float64)
    col = (t % GRID_W).astype(np.float64)
    n_freq = A_SUB // 4
    inv_freq = ROPE_BASE ** (-np.arange(n_freq, dtype=np.float64) / n_freq)
    ang = np.concatenate([row[:, None] * inv_freq, col[:, None] * inv_freq], axis=-1)
    cos = np.concatenate([np.cos(ang), np.ones((n_ctx, A_SUB // 2))], axis=0)
    sin = np.concatenate([np.sin(ang), np.zeros((n_ctx, A_SUB // 2))], axis=0)
    cos128 = np.tile(cos, (1, 4))
    sin128 = np.tile(np.concatenate([-sin, sin], axis=1), (1, 2))
    return jnp.asarray(cos128, F32), jnp.asarray(sin128, F32)


def _rope128(x, cos, sin):
    lane = lax.broadcasted_iota(jnp.int32, x.shape, 1)
    first = (lane % A_SUB) < (A_SUB // 2)
    partner = jnp.where(first, pltpu.roll(x, LANES - A_SUB // 2, 1), pltpu.roll(x, A_SUB // 2, 1))
    return x * cos + partner * sin


def _group_rms(x, gain):
    lane = lax.broadcasted_iota(jnp.int32, x.shape, 1)
    lo = lane < A_SUB
    xx = x * x
    s_lo = jnp.sum(jnp.where(lo, xx, 0.0), axis=-1, keepdims=True)
    s_hi = jnp.sum(jnp.where(lo, 0.0, xx), axis=-1, keepdims=True)
    inv = jnp.where(lo, lax.rsqrt(s_lo / A_SUB + EPS), lax.rsqrt(s_hi / A_SUB + EPS))
    return x * inv * gain


def _pick_q_tile(n, cap):
    return max(t for t in range(Q_SUB, cap + 1, Q_SUB) if n % t == 0)


def _softmax_pv(streams, kb_scr, vb_scr, s_scr):
    n = len(streams)
    dv = vb_scr.shape[1]
    outs = []
    m_prev = None
    for i in range(n + 1):
        score_spans = list(range(streams[i][1], streams[i][2], KEY_TILE)) if i < n else []
        value_spans = list(range(streams[i - 1][1], streams[i - 1][2], KEY_TILE)) if i > 0 else []
        m_acc = jnp.full((Q_SUB, LANES), -jnp.inf, F32)
        l_acc = jnp.zeros((Q_SUB, LANES), F32)
        o = jnp.zeros((Q_SUB, dv), F32)
        for j in range(max(len(score_spans), len(value_spans))):
            if j < len(score_spans):
                ks = score_spans[j]
                s = lax.dot_general(streams[i][0], kb_scr[ks:ks + KEY_TILE, :], NT_DIMS,
                                    preferred_element_type=F32)
                s_scr[i % 2, :, j * KEY_TILE:(j + 1) * KEY_TILE] = s
                for t in range(KEY_TILE // LANES):
                    m_acc = jnp.maximum(m_acc, s[:, t * LANES:(t + 1) * LANES])
            if j < len(value_spans):
                ks = value_spans[j]
                e = jnp.exp2(s_scr[(i - 1) % 2, :, j * KEY_TILE:(j + 1) * KEY_TILE] - m_prev)
                for t in range(KEY_TILE // LANES):
                    l_acc = l_acc + e[:, t * LANES:(t + 1) * LANES]
                o = o + jnp.dot(e.astype(BF16), vb_scr[ks:ks + KEY_TILE, :], preferred_element_type=F32)
        if i > 0:
            outs.append((o, jnp.sum(l_acc, axis=-1, keepdims=True)))
        if i < n:
            m_prev = jnp.max(m_acc, axis=-1, keepdims=True)
    return outs


def _diff_attn_kernel(lam_ref, qg_ref, kg_ref, sg_ref, cq_ref, sq_ref, ck_ref, sk_ref,
                      q_ref, k_ref, v_ref, o_ref, kb_scr, vb_scr, s_scr, *, rows, n_lat, lam_init, tq):
    qi = pl.program_id(2)

    @pl.when(qi == 0)
    def _():
        k = _rope128(_group_rms(k_ref[...].astype(F32), kg_ref[...]), ck_ref[...], sk_ref[...])
        kb_scr[...] = k.astype(BF16)
        vb_scr[...] = v_ref[...].astype(BF16)

    lp = lam_ref[...]
    lam = (jnp.exp(jnp.sum(lp[0:1] * lp[1:2], axis=-1, keepdims=True))
           - jnp.exp(jnp.sum(lp[2:3] * lp[3:4], axis=-1, keepdims=True)) + lam_init)

    def attend(first_key):
        streams = []
        for i, k0 in enumerate(first_key):
            sl = slice(i * Q_SUB, (i + 1) * Q_SUB)
            q = _rope128(_group_rms(q_ref[sl, :].astype(F32), qg_ref[...]), cq_ref[sl, :], sq_ref[sl, :])
            q = q * (A_SUB ** -0.5 * LOG2E)
            lane = lax.broadcasted_iota(jnp.int32, q.shape, 1)
            streams.append((jnp.where(lane < A_SUB, q, 0.0).astype(BF16), k0, rows))
            streams.append((jnp.where(lane < A_SUB, 0.0, q).astype(BF16), k0, rows))
        outs = _softmax_pv(streams, kb_scr, vb_scr, s_scr)
        for i in range(len(first_key)):
            (o1, l1), (o2, l2) = outs[2 * i], outs[2 * i + 1]
            o = o1 * (1.0 / l1) - o2 * (lam / l2)
            o_ref[i * Q_SUB:(i + 1) * Q_SUB, :] = (_rms(o, sg_ref[...]) * (1.0 - lam_init)).astype(o_ref.dtype)

    patterns = {}
    for t in range(rows // tq):
        pat = tuple(n_lat if t * tq + r0 >= n_lat else 0 for r0 in range(0, tq, Q_SUB))
        patterns.setdefault(pat, []).append(t)
    for pat, tiles in patterns.items():
        @pl.when((qi >= tiles[0]) & (qi <= tiles[-1]))
        def _(pat=pat):
            attend(pat)


def _diff_attn(proj, lam_p, q_gain, k_gain, sub_gain, cos, sin, *, n_heads, n_lat, lam_init, tq_cap=768):
    bsz, rows, _ = proj.shape
    tq = _pick_q_tile(rows, tq_cap)
    kern = functools.partial(_diff_attn_kernel, rows=rows, n_lat=n_lat, lam_init=lam_init, tq=tq)
    vec = lambda: pl.BlockSpec((1, LANES), lambda b, h, i: (0, 0))
    return pl.pallas_call(
        kern,
        out_shape=jax.ShapeDtypeStruct((bsz, rows, n_heads * HEAD_DIM), BF16),
        grid=(bsz, n_heads, rows // tq),
        in_specs=[
            pl.BlockSpec((4, A_SUB), lambda b, h, i: (0, 0)),
            vec(), vec(), vec(),
            pl.BlockSpec((tq, LANES), lambda b, h, i: (i, 0)),
            pl.BlockSpec((tq, LANES), lambda b, h, i: (i, 0)),
            pl.BlockSpec((rows, LANES), lambda b, h, i: (0, 0)),
            pl.BlockSpec((rows, LANES), lambda b, h, i: (0, 0)),
            pl.BlockSpec((None, tq, HEAD_DIM), lambda b, h, i: (b, i, h)),
            pl.BlockSpec((None, rows, HEAD_DIM), lambda b, h, i: (b, 0, n_heads + h)),
            pl.BlockSpec((None, rows, HEAD_DIM), lambda b, h, i: (b, 0, 2 * n_heads + h)),
        ],
        out_specs=pl.BlockSpec((None, tq, HEAD_DIM), lambda b, h, i: (b, i, h)),
        scratch_shapes=[pltpu.VMEM((rows, HEAD_DIM), BF16), pltpu.VMEM((rows, HEAD_DIM), BF16),
                        pltpu.VMEM((2, Q_SUB, rows), F32)],
        compiler_params=_params(("parallel", "parallel", "arbitrary")),
        name="diff_attn",
    )(lam_p, q_gain, k_gain, sub_gain, cos, sin, cos, sin, proj, proj, proj)


def _col(row_vec, eye):
    return jnp.sum(jnp.where(eye, row_vec, 0.0), axis=-1, keepdims=True)


def _gdn_prepare(x_refs, cw_refs, dsts, pad_scr, cols, *, rows, n_lat, tile):
    n_tiles = rows // tile
    for x_ref, cw_ref, dst, kind in zip(x_refs, cw_refs, dsts, "qkv"):
        pad_scr[CONV_PAD:CONV_PAD + rows, :] = x_ref[:, cols].astype(F32)
        cw = cw_ref[:, cols]
        for r in range(n_tiles):
            r0 = r * tile
            t = r0 + lax.broadcasted_iota(jnp.int32, (tile, 1), 0)
            acc = jnp.zeros((tile, HEAD_DIM), F32)
            for j in range(CONV_K):
                dd = j - CONV_K // 2
                xs = pad_scr[CONV_PAD + r0 + dd:CONV_PAD + r0 + dd + tile, :]
                if dd > 0 and r0 < n_lat <= r0 + tile + dd and n_lat < rows:
                    xs = jnp.where((t < n_lat) & (t + dd >= n_lat), 0.0, xs)
                if dd < 0 and r0 + dd < n_lat <= r0 + tile and n_lat < rows:
                    xs = jnp.where((t >= n_lat) & (t + dd < n_lat), 0.0, xs)
                acc = acc + xs * cw[j:j + 1, :]
            y = acc * jax.nn.sigmoid(acc)
            if kind != "v":
                y = y * lax.rsqrt(jnp.sum(y * y, axis=-1, keepdims=True) + EPS)
            if kind == "q":
                y = y * (HEAD_DIM ** -0.5)
            dst[r0:r0 + tile, :] = y


def _gdn_local(gidx, head0, heads, prm_ref, gates_ref, q_scr, k_scr, v_scr,
               w_scr, u_scr, qd_scr, kd_scr, qk_scr, gt_scr):
    per = GROUP // CHUNK
    ii = lax.broadcasted_iota(jnp.int32, (GROUP, GROUP), 0)
    jj = lax.broadcasted_iota(jnp.int32, (GROUP, GROUP), 1)
    eye = ii == jj
    same = (ii // CHUNK) == (jj // CHUNK)
    r0 = pl.multiple_of(gidx * GROUP, GROUP)
    rg = pl.ds(r0, GROUP)
    hds = range(heads)
    chains = [(hh, d) for hh in hds for d in (0, 1)]
    idx = range(len(chains))
    kc = [k_scr[hh, rg, :] for hh in hds]
    vc = [v_scr[hh, rg, :] for hh in hds]
    qc = [q_scr[hh, rg, :] for hh in hds]
    kk = [_dot_nt(kc[hh], kc[hh]) for hh in hds]
    qk = [_dot_nt(qc[hh], kc[hh]) for hh in hds]
    big_g, beta = [], []
    for hh, d in chains:
        alog = jnp.full((1, GROUP), prm_ref[d, head0 + hh], F32)
        dtb = jnp.full((1, GROUP), prm_ref[2 + d, head0 + hh], F32)
        g = -jnp.exp(alog) * jax.nn.softplus(gates_ref[d, hh, gidx] + dtb)
        beta.append(jax.nn.sigmoid(gates_ref[2 + d, hh, gidx]))
        incl = same & ((ii <= jj) if d == 0 else (ii >= jj))
        big_g.append(jnp.dot(g, jnp.where(incl, 1.0, 0.0), precision=lax.Precision.HIGHEST,
                             preferred_element_type=F32))
    g_col, beta_col, g_end_col, gamma, p, t_inv = [], [], [], [], [], []
    for ci, (hh, d) in enumerate(chains):
        g_col.append(_col(big_g[ci], eye))
        beta_col.append(_col(beta[ci], eye))
        end_lane = (ii // CHUNK) * CHUNK + (CHUNK - 1 if d == 0 else 0)
        g_end_col.append(jnp.sum(jnp.where(jj == end_lane, big_g[ci], 0.0), axis=-1, keepdims=True))
        after = same & ((ii >= jj) if d == 0 else (ii <= jj))
        strict = same & ((ii > jj) if d == 0 else (ii < jj))
        gamma.append(jnp.exp(jnp.where(after, g_col[ci] - big_g[ci], -jnp.inf)))
        p.append(jnp.where(strict, -(beta_col[ci] * kk[hh] * gamma[ci]), 0.0))
        t_inv.append(jnp.where(eye, 1.0, 0.0) + p[ci])
    for _ in range(int(math.log2(CHUNK)) - 1):
        p = [_dot(p[ci], p[ci]) for ci in idx]
        t_inv = [t_inv[ci] + _dot(t_inv[ci], p[ci]) for ci in idx]
    e_col = [jnp.exp(g_col[ci]) for ci in idx]
    w = [_dot(t_inv[ci], kc[hh] * (beta_col[ci] * e_col[ci])) for ci, (hh, d) in enumerate(chains)]
    u = [_dot(t_inv[ci], vc[hh] * beta_col[ci]) for ci, (hh, d) in enumerate(chains)]
    for ci, (hh, d) in enumerate(chains):
        w_scr[hh, d, rg, :] = w[ci].astype(BF16)
        u_scr[hh, d, rg, :] = u[ci]
        qd_scr[hh, d, rg, :] = (qc[hh] * e_col[ci]).astype(BF16)
        kd_scr[hh, d, rg, :] = (kc[hh] * jnp.exp(g_end_col[ci] - g_col[ci])).astype(BF16)
        qkg = (qk[hh] * gamma[ci]).astype(BF16)
        for a in range(per):
            lo = a * CHUNK
            qk_scr[hh, d, pl.ds(r0 + lo, CHUNK), :] = qkg[lo:lo + CHUNK, lo:lo + CHUNK]
            end = lo + (CHUNK - 1 if d == 0 else 0)
            gt_scr[hh, d, gidx * per + a] = jnp.broadcast_to(jnp.exp(big_g[ci][:, end:end + 1]), (1, HEAD_DIM))


def _gdn_kernel(prm_ref, gates_ref, cwq_ref, cwk_ref, cwv_ref, og_ref, xq_ref, xk_ref, xv_ref, z_ref,
                o_ref, pad_scr, q_scr, k_scr, v_scr, w_scr, u_scr, qd_scr, kd_scr, qk_scr, gt_scr, oacc_scr,
                *, rows, n_lat, tile, heads):
    n_chunks = rows // CHUNK
    lat_chunks = n_lat // CHUNK
    head0 = pl.program_id(1) * heads
    chains = [(hh, d) for hh in range(heads) for d in (0, 1)]
    pad_scr[0:CONV_PAD, :] = jnp.zeros((CONV_PAD, HEAD_DIM), F32)
    pad_scr[CONV_PAD + rows:, :] = jnp.zeros((CONV_PAD, HEAD_DIM), F32)

    for hh in range(heads):
        cols = slice(hh * HEAD_DIM, (hh + 1) * HEAD_DIM)
        _gdn_prepare((xq_ref, xk_ref, xv_ref), (cwq_ref, cwk_ref, cwv_ref),
                     (q_scr.at[hh], k_scr.at[hh], v_scr.at[hh]), pad_scr, cols, rows=rows, n_lat=n_lat, tile=tile)

    def local_body(gidx, carry):
        _gdn_local(gidx, head0, heads, prm_ref, gates_ref, q_scr, k_scr, v_scr,
                   w_scr, u_scr, qd_scr, kd_scr, qk_scr, gt_scr)
        return carry

    lax.fori_loop(0, rows // GROUP, local_body, 0)

    def scan_body(s, states):
        cidx = [lax.rem(s + lat_chunks, n_chunks) if d == 0 else n_chunks - 1 - s for hh, d in chains]
        rc = [pl.ds(pl.multiple_of(c * CHUNK, CHUNK), CHUNK) for c in cidx]
        sb = [st.astype(BF16) for st in states]
        ws = [jnp.dot(w_scr[hh, d, rc[ci], :], sb[ci], preferred_element_type=F32)
              for ci, (hh, d) in enumerate(chains)]
        vb = [(u_scr[hh, d, rc[ci], :] - ws[ci]).astype(BF16) for ci, (hh, d) in enumerate(chains)]
        new_states = [gt_scr[hh, d, cidx[ci]] * states[ci] + lax.dot_general(
            kd_scr[hh, d, rc[ci], :], vb[ci], TN_DIMS, preferred_element_type=F32)
            for ci, (hh, d) in enumerate(chains)]
        for ci, (hh, d) in enumerate(chains):
            oacc_scr[hh, d, rc[ci], :] = (jnp.dot(qd_scr[hh, d, rc[ci], :], sb[ci], preferred_element_type=F32)
                                          + jnp.dot(qk_scr[hh, d, rc[ci], :], vb[ci], preferred_element_type=F32))
        return tuple(new_states)

    zero = jnp.zeros((HEAD_DIM, HEAD_DIM), F32)
    lax.fori_loop(0, n_chunks, scan_body, (zero,) * len(chains))

    for hh in range(heads):
        cols = slice(hh * HEAD_DIM, (hh + 1) * HEAD_DIM)
        for r0 in range(0, rows, tile):
            o = oacc_scr[hh, 0, r0:r0 + tile, :] + oacc_scr[hh, 1, r0:r0 + tile, :]
            z = z_ref[r0:r0 + tile, cols].astype(F32)
            o_ref[r0:r0 + tile, cols] = (_rms(o, og_ref[...]) * (z * jax.nn.sigmoid(z))).astype(o_ref.dtype)


def _gdn(proj, gates, prm, conv_w, o_gain, *, n_heads, col0, n_lat, tile=256, heads=2):
    bsz, rows, _ = proj.shape
    n_chunks = rows // CHUNK
    wide = heads * HEAD_DIM
    kern = functools.partial(_gdn_kernel, rows=rows, n_lat=n_lat, tile=tile, heads=heads)
    seq = lambda off: pl.BlockSpec((None, rows, wide), lambda b, h: (b, 0, (col0 + off * n_heads) // heads + h))
    cw = lambda off: pl.BlockSpec((CONV_K, wide), lambda b, h: (0, off * n_heads // heads + h))
    per_head = lambda shape, dt: pltpu.VMEM((heads, 2) + shape, dt)
    return pl.pallas_call(
        kern,
        out_shape=jax.ShapeDtypeStruct((bsz, rows, n_heads * HEAD_DIM), BF16),
        grid=(bsz, n_heads // heads),
        in_specs=[
            pl.BlockSpec(memory_space=pltpu.SMEM),
            pl.BlockSpec((None, 4, heads, rows // GROUP, 1, GROUP), lambda b, h: (b, 0, h, 0, 0, 0)),
            cw(0), cw(1), cw(2),
            pl.BlockSpec((1, HEAD_DIM), lambda b, h: (0, 0)),
            seq(0), seq(1), seq(2), seq(3),
        ],
        out_specs=pl.BlockSpec((None, rows, wide), lambda b, h: (b, 0, h)),
        scratch_shapes=[
            pltpu.VMEM((rows + 2 * CONV_PAD, HEAD_DIM), F32),
            pltpu.VMEM((heads, rows, HEAD_DIM), F32),
            pltpu.VMEM((heads, rows, HEAD_DIM), F32),
            pltpu.VMEM((heads, rows, HEAD_DIM), F32),
            per_head((rows, HEAD_DIM), BF16),
            per_head((rows, HEAD_DIM), F32),
            per_head((rows, HEAD_DIM), BF16),
            per_head((rows, HEAD_DIM), BF16),
            per_head((rows, CHUNK), BF16),
            per_head((n_chunks, 1, HEAD_DIM), F32),
            per_head((rows, HEAD_DIM), F32),
        ],
        compiler_params=_params(("parallel", "arbitrary")),
        name="gdn",
    )(prm, gates, conv_w, conv_w, conv_w, o_gain, proj, proj, proj, proj)


def _mla_kernel(qng_ref, qrg_ref, kng_ref, krg_ref, cq_ref, sq_ref, ck_ref, sk_ref,
                qn_ref, qr_ref, kn_ref, v_ref, kr_ref, o_ref, kb_scr, vb_scr, s_scr, *, scale, rows, tq):
    h = pl.program_id(1)

    @pl.when(pl.program_id(2) == 0)
    def _():
        kb_scr[:, :QK_NOPE] = _rms(kn_ref[...].astype(F32), kng_ref[...]).astype(BF16)
        kr = _rope128(_rms(kr_ref[...], krg_ref[...]), ck_ref[...], sk_ref[...])
        kb_scr[:, QK_NOPE:] = kr.astype(BF16)
        vb_scr[...] = v_ref[...].astype(BF16)

    streams = []
    for r0 in range(0, tq, Q_SUB):
        sl = slice(r0, r0 + Q_SUB)
        qn = _rms(qn_ref[sl, :].astype(F32), qng_ref[...])
        qr = _rope128(_group_rms(qr_ref[sl, :].astype(F32), qrg_ref[...]), cq_ref[sl, :], sq_ref[sl, :])
        lane = lax.broadcasted_iota(jnp.int32, qr.shape, 1)
        mine = (lane // QK_ROPE) == (h % 2)
        q = jnp.concatenate([qn, jnp.where(mine, qr, 0.0)], axis=-1) * (scale * LOG2E)
        streams.append((q.astype(BF16), 0, rows))
    for i, (o, l) in enumerate(_softmax_pv(streams, kb_scr, vb_scr, s_scr)):
        o_ref[i * Q_SUB:(i + 1) * Q_SUB, :] = (o * (1.0 / l)).astype(o_ref.dtype)


def _mla_attn(q, kv, proj, qn_gain, qr_gain, kn_gain, kr_gain, cos, sin, *, n_heads, n_lat, kr_block,
              tq_cap=1024):
    bsz, rows, _ = kv.shape
    tq = _pick_q_tile(n_lat, tq_cap)
    kern = functools.partial(_mla_kernel, scale=(QK_NOPE + QK_ROPE) ** -0.5, rows=rows, tq=tq)
    vec = lambda: pl.BlockSpec((1, LANES), lambda b, h, i: (0, 0))
    return pl.pallas_call(
        kern,
        out_shape=jax.ShapeDtypeStruct((bsz, n_lat, n_heads * V_HEAD), BF16),
        grid=(bsz, n_heads, n_lat // tq),
        in_specs=[
            vec(), vec(), vec(), vec(),
            pl.BlockSpec((tq, LANES), lambda b, h, i: (i, 0)),
            pl.BlockSpec((tq, LANES), lambda b, h, i: (i, 0)),
            pl.BlockSpec((rows, LANES), lambda b, h, i: (0, 0)),
            pl.BlockSpec((rows, LANES), lambda b, h, i: (0, 0)),
            pl.BlockSpec((None, tq, QK_NOPE), lambda b, h, i: (b, i, h)),
            pl.BlockSpec((None, tq, LANES), lambda b, h, i: (b, i, n_heads + h // 2)),
            pl.BlockSpec((None, rows, QK_NOPE), lambda b, h, i: (b, 0, 2 * h)),
            pl.BlockSpec((None, rows, V_HEAD), lambda b, h, i: (b, 0, 2 * h + 1)),
            pl.BlockSpec((None, rows, LANES), lambda b, h, i: (b, 0, kr_block)),
        ],
        out_specs=pl.BlockSpec((None, tq, V_HEAD), lambda b, h, i: (b, i, h)),
        scratch_shapes=[pltpu.VMEM((rows, QK_NOPE + LANES), BF16), pltpu.VMEM((rows, V_HEAD), BF16),
                        pltpu.VMEM((2, Q_SUB, rows), F32)],
        compiler_params=_params(("parallel", "parallel", "arbitrary")),
        name="mla_attn",
    )(qn_gain, qr_gain, kn_gain, kr_gain, cos, sin, cos, sin, q, q, kv, kv, proj)


def _pad_cols(w, n):
    return jnp.pad(w, ((0, 0), (0, n - w.shape[1])))


def kernel(x, c, ctx, c_ctx, ada_w, ada_b, norm1, norm2, mlp_w1, mlp_w2, e_w_in, e_q_norm, e_k_norm, e_lam_q1, e_lam_k1, e_lam_q2, e_lam_k2, e_subln, e_conv, e_alog_f, e_alog_b, e_dtb_f, e_dtb_b, e_o_norm, e_w_out, o_w_in, o_qa_norm, o_w_uq, o_kva_norm, o_w_ukv, o_qn_norm, o_qr_norm, o_kn_norm, o_kr_norm, o_w_out):
    bsz, n_lat, d = x.shape
    n_ctx = ctx.shape[1]
    rows = n_lat + n_ctx
    depth = ada_w.shape[0]
    assert depth == 2 and bsz < 8
    mix_heads = d // HEAD_DIM
    a_heads = mix_heads // 2
    b_heads = mix_heads - a_heads
    c_heads = d // V_HEAD
    dup = lambda g: jnp.tile(g.reshape(1, -1), (1, 2))

    svec = jnp.concatenate([c, c_ctx[None, :], jnp.zeros((8 - bsz - 1, d), F32)], axis=0)
    mod = _adaln_mod(svec, ada_w, ada_b).reshape(depth, 8, 6, d)

    cos, sin = _rope_tables(n_lat, n_ctx)
    xs = jnp.concatenate([x, ctx], axis=1)

    a_qk = a_heads * HEAD_DIM
    main = 7 * a_qk
    n_gate = 4 * b_heads
    w_gate = _pad_cols(e_w_in[0][:, main:], LANES)
    proj, gates = _norm_matmul(xs, 0, d, norm1[0], e_w_in[0][:, :main].astype(BF16), mod=mod[0], which=0,
                               n_lat=n_lat, w_side=w_gate.astype(BF16), out_dtype=BF16, name="in_proj0")
    lam_p = jnp.concatenate([e_lam_q1, e_lam_k1, e_lam_q2, e_lam_k2], axis=0)
    o_a = _diff_attn(proj, lam_p, dup(e_q_norm[0]), dup(e_k_norm[0]), e_subln[0].reshape(1, -1), cos, sin,
                     n_heads=a_heads, n_lat=n_lat, lam_init=0.8 - 0.6 * math.exp(-0.3 * 0))
    gates = gates[:, :, :n_gate].reshape(bsz, rows // GROUP, GROUP, 4, b_heads)
    gates = gates.transpose(0, 3, 4, 1, 2).reshape(bsz, 4, b_heads, rows // GROUP, 1, GROUP)
    prm = jnp.concatenate([e_alog_f, e_alog_b, e_dtb_f, e_dtb_b], axis=0)
    o_b = _gdn(proj, gates, prm, e_conv[0], e_o_norm[0].reshape(1, -1), n_heads=b_heads,
               col0=3 * a_heads, n_lat=n_lat)
    xs = _out_proj(o_a, 0, o_b, 0, e_w_out[0].astype(BF16), xs, mod[0], gate_row=2, n_lat=n_lat, rows=rows,
                   name="out_proj0")
    hid = _norm_matmul(xs, 0, d, norm2[0], mlp_w1[0].astype(BF16), mod=mod[0], which=1, n_lat=n_lat,
                       relu2=True, out_dtype=BF16, name="mlp_up0")
    xs = _down_proj(hid, mlp_w2[0].astype(BF16), xs, mod[0], gate_row=5, n_lat=n_lat, name="mlp_down0")

    w_in1 = jnp.concatenate([o_w_in[0], o_w_in[0][:, Q_LORA + KV_LORA:]], axis=1)
    proj = _norm_matmul(xs, 0, d, norm1[1], w_in1.astype(BF16), mod=mod[1], which=0, n_lat=n_lat,
                        tn_cap=w_in1.shape[1], name="in_proj1")
    w_uq = o_w_uq[0].reshape(Q_LORA, c_heads, QK_NOPE + QK_ROPE)
    w_uq = jnp.concatenate([w_uq[:, :, :QK_NOPE].reshape(Q_LORA, -1), w_uq[:, :, QK_NOPE:].reshape(Q_LORA, -1)],
                           axis=1)
    q = _norm_matmul(proj, 0, Q_LORA, o_qa_norm[0], w_uq.astype(BF16), tn_cap=1024, out_dtype=BF16,
                     name="q_up")
    kv = _norm_matmul(proj, 1, KV_LORA, o_kva_norm[0], o_w_ukv[0].astype(BF16), tn_cap=1024, out_dtype=BF16,
                      name="kv_up")
    o_c = _mla_attn(q, kv, proj, o_qn_norm[0].reshape(1, -1), dup(o_qr_norm[0]), o_kn_norm[0].reshape(1, -1),
                    dup(o_kr_norm[0]), cos, sin, n_heads=c_heads, n_lat=n_lat,
                    kr_block=(Q_LORA + KV_LORA) // LANES)
    xl = _out_proj(o_c, 0, o_c, 1, o_w_out[0].astype(BF16), xs, mod[1], gate_row=2, n_lat=n_lat, rows=n_lat,
                   tm_cap=1024, name="out_proj1")
    hid = _norm_matmul(xl, 0, d, norm2[1], mlp_w1[1].astype(BF16), mod=mod[1], which=1, n_lat=n_lat,
                       relu2=True, out_dtype=BF16, tm_cap=1024, name="mlp_up1")
    return _down_proj(hid, mlp_w2[1].astype(BF16), xl, mod[1], gate_row=5, n_lat=n_lat, tm_cap=1024,
                      name="mlp_down1")
```

```python
import functools
import math

import numpy as np
import jax
import jax.numpy as jnp
from jax import lax
from jax.experimental import pallas as pl
from jax.experimental.pallas import tpu as pltpu

EPS = 1e-6
ROPE_BASE = 10000.0
GRID_W = 64
HEAD_DIM = 128
A_SUB = 64
CHUNK = 64
CONV_K = 5
QK_NOPE = 128
QK_ROPE = 64
V_HEAD = 128
Q_LORA = 512
KV_LORA = 512
LANES = 128
CONV_PAD = 8
KEY_TILE = 256
Q_SUB = 128
GROUP = 256
NORM_ROWS = 32
LOG2E = math.log2(math.e)
VMEM_LIMIT = 56 * 1024 * 1024

F32 = jnp.float32
BF16 = jnp.bfloat16
NT_DIMS = (((1,), (1,)), ((), ()))
TN_DIMS = (((0,), (0,)), ((), ()))


def _params(sem):
    return pltpu.CompilerParams(dimension_semantics=sem, vmem_limit_bytes=VMEM_LIMIT)


def _dot(a, b):
    return jnp.dot(a.astype(BF16), b.astype(BF16), preferred_element_type=F32)


def _dot_nt(a, b):
    return lax.dot_general(a.astype(BF16), b.astype(BF16), NT_DIMS, preferred_element_type=F32)


def _dot_tn(a, b):
    return lax.dot_general(a.astype(BF16), b.astype(BF16), TN_DIMS, preferred_element_type=F32)


def _rms(x, gain):
    return x * lax.rsqrt(jnp.mean(x * x, axis=-1, keepdims=True) + EPS) * gain


def _pick_tile(n, cap):
    t = min(n, cap)
    while n % t or t % 8:
        t -= 8
    return t


def _mod_kernel(s_ref, w_ref, b_ref, o_ref):
    s = s_ref[...]
    s = s * jax.nn.sigmoid(s)
    o_ref[...] = _dot(s, w_ref[...]) + b_ref[...]


def _adaln_mod(svec, ada_w, ada_b):
    depth, d, n = ada_w.shape
    tn = _pick_tile(n, 1024)
    return pl.pallas_call(
        _mod_kernel,
        out_shape=jax.ShapeDtypeStruct((depth, 8, n), F32),
        grid=(depth, n // tn),
        in_specs=[
            pl.BlockSpec((8, d), lambda l, j: (0, 0)),
            pl.BlockSpec((None, d, tn), lambda l, j: (l, 0, j)),
            pl.BlockSpec((None, 1, tn), lambda l, j: (l, 0, j)),
        ],
        out_specs=pl.BlockSpec((None, 8, tn), lambda l, j: (l, 0, j)),
        compiler_params=_params(("parallel", "parallel")),
        name="adaln_mod",
    )(svec, ada_w, ada_b.reshape(depth, 1, n))


def _norm_matmul_kernel(*refs, tm, n_lat, shift_row, scale_row, modulate, relu2, side):
    refs = list(refs)
    x_ref, g_ref = refs[:2]
    mb_ref, mc_ref = refs[2:4] if modulate else (None, None)
    rest = refs[4:] if modulate else refs[2:]
    if side:
        w_ref, ws_ref, o_ref, os_ref, h_scr = rest
    else:
        w_ref, o_ref, h_scr = rest

    @pl.when(pl.program_id(2) == 0)
    def _():
        def norm_rows(c, carry):
            r0 = pl.multiple_of(c * NORM_ROWS, NORM_ROWS)
            y = _rms(x_ref[pl.ds(r0, NORM_ROWS), :], g_ref[...])
            if modulate:
                t = pl.program_id(1) * tm + r0 + lax.broadcasted_iota(jnp.int32, (NORM_ROWS, 1), 0)
                is_ctx = t >= n_lat
                sc = jnp.where(is_ctx, mc_ref[scale_row:scale_row + 1, :], mb_ref[scale_row:scale_row + 1, :])
                sh = jnp.where(is_ctx, mc_ref[shift_row:shift_row + 1, :], mb_ref[shift_row:shift_row + 1, :])
                y = y * (1.0 + sc) + sh
            h_scr[pl.ds(r0, NORM_ROWS), :] = y.astype(BF16)
            return carry

        lax.fori_loop(0, tm // NORM_ROWS, norm_rows, 0)
        if side:
            os_ref[...] = jnp.dot(h_scr[...], ws_ref[...], preferred_element_type=F32)

    acc = jnp.dot(h_scr[...], w_ref[...], preferred_element_type=F32)
    if relu2:
        acc = jnp.square(jnp.maximum(acc, 0.0))
    o_ref[...] = acc.astype(o_ref.dtype)


def _weight_spec(w, layer, block, index_map):
    if w.ndim == 2:
        return pl.BlockSpec(block, index_map)
    return pl.BlockSpec((None,) + block, lambda *g: (layer,) + index_map(*g))


def _norm_matmul(x, col_block, k, gain, w, *, layer=0, n=None, mod=None, which=0, n_lat=None, relu2=False,
                 w_side=None, out_dtype=F32, tm_cap=1152, tn_cap=1024, name="norm_matmul"):
    bsz, rows, _ = x.shape
    if n is None:
        n = w.shape[-1]
    tm = _pick_tile(rows, tm_cap)
    tn = _pick_tile(n, tn_cap)
    modulate = mod is not None
    side = w_side is not None
    if n_lat is None:
        n_lat = rows
    in_specs = [
        pl.BlockSpec((None, tm, k), lambda b, i, j: (b, i, col_block)),
        pl.BlockSpec((1, k), lambda b, i, j: (0, 0)),
    ]
    args = [x, gain.reshape(1, k)]
    if modulate:
        in_specs += [
            pl.BlockSpec((None, 6, k), lambda b, i, j: (b, 0, 0)),
            pl.BlockSpec((None, 6, k), lambda b, i, j: (bsz, 0, 0)),
        ]
        args += [mod, mod]
    in_specs.append(_weight_spec(w, layer, (k, tn), lambda b, i, j: (0, j)))
    args.append(w)
    out_shape = [jax.ShapeDtypeStruct((bsz, rows, n), out_dtype)]
    out_specs = [pl.BlockSpec((None, tm, tn), lambda b, i, j: (b, i, j))]
    if side:
        in_specs.append(pl.BlockSpec((k, LANES), lambda b, i, j: (0, 0)))
        args.append(w_side)
        out_shape.append(jax.ShapeDtypeStruct((bsz, rows, LANES), F32))
        out_specs.append(pl.BlockSpec((None, tm, LANES), lambda b, i, j: (b, i, 0)))
    kern = functools.partial(_norm_matmul_kernel, tm=tm, n_lat=n_lat, shift_row=3 * which,
                             scale_row=3 * which + 1, modulate=modulate, relu2=relu2, side=side)
    out = pl.pallas_call(
        kern,
        out_shape=out_shape,
        grid=(bsz, rows // tm, n // tn),
        in_specs=in_specs,
        out_specs=out_specs,
        scratch_shapes=[pltpu.VMEM((tm, k), BF16)],
        compiler_params=_params(("parallel", "parallel", "arbitrary")),
        name=name,
    )(*args)
    return out if side else out[0]


def _gate(mb_ref, mc_ref, gate_row, i, tm, n_lat):
    t = i * tm + lax.broadcasted_iota(jnp.int32, (tm, 1), 0)
    return jnp.where(t >= n_lat, mc_ref[gate_row:gate_row + 1, :], mb_ref[gate_row:gate_row + 1, :])


def _out_proj_kernel(a1_ref, a2_ref, w_ref, r_ref, mb_ref, mc_ref, o_ref, *, tm, n_lat, k1, gate_row):
    acc = jnp.dot(a1_ref[...], w_ref[:k1, :], preferred_element_type=F32)
    acc += jnp.dot(a2_ref[...], w_ref[k1:, :], preferred_element_type=F32)
    g = _gate(mb_ref, mc_ref, gate_row, pl.program_id(1), tm, n_lat)
    o_ref[...] = r_ref[...] + g * acc


def _out_proj(a1, c1, a2, c2, w, res, mod, *, gate_row, n_lat, rows, layer=0, tm_cap=1152, tn_cap=1024,
              name="out_proj"):
    bsz = res.shape[0]
    d = w.shape[-1]
    k1 = w.shape[-2] // 2
    tm = _pick_tile(rows, tm_cap)
    tn = _pick_tile(d, tn_cap)
    kern = functools.partial(_out_proj_kernel, tm=tm, n_lat=n_lat, k1=k1, gate_row=gate_row)
    return pl.pallas_call(
        kern,
        out_shape=jax.ShapeDtypeStruct((bsz, rows, d), F32),
        grid=(bsz, rows // tm, d // tn),
        in_specs=[
            pl.BlockSpec((None, tm, k1), lambda b, i, j: (b, i, c1)),
            pl.BlockSpec((None, tm, k1), lambda b, i, j: (b, i, c2)),
            _weight_spec(w, layer, (2 * k1, tn), lambda b, i, j: (0, j)),
            pl.BlockSpec((None, tm, tn), lambda b, i, j: (b, i, j)),
            pl.BlockSpec((None, 6, tn), lambda b, i, j: (b, 0, j)),
            pl.BlockSpec((None, 6, tn), lambda b, i, j: (bsz, 0, j)),
        ],
        out_specs=pl.BlockSpec((None, tm, tn), lambda b, i, j: (b, i, j)),
        compiler_params=_params(("parallel", "parallel", "arbitrary")),
        name=name,
    )(a1, a2, w, res, mod, mod)


def _down_proj_kernel(a_ref, w_ref, r_ref, mb_ref, mc_ref, o_ref, acc_scr, *, tm, n_lat, gate_row):
    kk = pl.program_id(3)

    @pl.when(kk == 0)
    def _():
        acc_scr[...] = jnp.zeros_like(acc_scr)

    acc_scr[...] += jnp.dot(a_ref[...], w_ref[...], preferred_element_type=F32)

    @pl.when(kk == pl.num_programs(3) - 1)
    def _():
        g = _gate(mb_ref, mc_ref, gate_row, pl.program_id(1), tm, n_lat)
        o_ref[...] = r_ref[...] + g * acc_scr[...]


def _down_proj(a, w, res, mod, *, gate_row, n_lat, layer=0, tm_cap=1152, tn_cap=512, tk=4096,
               name="down_proj"):
    bsz, rows, kdim = a.shape
    d = w.shape[-1]
    tm = _pick_tile(rows, tm_cap)
    tn = _pick_tile(d, tn_cap)
    kern = functools.partial(_down_proj_kernel, tm=tm, n_lat=n_lat, gate_row=gate_row)
    return pl.pallas_call(
        kern,
        out_shape=jax.ShapeDtypeStruct((bsz, rows, d), F32),
        grid=(bsz, rows // tm, d // tn, kdim // tk),
        in_specs=[
            pl.BlockSpec((None, tm, tk), lambda b, i, j, k: (b, i, k)),
            _weight_spec(w, layer, (tk, tn), lambda b, i, j, k: (k, j)),
            pl.BlockSpec((None, tm, tn), lambda b, i, j, k: (b, i, j)),
            pl.BlockSpec((None, 6, tn), lambda b, i, j, k: (b, 0, j)),
            pl.BlockSpec((None, 6, tn), lambda b, i, j, k: (bsz, 0, j)),
        ],
        out_specs=pl.BlockSpec((None, tm, tn), lambda b, i, j, k: (b, i, j)),
        scratch_shapes=[pltpu.VMEM((tm, tn), F32)],
        compiler_params=_params(("parallel", "parallel", "parallel", "arbitrary")),
        name=name,
    )(a, w, res, mod, mod)


def _rope_tables(n_lat, n_ctx):
    t = np.arange(n_lat)
    row = (t // GRID_W).astype(np.float64)
    col = (t % GRID_W).astype(np.float64)
    n_freq = A_SUB // 4
    inv_freq = ROPE_BASE ** (-np.arange(n_freq, dtype=np.float64) / n_freq)
    ang = np.concatenate([row[:, None] * inv_freq, col[:, None] * inv_freq], axis=-1)
    cos = np.concatenate([np.cos(ang), np.ones((n_ctx, A_SUB // 2))], axis=0)
    sin = np.concatenate([np.sin(ang), np.zeros((n_ctx, A_SUB // 2))], axis=0)
    cos128 = np.tile(cos, (1, 4))
    sin128 = np.tile(np.concatenate([-sin, sin], axis=1), (1, 2))
    return jnp.asarray(cos128, F32), jnp.asarray(sin128, F32)


def _rope128(x, cos, sin):
    lane = lax.broadcasted_iota(jnp.int32, x.shape, 1)
    first = (lane % A_SUB) < (A_SUB // 2)
    partner = jnp.where(first, pltpu.roll(x, LANES - A_SUB // 2, 1), pltpu.roll(x, A_SUB // 2, 1))
    return x * cos + partner * sin


def _group_rms(x, gain):
    lane = lax.broadcasted_iota(jnp.int32, x.shape, 1)
    lo = lane < A_SUB
    xx = x * x
    s_lo = jnp.sum(jnp.where(lo, xx, 0.0), axis=-1, keepdims=True)
    s_hi = jnp.sum(jnp.where(lo, 0.0, xx), axis=-1, keepdims=True)
    inv = jnp.where(lo, lax.rsqrt(s_lo / A_SUB + EPS), lax.rsqrt(s_hi / A_SUB + EPS))
    return x * inv * gain


def _pick_q_tile(n, cap):
    return max(t for t in range(Q_SUB, cap + 1, Q_SUB) if n % t == 0)


def _softmax_pv(streams, kb_scr, vb_scr, s_scr):
    n = len(streams)
    dv = vb_scr.shape[1]
    outs = []
    m_prev = None
    for i in range(n + 1):
        score_spans = list(range(streams[i][1], streams[i][2], KEY_TILE)) if i < n else []
        value_spans = list(range(streams[i - 1][1], streams[i - 1][2], KEY_TILE)) if i > 0 else []
        m_acc = jnp.full((Q_SUB, LANES), -jnp.inf, F32)
        l_acc = jnp.zeros((Q_SUB, LANES), F32)
        o = jnp.zeros((Q_SUB, dv), F32)
        for j in range(max(len(score_spans), len(value_spans))):
            if j < len(score_spans):
                ks = score_spans[j]
                s = lax.dot_general(streams[i][0], kb_scr[ks:ks + KEY_TILE, :], NT_DIMS,
                                    preferred_element_type=F32)
                s_scr[i % 2, :, j * KEY_TILE:(j + 1) * KEY_TILE] = s
                for t in range(KEY_TILE // LANES):
                    m_acc = jnp.maximum(m_acc, s[:, t * LANES:(t + 1) * LANES])
            if j < len(value_spans):
                ks = value_spans[j]
                e = jnp.exp2(s_scr[(i - 1) % 2, :, j * KEY_TILE:(j + 1) * KEY_TILE] - m_prev)
                for t in range(KEY_TILE // LANES):
                    l_acc = l_acc + e[:, t * LANES:(t + 1) * LANES]
                o = o + jnp.dot(e.astype(BF16), vb_scr[ks:ks + KEY_TILE, :], preferred_element_type=F32)
        if i > 0:
            outs.append((o, jnp.sum(l_acc, axis=-1, keepdims=True)))
        if i < n:
            m_prev = jnp.max(m_acc, axis=-1, keepdims=True)
    return outs


def _diff_attn_kernel(lam_ref, qg_ref, kg_ref, sg_ref, cq_ref, sq_ref, ck_ref, sk_ref,
                      q_ref, k_ref, v_ref, o_ref, kb_scr, vb_scr, s_scr, *, rows, n_lat, lam_init, tq):
    qi = pl.program_id(2)

    @pl.when(qi == 0)
    def _():
        k = _rope128(_group_rms(k_ref[...].astype(F32), kg_ref[...]), ck_ref[...], sk_ref[...])
        kb_scr[...] = k.astype(BF16)
        vb_scr[...] = v_ref[...].astype(BF16)

    lp = lam_ref[...]
    lam = (jnp.exp(jnp.sum(lp[0:1] * lp[1:2], axis=-1, keepdims=True))
           - jnp.exp(jnp.sum(lp[2:3] * lp[3:4], axis=-1, keepdims=True)) + lam_init)

    def attend(first_key):
        streams = []
        for i, k0 in enumerate(first_key):
            sl = slice(i * Q_SUB, (i + 1) * Q_SUB)
            q = _rope128(_group_rms(q_ref[sl, :].astype(F32), qg_ref[...]), cq_ref[sl, :], sq_ref[sl, :])
            q = q * (A_SUB ** -0.5 * LOG2E)
            lane = lax.broadcasted_iota(jnp.int32, q.shape, 1)
            streams.append((jnp.where(lane < A_SUB, q, 0.0).astype(BF16), k0, rows))
            streams.append((jnp.where(lane < A_SUB, 0.0, q).astype(BF16), k0, rows))
        outs = _softmax_pv(streams, kb_scr, vb_scr, s_scr)
        for i in range(len(first_key)):
            (o1, l1), (o2, l2) = outs[2 * i], outs[2 * i + 1]
            o = o1 * (1.0 / l1) - o2 * (lam / l2)
            o_ref[i * Q_SUB:(i + 1) * Q_SUB, :] = (_rms(o, sg_ref[...]) * (1.0 - lam_init)).astype(o_ref.dtype)

    patterns = {}
    for t in range(rows // tq):
        pat = tuple(n_lat if t * tq + r0 >= n_lat else 0 for r0 in range(0, tq, Q_SUB))
        patterns.setdefault(pat, []).append(t)
    for pat, tiles in patterns.items():
        @pl.when((qi >= tiles[0]) & (qi <= tiles[-1]))
        def _(pat=pat):
            attend(pat)


def _diff_attn(proj, lam_p, q_gain, k_gain, sub_gain, cos, sin, *, n_heads, n_lat, lam_init, tq_cap=768):
    bsz, rows, _ = proj.shape
    tq = _pick_q_tile(rows, tq_cap)
    kern = functools.partial(_diff_attn_kernel, rows=rows, n_lat=n_lat, lam_init=lam_init, tq=tq)
    vec = lambda: pl.BlockSpec((1, LANES), lambda b, h, i: (0, 0))
    return pl.pallas_call(
        kern,
        out_shape=jax.ShapeDtypeStruct((bsz, rows, n_heads * HEAD_DIM), BF16),
        grid=(bsz, n_heads, rows // tq),
        in_specs=[
            pl.BlockSpec((4, A_SUB), lambda b, h, i: (0, 0)),
            vec(), vec(), vec(),
            pl.BlockSpec((tq, LANES), lambda b, h, i: (i, 0)),
            pl.BlockSpec((tq, LANES), lambda b, h, i: (i, 0)),
            pl.BlockSpec((rows, LANES), lambda b, h, i: (0, 0)),
            pl.BlockSpec((rows, LANES), lambda b, h, i: (0, 0)),
            pl.BlockSpec((None, tq, HEAD_DIM), lambda b, h, i: (b, i, h)),
            pl.BlockSpec((None, rows, HEAD_DIM), lambda b, h, i: (b, 0, n_heads + h)),
            pl.BlockSpec((None, rows, HEAD_DIM), lambda b, h, i: (b, 0, 2 * n_heads + h)),
        ],
        out_specs=pl.BlockSpec((None, tq, HEAD_DIM), lambda b, h, i: (b, i, h)),
        scratch_shapes=[pltpu.VMEM((rows, HEAD_DIM), BF16), pltpu.VMEM((rows, HEAD_DIM), BF16),
                        pltpu.VMEM((2, Q_SUB, rows), F32)],
        compiler_params=_params(("parallel", "parallel", "arbitrary")),
        name="diff_attn",
    )(lam_p, q_gain, k_gain, sub_gain, cos, sin, cos, sin, proj, proj, proj)


def _col(row_vec, eye):
    return jnp.sum(jnp.where(eye, row_vec, 0.0), axis=-1, keepdims=True)


def _gdn_prepare(x_refs, cw_refs, dsts, pad_scr, cols, *, rows, n_lat, tile):
    n_tiles = rows // tile
    for x_ref, cw_ref, dst, kind in zip(x_refs, cw_refs, dsts, "qkv"):
        pad_scr[CONV_PAD:CONV_PAD + rows, :] = x_ref[:, cols].astype(F32)
        cw = cw_ref[:, cols]
        for r in range(n_tiles):
            r0 = r * tile
            t = r0 + lax.broadcasted_iota(jnp.int32, (tile, 1), 0)
            acc = jnp.zeros((tile, HEAD_DIM), F32)
            for j in range(CONV_K):
                dd = j - CONV_K // 2
                xs = pad_scr[CONV_PAD + r0 + dd:CONV_PAD + r0 + dd + tile, :]
                if dd > 0 and r0 < n_lat <= r0 + tile + dd and n_lat < rows:
                    xs = jnp.where((t < n_lat) & (t + dd >= n_lat), 0.0, xs)
                if dd < 0 and r0 + dd < n_lat <= r0 + tile and n_lat < rows:
                    xs = jnp.where((t >= n_lat) & (t + dd < n_lat), 0.0, xs)
                acc = acc + xs * cw[j:j + 1, :]
            y = acc * jax.nn.sigmoid(acc)
            if kind != "v":
                y = y * lax.rsqrt(jnp.sum(y * y, axis=-1, keepdims=True) + EPS)
            if kind == "q":
                y = y * (HEAD_DIM ** -0.5)
            dst[r0:r0 + tile, :] = y


def _gdn_local(gidx, head0, heads, prm_ref, gates_ref, q_scr, k_scr, v_scr,
               w_scr, u_scr, qd_scr, kd_scr, qk_scr, gt_scr):
    per = GROUP // CHUNK
    ii = lax.broadcasted_iota(jnp.int32, (GROUP, GROUP), 0)
    jj = lax.broadcasted_iota(jnp.int32, (GROUP, GROUP), 1)
    eye = ii == jj
    same = (ii // CHUNK) == (jj // CHUNK)
    r0 = pl.multiple_of(gidx * GROUP, GROUP)
    rg = pl.ds(r0, GROUP)
    hds = range(heads)
    chains = [(hh, d) for hh in hds for d in (0, 1)]
    idx = range(len(chains))
    kc = [k_scr[hh, rg, :] for hh in hds]
    vc = [v_scr[hh, rg, :] for hh in hds]
    qc = [q_scr[hh, rg, :] for hh in hds]
    kk = [_dot_nt(kc[hh], kc[hh]) for hh in hds]
    qk = [_dot_nt(qc[hh], kc[hh]) for hh in hds]
    big_g, beta = [], []
    for hh, d in chains:
        alog = jnp.full((1, GROUP), prm_ref[d, head0 + hh], F32)
        dtb = jnp.full((1, GROUP), prm_ref[2 + d, head0 + hh], F32)
        g = -jnp.exp(alog) * jax.nn.softplus(gates_ref[d, hh, gidx] + dtb)
        beta.append(jax.nn.sigmoid(gates_ref[2 + d, hh, gidx]))
        incl = same & ((ii <= jj) if d == 0 else (ii >= jj))
        big_g.append(jnp.dot(g, jnp.where(incl, 1.0, 0.0), precision=lax.Precision.HIGHEST,
                             preferred_element_type=F32))
    g_col, beta_col, g_end_col, gamma, p, t_inv = [], [], [], [], [], []
    for ci, (hh, d) in enumerate(chains):
        g_col.append(_col(big_g[ci], eye))
        beta_col.append(_col(beta[ci], eye))
        end_lane = (ii // CHUNK) * CHUNK + (CHUNK - 1 if d == 0 else 0)
        g_end_col.append(jnp.sum(jnp.where(jj == end_lane, big_g[ci], 0.0), axis=-1, keepdims=True))
        after = same & ((ii >= jj) if d == 0 else (ii <= jj))
        strict = same & ((ii > jj) if d == 0 else (ii < jj))
        gamma.append(jnp.exp(jnp.where(after, g_col[ci] - big_g[ci], -jnp.inf)))
        p.append(jnp.where(strict, -(beta_col[ci] * kk[hh] * gamma[ci]), 0.0))
        t_inv.append(jnp.where(eye, 1.0, 0.0) + p[ci])
    for _ in range(int(math.log2(CHUNK)) - 1):
        p = [_dot(p[ci], p[ci]) for ci in idx]
        t_inv = [t_inv[ci] + _dot(t_inv[ci], p[ci]) for ci in idx]
    e_col = [jnp.exp(g_col[ci]) for ci in idx]
    w = [_dot(t_inv[ci], kc[hh] * (beta_col[ci] * e_col[ci])) for ci, (hh, d) in enumerate(chains)]
    u = [_dot(t_inv[ci], vc[hh] * beta_col[ci]) for ci, (hh, d) in enumerate(chains)]
    for ci, (hh, d) in enumerate(chains):
        w_scr[hh, d, rg, :] = w[ci].astype(BF16)
        u_scr[hh, d, rg, :] = u[ci]
        qd_scr[hh, d, rg, :] = (qc[hh] * e_col[ci]).astype(BF16)
        kd_scr[hh, d, rg, :] = (kc[hh] * jnp.exp(g_end_col[ci] - g_col[ci])).astype(BF16)
        qkg = (qk[hh] * gamma[ci]).astype(BF16)
        for a in range(per):
            lo = a * CHUNK
            qk_scr[hh, d, pl.ds(r0 + lo, CHUNK), :] = qkg[lo:lo + CHUNK, lo:lo + CHUNK]
            end = lo + (CHUNK - 1 if d == 0 else 0)
            gt_scr[hh, d, gidx * per + a] = jnp.broadcast_to(jnp.exp(big_g[ci][:, end:end + 1]), (1, HEAD_DIM))


def _gdn_kernel(prm_ref, gates_ref, cwq_ref, cwk_ref, cwv_ref, og_ref, xq_ref, xk_ref, xv_ref, z_ref,
                o_ref, pad_scr, q_scr, k_scr, v_scr, w_scr, u_scr, qd_scr, kd_scr, qk_scr, gt_scr, oacc_scr,
                *, rows, n_lat, tile, heads):
    n_chunks = rows // CHUNK
    lat_chunks = n_lat // CHUNK
    head0 = pl.program_id(1) * heads
    chains = [(hh, d) for hh in range(heads) for d in (0, 1)]
    pad_scr[0:CONV_PAD, :] = jnp.zeros((CONV_PAD, HEAD_DIM), F32)
    pad_scr[CONV_PAD + rows:, :] = jnp.zeros((CONV_PAD, HEAD_DIM), F32)

    for hh in range(heads):
        cols = slice(hh * HEAD_DIM, (hh + 1) * HEAD_DIM)
        _gdn_prepare((xq_ref, xk_ref, xv_ref), (cwq_ref, cwk_ref, cwv_ref),
                     (q_scr.at[hh], k_scr.at[hh], v_scr.at[hh]), pad_scr, cols, rows=rows, n_lat=n_lat, tile=tile)

    def local_body(gidx, carry):
        _gdn_local(gidx, head0, heads, prm_ref, gates_ref, q_scr, k_scr, v_scr,
                   w_scr, u_scr, qd_scr, kd_scr, qk_scr, gt_scr)
        return carry

    lax.fori_loop(0, rows // GROUP, local_body, 0)

    def scan_body(s, states):
        cidx = [lax.rem(s + lat_chunks, n_chunks) if d == 0 else n_chunks - 1 - s for hh, d in chains]
        rc = [pl.ds(pl.multiple_of(c * CHUNK, CHUNK), CHUNK) for c in cidx]
        sb = [st.astype(BF16) for st in states]
        ws = [jnp.dot(w_scr[hh, d, rc[ci], :], sb[ci], preferred_element_type=F32)
              for ci, (hh, d) in enumerate(chains)]
        vb = [(u_scr[hh, d, rc[ci], :] - ws[ci]).astype(BF16) for ci, (hh, d) in enumerate(chains)]
        new_states = [gt_scr[hh, d, cidx[ci]] * states[ci] + lax.dot_general(
            kd_scr[hh, d, rc[ci], :], vb[ci], TN_DIMS, preferred_element_type=F32)
            for ci, (hh, d) in enumerate(chains)]
        for ci, (hh, d) in enumerate(chains):
            oacc_scr[hh, d, rc[ci], :] = (jnp.dot(qd_scr[hh, d, rc[ci], :], sb[ci], preferred_element_type=F32)
                                          + jnp.dot(qk_scr[hh, d, rc[ci], :], vb[ci], preferred_element_type=F32))
        return tuple(new_states)

    zero = jnp.zeros((HEAD_DIM, HEAD_DIM), F32)
    lax.fori_loop(0, n_chunks, scan_body, (zero,) * len(chains))

    for hh in range(heads):
        cols = slice(hh * HEAD_DIM, (hh + 1) * HEAD_DIM)
        for r0 in range(0, rows, tile):
            o = oacc_scr[hh, 0, r0:r0 + tile, :] + oacc_scr[hh, 1, r0:r0 + tile, :]
            z = z_ref[r0:r0 + tile, cols].astype(F32)
            o_ref[r0:r0 + tile, cols] = (_rms(o, og_ref[...]) * (z * jax.nn.sigmoid(z))).astype(o_ref.dtype)


def _gdn(proj, gates, prm, conv_w, o_gain, *, n_heads, col0, n_lat, tile=256, heads=2):
    bsz, rows, _ = proj.shape
    n_chunks = rows // CHUNK
    wide = heads * HEAD_DIM
    kern = functools.partial(_gdn_kernel, rows=rows, n_lat=n_lat, tile=tile, heads=heads)
    seq = lambda off: pl.BlockSpec((None, rows, wide), lambda b, h: (b, 0, (col0 + off * n_heads) // heads + h))
    cw = lambda off: pl.BlockSpec((CONV_K, wide), lambda b, h: (0, off * n_heads // heads + h))
    per_head = lambda shape, dt: pltpu.VMEM((heads, 2) + shape, dt)
    return pl.pallas_call(
        kern,
        out_shape=jax.ShapeDtypeStruct((bsz, rows, n_heads * HEAD_DIM), BF16),
        grid=(bsz, n_heads // heads),
        in_specs=[
            pl.BlockSpec(memory_space=pltpu.SMEM),
            pl.BlockSpec((None, 4, heads, rows // GROUP, 1, GROUP), lambda b, h: (b, 0, h, 0, 0, 0)),
            cw(0), cw(1), cw(2),
            pl.BlockSpec((1, HEAD_DIM), lambda b, h: (0, 0)),
            seq(0), seq(1), seq(2), seq(3),
        ],
        out_specs=pl.BlockSpec((None, rows, wide), lambda b, h: (b, 0, h)),
        scratch_shapes=[
            pltpu.VMEM((rows + 2 * CONV_PAD, HEAD_DIM), F32),
            pltpu.VMEM((heads, rows, HEAD_DIM), F32),
            pltpu.VMEM((heads, rows, HEAD_DIM), F32),
            pltpu.VMEM((heads, rows, HEAD_DIM), F32),
            per_head((rows, HEAD_DIM), BF16),
            per_head((rows, HEAD_DIM), F32),
            per_head((rows, HEAD_DIM), BF16),
            per_head((rows, HEAD_DIM), BF16),
            per_head((rows, CHUNK), BF16),
            per_head((n_chunks, 1, HEAD_DIM), F32),
            per_head((rows, HEAD_DIM), F32),
        ],
        compiler_params=_params(("parallel", "arbitrary")),
        name="gdn",
    )(prm, gates, conv_w, conv_w, conv_w, o_gain, proj, proj, proj, proj)


def _mla_kernel(qng_ref, qrg_ref, kng_ref, krg_ref, cq_ref, sq_ref, ck_ref, sk_ref,
                qn_ref, qr_ref, kn_ref, v_ref, kr_ref, o_ref, kb_scr, vb_scr, s_scr, *, scale, rows, tq):
    h = pl.program_id(1)

    @pl.when(pl.program_id(2) == 0)
    def _():
        kb_scr[:, :QK_NOPE] = _rms(kn_ref[...].astype(F32), kng_ref[...]).astype(BF16)
        kr = _rope128(_rms(kr_ref[...], krg_ref[...]), ck_ref[...], sk_ref[...])
        kb_scr[:, QK_NOPE:] = kr.astype(BF16)
        vb_scr[...] = v_ref[...].astype(BF16)

    streams = []
    for r0 in range(0, tq, Q_SUB):
        sl = slice(r0, r0 + Q_SUB)
        qn = _rms(qn_ref[sl, :].astype(F32), qng_ref[...])
        qr = _rope128(_group_rms(qr_ref[sl, :].astype(F32), qrg_ref[...]), cq_ref[sl, :], sq_ref[sl, :])
        lane = lax.broadcasted_iota(jnp.int32, qr.shape, 1)
        mine = (lane // QK_ROPE) == (h % 2)
        q = jnp.concatenate([qn, jnp.where(mine, qr, 0.0)], axis=-1) * (scale * LOG2E)
        streams.append((q.astype(BF16), 0, rows))
    for i, (o, l) in enumerate(_softmax_pv(streams, kb_scr, vb_scr, s_scr)):
        o_ref[i * Q_SUB:(i + 1) * Q_SUB, :] = (o * (1.0 / l)).astype(o_ref.dtype)


def _mla_attn(q, kv, proj, qn_gain, qr_gain, kn_gain, kr_gain, cos, sin, *, n_heads, n_lat, kr_block,
              tq_cap=1024):
    bsz, rows, _ = kv.shape
    tq = _pick_q_tile(n_lat, tq_cap)
    kern = functools.partial(_mla_kernel, scale=(QK_NOPE + QK_ROPE) ** -0.5, rows=rows, tq=tq)
    vec = lambda: pl.BlockSpec((1, LANES), lambda b, h, i: (0, 0))
    return pl.pallas_call(
        kern,
        out_shape=jax.ShapeDtypeStruct((bsz, n_lat, n_heads * V_HEAD), BF16),
        grid=(bsz, n_heads, n_lat // tq),
        in_specs=[
            vec(), vec(), vec(), vec(),
            pl.BlockSpec((tq, LANES), lambda b, h, i: (i, 0)),
            pl.BlockSpec((tq, LANES), lambda b, h, i: (i, 0)),
            pl.BlockSpec((rows, LANES), lambda b, h, i: (0, 0)),
            pl.BlockSpec((rows, LANES), lambda b, h, i: (0, 0)),
            pl.BlockSpec((None, tq, QK_NOPE), lambda b, h, i: (b, i, h)),
            pl.BlockSpec((None, tq, LANES), lambda b, h, i: (b, i, n_heads + h // 2)),
            pl.BlockSpec((None, rows, QK_NOPE), lambda b, h, i: (b, 0, 2 * h)),
            pl.BlockSpec((None, rows, V_HEAD), lambda b, h, i: (b, 0, 2 * h + 1)),
            pl.BlockSpec((None, rows, LANES), lambda b, h, i: (b, 0, kr_block)),
        ],
        out_specs=pl.BlockSpec((None, tq, V_HEAD), lambda b, h, i: (b, i, h)),
        scratch_shapes=[pltpu.VMEM((rows, QK_NOPE + LANES), BF16), pltpu.VMEM((rows, V_HEAD), BF16),
                        pltpu.VMEM((2, Q_SUB, rows), F32)],
        compiler_params=_params(("parallel", "parallel", "arbitrary")),
        name="mla_attn",
    )(qn_gain, qr_gain, kn_gain, kr_gain, cos, sin, cos, sin, q, q, kv, kv, proj)


def _pad_cols(w, n):
    return jnp.pad(w, ((0, 0), (0, n - w.shape[1])))


def kernel(x, c, ctx, c_ctx, ada_w, ada_b, norm1, norm2, mlp_w1, mlp_w2, e_w_in, e_q_norm, e_k_norm, e_lam_q1, e_lam_k1, e_lam_q2, e_lam_k2, e_subln, e_conv, e_alog_f, e_alog_b, e_dtb_f, e_dtb_b, e_o_norm, e_w_out, o_w_in, o_qa_norm, o_w_uq, o_kva_norm, o_w_ukv, o_qn_norm, o_qr_norm, o_kn_norm, o_kr_norm, o_w_out):
    bsz, n_lat, d = x.shape
    n_ctx = ctx.shape[1]
    rows = n_lat + n_ctx
    depth = ada_w.shape[0]
    assert depth == 2 and bsz < 8
    mix_heads = d // HEAD_DIM
    a_heads = mix_heads // 2
    b_heads = mix_heads - a_heads
    c_heads = d // V_HEAD
    dup = lambda g: jnp.tile(g.reshape(1, -1), (1, 2))

    svec = jnp.concatenate([c, c_ctx[None, :], jnp.zeros((8 - bsz - 1, d), F32)], axis=0)
    mod = _adaln_mod(svec, ada_w, ada_b).reshape(depth, 8, 6, d)

    cos, sin = _rope_tables(n_lat, n_ctx)
    xs = jnp.concatenate([x, ctx], axis=1)

    a_qk = a_heads * HEAD_DIM
    main = 7 * a_qk
    n_gate = 4 * b_heads
    w_gate = _pad_cols(e_w_in[0][:, main:], LANES)
    proj, gates = _norm_matmul(xs, 0, d, norm1[0], e_w_in.astype(BF16), n=main, mod=mod[0], which=0,
                               n_lat=n_lat, w_side=w_gate.astype(BF16), out_dtype=BF16, name="in_proj0")
    lam_p = jnp.concatenate([e_lam_q1, e_lam_k1, e_lam_q2, e_lam_k2], axis=0)
    o_a = _diff_attn(proj, lam_p, dup(e_q_norm[0]), dup(e_k_norm[0]), e_subln[0].reshape(1, -1), cos, sin,
                     n_heads=a_heads, n_lat=n_lat, lam_init=0.8 - 0.6 * math.exp(-0.3 * 0))
    gates = gates[:, :, :n_gate].reshape(bsz, rows // GROUP, GROUP, 4, b_heads)
    gates = gates.transpose(0, 3, 4, 1, 2).reshape(bsz, 4, b_heads, rows // GROUP, 1, GROUP)
    prm = jnp.concatenate([e_alog_f, e_alog_b, e_dtb_f, e_dtb_b], axis=0)
    o_b = _gdn(proj, gates, prm, e_conv[0], e_o_norm[0].reshape(1, -1), n_heads=b_heads,
               col0=3 * a_heads, n_lat=n_lat)
    w1 = mlp_w1.astype(BF16)
    w2 = mlp_w2.astype(BF16)
    xs = _out_proj(o_a, 0, o_b, 0, e_w_out.astype(BF16), xs, mod[0], gate_row=2, n_lat=n_lat, rows=rows,
                   name="out_proj0")
    hid = _norm_matmul(xs, 0, d, norm2[0], w1, layer=0, mod=mod[0], which=1, n_lat=n_lat,
                       relu2=True, out_dtype=BF16, name="mlp_up0")
    xs = _down_proj(hid, w2, xs, mod[0], layer=0, gate_row=5, n_lat=n_lat, name="mlp_down0")

    w_in1 = jnp.concatenate([o_w_in[0], o_w_in[0][:, Q_LORA + KV_LORA:]], axis=1)
    proj = _norm_matmul(xs, 0, d, norm1[1], w_in1.astype(BF16), mod=mod[1], which=0, n_lat=n_lat,
                        tn_cap=w_in1.shape[1], name="in_proj1")
    w_uq = o_w_uq[0].reshape(Q_LORA, c_heads, QK_NOPE + QK_ROPE)
    w_uq = jnp.concatenate([w_uq[:, :, :QK_NOPE].reshape(Q_LORA, -1), w_uq[:, :, QK_NOPE:].reshape(Q_LORA, -1)],
                           axis=1)
    q = _norm_matmul(proj, 0, Q_LORA, o_qa_norm[0], w_uq.astype(BF16), tn_cap=1024, out_dtype=BF16,
                     name="q_up")
    kv = _norm_matmul(proj, 1, KV_LORA, o_kva_norm[0], o_w_ukv.astype(BF16), tn_cap=1024, out_dtype=BF16,
                      name="kv_up")
    o_c = _mla_attn(q, kv, proj, o_qn_norm[0].reshape(1, -1), dup(o_qr_norm[0]), o_kn_norm[0].reshape(1, -1),
                    dup(o_kr_norm[0]), cos, sin, n_heads=c_heads, n_lat=n_lat,
                    kr_block=(Q_LORA + KV_LORA) // LANES)
    xl = _out_proj(o_c, 0, o_c, 1, o_w_out.astype(BF16), xs, mod[1], gate_row=2, n_lat=n_lat, rows=n_lat,
                   tm_cap=1024, name="out_proj1")
    hid = _norm_matmul(xl, 0, d, norm2[1], w1, layer=1, mod=mod[1], which=1, n_lat=n_lat,
                       relu2=True, out_dtype=BF16, tm_cap=1024, name="mlp_up1")
    return _down_proj(hid, w2, xl, mod[1], layer=1, gate_row=5, n_lat=n_lat, tm_cap=1024, name="mlp_down1")
```

```python
import functools
import math

import numpy as np
import jax
import jax.numpy as jnp
from jax import lax
from jax.experimental import pallas as pl
from jax.experimental.pallas import tpu as pltpu

EPS = 1e-6
ROPE_BASE = 10000.0
GRID_W = 64
HEAD_DIM = 128
A_SUB = 64
CHUNK = 64
CONV_K = 5
QK_NOPE = 128
QK_ROPE = 64
V_HEAD = 128
Q_LORA = 512
KV_LORA = 512
LANES = 128
CONV_PAD = 8
KEY_TILE = 256
Q_SUB = 128
GROUP = 256
NORM_ROWS = 32
LOG2E = math.log2(math.e)
VMEM_LIMIT = 56 * 1024 * 1024

F32 = jnp.float32
BF16 = jnp.bfloat16
NT_DIMS = (((1,), (1,)), ((), ()))
TN_DIMS = (((0,), (0,)), ((), ()))


def _params(sem):
    return pltpu.CompilerParams(dimension_semantics=sem, vmem_limit_bytes=VMEM_LIMIT)


def _dot(a, b):
    return jnp.dot(a.astype(BF16), b.astype(BF16), preferred_element_type=F32)


def _dot_nt(a, b):
    return lax.dot_general(a.astype(BF16), b.astype(BF16), NT_DIMS, preferred_element_type=F32)


def _dot_tn(a, b):
    return lax.dot_general(a.astype(BF16), b.astype(BF16), TN_DIMS, preferred_element_type=F32)


def _rms(x, gain):
    return x * lax.rsqrt(jnp.mean(x * x, axis=-1, keepdims=True) + EPS) * gain


def _pick_tile(n, cap):
    t = min(n, cap)
    while n % t or t % 8:
        t -= 8
    return t


def _mod_kernel(s_ref, w_ref, b_ref, o_ref):
    s = s_ref[...]
    s = s * jax.nn.sigmoid(s)
    o_ref[...] = _dot(s, w_ref[...]) + b_ref[...]


def _adaln_mod(svec, ada_w, ada_b):
    depth, d, n = ada_w.shape
    tn = _pick_tile(n, 1024)
    return pl.pallas_call(
        _mod_kernel,
        out_shape=jax.ShapeDtypeStruct((depth, 8, n), F32),
        grid=(depth, n // tn),
        in_specs=[
            pl.BlockSpec((8, d), lambda l, j: (0, 0)),
            pl.BlockSpec((None, d, tn), lambda l, j: (l, 0, j)),
            pl.BlockSpec((None, 1, tn), lambda l, j: (l, 0, j)),
        ],
        out_specs=pl.BlockSpec((None, 8, tn), lambda l, j: (l, 0, j)),
        compiler_params=_params(("parallel", "parallel")),
        name="adaln_mod",
    )(svec, ada_w, ada_b.reshape(depth, 1, n))


def _norm_matmul_kernel(*refs, tm, n_lat, shift_row, scale_row, modulate, relu2, side):
    refs = list(refs)
    x_ref, g_ref = refs[:2]
    mb_ref, mc_ref = refs[2:4] if modulate else (None, None)
    rest = refs[4:] if modulate else refs[2:]
    if side:
        w_ref, ws_ref, o_ref, os_ref, h_scr = rest
    else:
        w_ref, o_ref, h_scr = rest

    @pl.when(pl.program_id(2) == 0)
    def _():
        def norm_rows(c, carry):
            r0 = pl.multiple_of(c * NORM_ROWS, NORM_ROWS)
            y = _rms(x_ref[pl.ds(r0, NORM_ROWS), :], g_ref[...])
            if modulate:
                t = pl.program_id(1) * tm + r0 + lax.broadcasted_iota(jnp.int32, (NORM_ROWS, 1), 0)
                is_ctx = t >= n_lat
                sc = jnp.where(is_ctx, mc_ref[scale_row:scale_row + 1, :], mb_ref[scale_row:scale_row + 1, :])
                sh = jnp.where(is_ctx, mc_ref[shift_row:shift_row + 1, :], mb_ref[shift_row:shift_row + 1, :])
                y = y * (1.0 + sc) + sh
            h_scr[pl.ds(r0, NORM_ROWS), :] = y.astype(BF16)
            return carry

        lax.fori_loop(0, tm // NORM_ROWS, norm_rows, 0)
        if side:
            os_ref[...] = jnp.dot(h_scr[...], ws_ref[...], preferred_element_type=F32)

    acc = jnp.dot(h_scr[...], w_ref[...], preferred_element_type=F32)
    if relu2:
        acc = jnp.square(jnp.maximum(acc, 0.0))
    o_ref[...] = acc.astype(o_ref.dtype)


def _weight_spec(w, layer, block, index_map):
    if w.ndim == 2:
        return pl.BlockSpec(block, index_map)
    return pl.BlockSpec((None,) + block, lambda *g: (layer,) + index_map(*g))


def _norm_matmul(x, col_block, k, gain, w, *, layer=0, n=None, mod=None, which=0, n_lat=None, relu2=False,
                 w_side=None, out_dtype=F32, tm_cap=1152, tn_cap=1024, name="norm_matmul"):
    bsz, rows, _ = x.shape
    if n is None:
        n = w.shape[-1]
    tm = _pick_tile(rows, tm_cap)
    tn = _pick_tile(n, tn_cap)
    modulate = mod is not None
    side = w_side is not None
    if n_lat is None:
        n_lat = rows
    in_specs = [
        pl.BlockSpec((None, tm, k), lambda b, i, j: (b, i, col_block)),
        pl.BlockSpec((1, k), lambda b, i, j: (0, 0)),
    ]
    args = [x, gain.reshape(1, k)]
    if modulate:
        in_specs += [
            pl.BlockSpec((None, 6, k), lambda b, i, j: (b, 0, 0)),
            pl.BlockSpec((None, 6, k), lambda b, i, j: (bsz, 0, 0)),
        ]
        args += [mod, mod]
    in_specs.append(_weight_spec(w, layer, (k, tn), lambda b, i, j: (0, j)))
    args.append(w)
    out_shape = [jax.ShapeDtypeStruct((bsz, rows, n), out_dtype)]
    out_specs = [pl.BlockSpec((None, tm, tn), lambda b, i, j: (b, i, j))]
    if side:
        in_specs.append(pl.BlockSpec((k, LANES), lambda b, i, j: (0, 0)))
        args.append(w_side)
        out_shape.append(jax.ShapeDtypeStruct((bsz, rows, LANES), F32))
        out_specs.append(pl.BlockSpec((None, tm, LANES), lambda b, i, j: (b, i, 0)))
    kern = functools.partial(_norm_matmul_kernel, tm=tm, n_lat=n_lat, shift_row=3 * which,
                             scale_row=3 * which + 1, modulate=modulate, relu2=relu2, side=side)
    out = pl.pallas_call(
        kern,
        out_shape=out_shape,
        grid=(bsz, rows // tm, n // tn),
        in_specs=in_specs,
        out_specs=out_specs,
        scratch_shapes=[pltpu.VMEM((tm, k), BF16)],
        compiler_params=_params(("parallel", "parallel", "arbitrary")),
        name=name,
    )(*args)
    return out if side else out[0]


def _gate(mb_ref, mc_ref, gate_row, i, tm, n_lat):
    t = i * tm + lax.broadcasted_iota(jnp.int32, (tm, 1), 0)
    return jnp.where(t >= n_lat, mc_ref[gate_row:gate_row + 1, :], mb_ref[gate_row:gate_row + 1, :])


def _out_proj_kernel(a1_ref, a2_ref, w_ref, r_ref, mb_ref, mc_ref, o_ref, *, tm, n_lat, k1, gate_row):
    acc = jnp.dot(a1_ref[...], w_ref[:k1, :], preferred_element_type=F32)
    acc += jnp.dot(a2_ref[...], w_ref[k1:, :], preferred_element_type=F32)
    g = _gate(mb_ref, mc_ref, gate_row, pl.program_id(1), tm, n_lat)
    o_ref[...] = r_ref[...] + g * acc


def _out_proj(a1, c1, a2, c2, w, res, mod, *, gate_row, n_lat, rows, layer=0, tm_cap=1152, tn_cap=1024,
              name="out_proj"):
    bsz = res.shape[0]
    d = w.shape[-1]
    k1 = w.shape[-2] // 2
    tm = _pick_tile(rows, tm_cap)
    tn = _pick_tile(d, tn_cap)
    kern = functools.partial(_out_proj_kernel, tm=tm, n_lat=n_lat, k1=k1, gate_row=gate_row)
    return pl.pallas_call(
        kern,
        out_shape=jax.ShapeDtypeStruct((bsz, rows, d), F32),
        grid=(bsz, rows // tm, d // tn),
        in_specs=[
            pl.BlockSpec((None, tm, k1), lambda b, i, j: (b, i, c1)),
            pl.BlockSpec((None, tm, k1), lambda b, i, j: (b, i, c2)),
            _weight_spec(w, layer, (2 * k1, tn), lambda b, i, j: (0, j)),
            pl.BlockSpec((None, tm, tn), lambda b, i, j: (b, i, j)),
            pl.BlockSpec((None, 6, tn), lambda b, i, j: (b, 0, j)),
            pl.BlockSpec((None, 6, tn), lambda b, i, j: (bsz, 0, j)),
        ],
        out_specs=pl.BlockSpec((None, tm, tn), lambda b, i, j: (b, i, j)),
        compiler_params=_params(("parallel", "parallel", "arbitrary")),
        name=name,
    )(a1, a2, w, res, mod, mod)


def _down_proj_kernel(a_ref, w_ref, r_ref, mb_ref, mc_ref, o_ref, acc_scr, *, tm, n_lat, gate_row):
    kk = pl.program_id(3)

    @pl.when(kk == 0)
    def _():
        acc_scr[...] = jnp.zeros_like(acc_scr)

    acc_scr[...] += jnp.dot(a_ref[...], w_ref[...], preferred_element_type=F32)

    @pl.when(kk == pl.num_programs(3) - 1)
    def _():
        g = _gate(mb_ref, mc_ref, gate_row, pl.program_id(1), tm, n_lat)
        o_ref[...] = r_ref[...] + g * acc_scr[...]


def _down_proj(a, w, res, mod, *, gate_row, n_lat, layer=0, tm_cap=1152, tn_cap=512, tk=4096,
               name="down_proj"):
    bsz, rows, kdim = a.shape
    d = w.shape[-1]
    tm = _pick_tile(rows, tm_cap)
    tn = _pick_tile(d, tn_cap)
    kern = functools.partial(_down_proj_kernel, tm=tm, n_lat=n_lat, gate_row=gate_row)
    return pl.pallas_call(
        kern,
        out_shape=jax.ShapeDtypeStruct((bsz, rows, d), F32),
        grid=(bsz, rows // tm, d // tn, kdim // tk),
        in_specs=[
            pl.BlockSpec((None, tm, tk), lambda b, i, j, k: (b, i, k)),
            _weight_spec(w, layer, (tk, tn), lambda b, i, j, k: (k, j)),
            pl.BlockSpec((None, tm, tn), lambda b, i, j, k: (b, i, j)),
            pl.BlockSpec((None, 6, tn), lambda b, i, j, k: (b, 0, j)),
            pl.BlockSpec((None, 6, tn), lambda b, i, j, k: (bsz, 0, j)),
        ],
        out_specs=pl.BlockSpec((None, tm, tn), lambda b, i, j, k: (b, i, j)),
        scratch_shapes=[pltpu.VMEM((tm, tn), F32)],
        compiler_params=_params(("parallel", "parallel", "parallel", "arbitrary")),
        name=name,
    )(a, w, res, mod, mod)


def _rope_tables(n_lat, n_ctx):
    t = np.arange(n_lat)
    row = (t // GRID_W).astype(np.float64)
    col = (t % GRID_W).astype(np.float64)
    n_freq = A_SUB // 4
    inv_freq = ROPE_BASE ** (-np.arange(n_freq, dtype=np.float64) / n_freq)
    ang = np.concatenate([row[:, None] * inv_freq, col[:, None] * inv_freq], axis=-1)
    cos = np.concatenate([np.cos(ang), np.ones((n_ctx, A_SUB // 2))], axis=0)
    sin = np.concatenate([np.sin(ang), np.zeros((n_ctx, A_SUB // 2))], axis=0)
    cos128 = np.tile(cos, (1, 4))
    sin128 = np.tile(np.concatenate([-sin, sin], axis=1), (1, 2))
    return jnp.asarray(cos128, F32), jnp.asarray(sin128, F32)


def _rope128(x, cos, sin):
    lane = lax.broadcasted_iota(jnp.int32, x.shape, 1)
    first = (lane % A_SUB) < (A_SUB // 2)
    partner = jnp.where(first, pltpu.roll(x, LANES - A_SUB // 2, 1), pltpu.roll(x, A_SUB // 2, 1))
    return x * cos + partner * sin


def _group_rms(x, gain):
    lane = lax.broadcasted_iota(jnp.int32, x.shape, 1)
    lo = lane < A_SUB
    xx = x * x
    s_lo = jnp.sum(jnp.where(lo, xx, 0.0), axis=-1, keepdims=True)
    s_hi = jnp.sum(jnp.where(lo, 0.0, xx), axis=-1, keepdims=True)
    inv = jnp.where(lo, lax.rsqrt(s_lo / A_SUB + EPS), lax.rsqrt(s_hi / A_SUB + EPS))
    return x * inv * gain


def _pick_q_tile(n, cap):
    return max(t for t in range(Q_SUB, cap + 1, Q_SUB) if n % t == 0)


def _softmax_pv(streams, kb_scr, vb_scr, s_scr):
    n = len(streams)
    n_rows = streams[0][0].shape[0]
    dv = vb_scr.shape[1]
    outs = []
    m_prev = None
    for i in range(n + 1):
        score_spans = list(range(streams[i][1], streams[i][2], KEY_TILE)) if i < n else []
        value_spans = list(range(streams[i - 1][1], streams[i - 1][2], KEY_TILE)) if i > 0 else []
        m_acc = jnp.full((n_rows, LANES), -jnp.inf, F32)
        l_acc = jnp.zeros((n_rows, LANES), F32)
        o = jnp.zeros((n_rows, dv), F32)
        for j in range(max(len(score_spans), len(value_spans))):
            if j < len(score_spans):
                ks = score_spans[j]
                s = lax.dot_general(streams[i][0], kb_scr[ks:ks + KEY_TILE, :], NT_DIMS,
                                    preferred_element_type=F32)
                s_scr[i % 2, :, j * KEY_TILE:(j + 1) * KEY_TILE] = s
                for t in range(KEY_TILE // LANES):
                    m_acc = jnp.maximum(m_acc, s[:, t * LANES:(t + 1) * LANES])
            if j < len(value_spans):
                ks = value_spans[j]
                e = jnp.exp2(s_scr[(i - 1) % 2, :, j * KEY_TILE:(j + 1) * KEY_TILE] - m_prev)
                for t in range(KEY_TILE // LANES):
                    l_acc = l_acc + e[:, t * LANES:(t + 1) * LANES]
                o = o + jnp.dot(e.astype(BF16), vb_scr[ks:ks + KEY_TILE, :], preferred_element_type=F32)
        if i > 0:
            outs.append((o, jnp.sum(l_acc, axis=-1, keepdims=True)))
        if i < n:
            m_prev = jnp.max(m_acc, axis=-1, keepdims=True)
    return outs


def _diff_attn_kernel(lam_ref, qg_ref, kg_ref, sg_ref, cq_ref, sq_ref, ck_ref, sk_ref,
                      q_ref, k_ref, v_ref, o_ref, kb_scr, vb_scr, s_scr, *, rows, n_lat, lam_init, tq):
    qi = pl.program_id(2)

    @pl.when(qi == 0)
    def _():
        k = _rope128(_group_rms(k_ref[...].astype(F32), kg_ref[...]), ck_ref[...], sk_ref[...])
        kb_scr[...] = k.astype(BF16)
        vb_scr[...] = v_ref[...].astype(BF16)

    lp = lam_ref[...]
    lam = (jnp.exp(jnp.sum(lp[0:1] * lp[1:2], axis=-1, keepdims=True))
           - jnp.exp(jnp.sum(lp[2:3] * lp[3:4], axis=-1, keepdims=True)) + lam_init)

    def attend(first_key):
        streams = []
        for i, k0 in enumerate(first_key):
            sl = slice(i * Q_SUB, (i + 1) * Q_SUB)
            q = _rope128(_group_rms(q_ref[sl, :].astype(F32), qg_ref[...]), cq_ref[sl, :], sq_ref[sl, :])
            q = q * (A_SUB ** -0.5 * LOG2E)
            lane = lax.broadcasted_iota(jnp.int32, q.shape, 1)
            q12 = jnp.concatenate([jnp.where(lane < A_SUB, q, 0.0), jnp.where(lane < A_SUB, 0.0, q)], axis=0)
            streams.append((q12.astype(BF16), k0, rows))
        outs = _softmax_pv(streams, kb_scr, vb_scr, s_scr)
        for i, (o12, l12) in enumerate(outs):
            o = o12[:Q_SUB] * (1.0 / l12[:Q_SUB]) - o12[Q_SUB:] * (lam / l12[Q_SUB:])
            o_ref[i * Q_SUB:(i + 1) * Q_SUB, :] = (_rms(o, sg_ref[...]) * (1.0 - lam_init)).astype(o_ref.dtype)

    patterns = {}
    for t in range(rows // tq):
        pat = tuple(n_lat if t * tq + r0 >= n_lat else 0 for r0 in range(0, tq, Q_SUB))
        patterns.setdefault(pat, []).append(t)
    for pat, tiles in patterns.items():
        @pl.when((qi >= tiles[0]) & (qi <= tiles[-1]))
        def _(pat=pat):
            attend(pat)


def _diff_attn(proj, lam_p, q_gain, k_gain, sub_gain, cos, sin, *, n_heads, n_lat, lam_init, tq_cap=1152):
    bsz, rows, _ = proj.shape
    tq = _pick_q_tile(rows, tq_cap)
    kern = functools.partial(_diff_attn_kernel, rows=rows, n_lat=n_lat, lam_init=lam_init, tq=tq)
    vec = lambda: pl.BlockSpec((1, LANES), lambda b, h, i: (0, 0))
    return pl.pallas_call(
        kern,
        out_shape=jax.ShapeDtypeStruct((bsz, rows, n_heads * HEAD_DIM), BF16),
        grid=(bsz, n_heads, rows // tq),
        in_specs=[
            pl.BlockSpec((4, A_SUB), lambda b, h, i: (0, 0)),
            vec(), vec(), vec(),
            pl.BlockSpec((tq, LANES), lambda b, h, i: (i, 0)),
            pl.BlockSpec((tq, LANES), lambda b, h, i: (i, 0)),
            pl.BlockSpec((rows, LANES), lambda b, h, i: (0, 0)),
            pl.BlockSpec((rows, LANES), lambda b, h, i: (0, 0)),
            pl.BlockSpec((None, tq, HEAD_DIM), lambda b, h, i: (b, i, h)),
            pl.BlockSpec((None, rows, HEAD_DIM), lambda b, h, i: (b, 0, n_heads + h)),
            pl.BlockSpec((None, rows, HEAD_DIM), lambda b, h, i: (b, 0, 2 * n_heads + h)),
        ],
        out_specs=pl.BlockSpec((None, tq, HEAD_DIM), lambda b, h, i: (b, i, h)),
        scratch_shapes=[pltpu.VMEM((rows, HEAD_DIM), BF16), pltpu.VMEM((rows, HEAD_DIM), BF16),
                        pltpu.VMEM((2, 2 * Q_SUB, rows), F32)],
        compiler_params=_params(("parallel", "parallel", "arbitrary")),
        name="diff_attn",
    )(lam_p, q_gain, k_gain, sub_gain, cos, sin, cos, sin, proj, proj, proj)


def _col(row_vec, eye):
    return jnp.sum(jnp.where(eye, row_vec, 0.0), axis=-1, keepdims=True)


def _gdn_prepare(x_refs, cw_refs, dsts, pad_scr, cols, *, rows, n_lat, tile):
    n_tiles = rows // tile
    for x_ref, cw_ref, dst, kind in zip(x_refs, cw_refs, dsts, "qkv"):
        pad_scr[CONV_PAD:CONV_PAD + rows, :] = x_ref[:, cols].astype(F32)
        cw = cw_ref[:, cols]
        for r in range(n_tiles):
            r0 = r * tile
            t = r0 + lax.broadcasted_iota(jnp.int32, (tile, 1), 0)
            acc = jnp.zeros((tile, HEAD_DIM), F32)
            for j in range(CONV_K):
                dd = j - CONV_K // 2
                xs = pad_scr[CONV_PAD + r0 + dd:CONV_PAD + r0 + dd + tile, :]
                if dd > 0 and r0 < n_lat <= r0 + tile + dd and n_lat < rows:
                    xs = jnp.where((t < n_lat) & (t + dd >= n_lat), 0.0, xs)
                if dd < 0 and r0 + dd < n_lat <= r0 + tile and n_lat < rows:
                    xs = jnp.where((t >= n_lat) & (t + dd < n_lat), 0.0, xs)
                acc = acc + xs * cw[j:j + 1, :]
            y = acc * jax.nn.sigmoid(acc)
            if kind != "v":
                y = y * lax.rsqrt(jnp.sum(y * y, axis=-1, keepdims=True) + EPS)
            if kind == "q":
                y = y * (HEAD_DIM ** -0.5)
            dst[r0:r0 + tile, :] = y


def _gdn_local(gidx, head0, heads, prm_ref, gates_ref, q_scr, k_scr, v_scr,
               w_scr, u_scr, qd_scr, kd_scr, qk_scr, gt_scr):
    per = GROUP // CHUNK
    ii = lax.broadcasted_iota(jnp.int32, (GROUP, GROUP), 0)
    jj = lax.broadcasted_iota(jnp.int32, (GROUP, GROUP), 1)
    eye = ii == jj
    same = (ii // CHUNK) == (jj // CHUNK)
    r0 = pl.multiple_of(gidx * GROUP, GROUP)
    rg = pl.ds(r0, GROUP)
    hds = range(heads)
    chains = [(hh, d) for hh in hds for d in (0, 1)]
    idx = range(len(chains))
    kc = [k_scr[hh, rg, :] for hh in hds]
    vc = [v_scr[hh, rg, :] for hh in hds]
    qc = [q_scr[hh, rg, :] for hh in hds]
    kk = [_dot_nt(kc[hh], kc[hh]) for hh in hds]
    qk = [_dot_nt(qc[hh], kc[hh]) for hh in hds]
    big_g, beta = [], []
    for hh, d in chains:
        alog = jnp.full((1, GROUP), prm_ref[d, head0 + hh], F32)
        dtb = jnp.full((1, GROUP), prm_ref[2 + d, head0 + hh], F32)
        g = -jnp.exp(alog) * jax.nn.softplus(gates_ref[d, hh, gidx] + dtb)
        beta.append(jax.nn.sigmoid(gates_ref[2 + d, hh, gidx]))
        incl = same & ((ii <= jj) if d == 0 else (ii >= jj))
        big_g.append(jnp.sum(jnp.where(incl, _col(g, eye), 0.0), axis=0, keepdims=True))
    g_col, beta_col, g_end_col, gamma, p, t_inv = [], [], [], [], [], []
    for ci, (hh, d) in enumerate(chains):
        g_col.append(_col(big_g[ci], eye))
        beta_col.append(_col(beta[ci], eye))
        end_lane = (ii // CHUNK) * CHUNK + (CHUNK - 1 if d == 0 else 0)
        g_end_col.append(jnp.sum(jnp.where(jj == end_lane, big_g[ci], 0.0), axis=-1, keepdims=True))
        after = same & ((ii >= jj) if d == 0 else (ii <= jj))
        strict = same & ((ii > jj) if d == 0 else (ii < jj))
        gamma.append(jnp.exp(jnp.where(after, g_col[ci] - big_g[ci], -jnp.inf)))
        p.append(jnp.where(strict, -(beta_col[ci] * kk[hh] * gamma[ci]), 0.0))
        t_inv.append(jnp.where(eye, 1.0, 0.0) + p[ci])
    for _ in range(int(math.log2(CHUNK)) - 1):
        p = [_dot(p[ci], p[ci]) for ci in idx]
        t_inv = [t_inv[ci] + _dot(t_inv[ci], p[ci]) for ci in idx]
    e_col = [jnp.exp(g_col[ci]) for ci in idx]
    w = [_dot(t_inv[ci], kc[hh] * (beta_col[ci] * e_col[ci])) for ci, (hh, d) in enumerate(chains)]
    u = [_dot(t_inv[ci], vc[hh] * beta_col[ci]) for ci, (hh, d) in enumerate(chains)]
    for ci, (hh, d) in enumerate(chains):
        w_scr[hh, d, rg, :] = w[ci].astype(BF16)
        u_scr[hh, d, rg, :] = u[ci]
        qd_scr[hh, d, rg, :] = (qc[hh] * e_col[ci]).astype(BF16)
        kd_scr[hh, d, rg, :] = (kc[hh] * jnp.exp(g_end_col[ci] - g_col[ci])).astype(BF16)
        qkg = (qk[hh] * gamma[ci]).astype(BF16)
        for a in range(per):
            lo = a * CHUNK
            qk_scr[hh, d, pl.ds(r0 + lo, CHUNK), :] = qkg[lo:lo + CHUNK, lo:lo + CHUNK]
            end = lo + (CHUNK - 1 if d == 0 else 0)
            gt_scr[hh, d, gidx * per + a] = jnp.broadcast_to(jnp.exp(big_g[ci][:, end:end + 1]), (1, HEAD_DIM))


def _gdn_kernel(prm_ref, gates_ref, cwq_ref, cwk_ref, cwv_ref, og_ref, xq_ref, xk_ref, xv_ref, z_ref,
                o_ref, pad_scr, q_scr, k_scr, v_scr, w_scr, u_scr, qd_scr, kd_scr, qk_scr, gt_scr, oacc_scr,
                *, rows, n_lat, tile, heads):
    n_chunks = rows // CHUNK
    lat_chunks = n_lat // CHUNK
    head0 = pl.program_id(1) * heads
    chains = [(hh, d) for hh in range(heads) for d in (0, 1)]
    pad_scr[0:CONV_PAD, :] = jnp.zeros((CONV_PAD, HEAD_DIM), F32)
    pad_scr[CONV_PAD + rows:, :] = jnp.zeros((CONV_PAD, HEAD_DIM), F32)

    for hh in range(heads):
        cols = slice(hh * HEAD_DIM, (hh + 1) * HEAD_DIM)
        _gdn_prepare((xq_ref, xk_ref, xv_ref), (cwq_ref, cwk_ref, cwv_ref),
                     (q_scr.at[hh], k_scr.at[hh], v_scr.at[hh]), pad_scr, cols, rows=rows, n_lat=n_lat, tile=tile)

    def local_body(gidx, carry):
        _gdn_local(gidx, head0, heads, prm_ref, gates_ref, q_scr, k_scr, v_scr,
                   w_scr, u_scr, qd_scr, kd_scr, qk_scr, gt_scr)
        return carry

    lax.fori_loop(0, rows // GROUP, local_body, 0)

    def scan_body(s, states):
        cidx = [lax.rem(s + lat_chunks, n_chunks) if d == 0 else n_chunks - 1 - s for hh, d in chains]
        rc = [pl.ds(pl.multiple_of(c * CHUNK, CHUNK), CHUNK) for c in cidx]
        sb = [st.astype(BF16) for st in states]
        ws = [jnp.dot(w_scr[hh, d, rc[ci], :], sb[ci], preferred_element_type=F32)
              for ci, (hh, d) in enumerate(chains)]
        vb = [(u_scr[hh, d, rc[ci], :] - ws[ci]).astype(BF16) for ci, (hh, d) in enumerate(chains)]
        new_states = [gt_scr[hh, d, cidx[ci]] * states[ci] + lax.dot_general(
            kd_scr[hh, d, rc[ci], :], vb[ci], TN_DIMS, preferred_element_type=F32)
            for ci, (hh, d) in enumerate(chains)]
        for ci, (hh, d) in enumerate(chains):
            oacc_scr[hh, d, rc[ci], :] = (jnp.dot(qd_scr[hh, d, rc[ci], :], sb[ci], preferred_element_type=F32)
                                          + jnp.dot(qk_scr[hh, d, rc[ci], :], vb[ci], preferred_element_type=F32))
        return tuple(new_states)

    zero = jnp.zeros((HEAD_DIM, HEAD_DIM), F32)
    lax.fori_loop(0, n_chunks, scan_body, (zero,) * len(chains))

    for hh in range(heads):
        cols = slice(hh * HEAD_DIM, (hh + 1) * HEAD_DIM)
        for r0 in range(0, rows, tile):
            o = oacc_scr[hh, 0, r0:r0 + tile, :] + oacc_scr[hh, 1, r0:r0 + tile, :]
            z = z_ref[r0:r0 + tile, cols].astype(F32)
            o_ref[r0:r0 + tile, cols] = (_rms(o, og_ref[...]) * (z * jax.nn.sigmoid(z))).astype(o_ref.dtype)


def _gdn(proj, gates, prm, conv_w, o_gain, *, n_heads, col0, n_lat, tile=256, heads=2):
    bsz, rows, _ = proj.shape
    n_chunks = rows // CHUNK
    wide = heads * HEAD_DIM
    kern = functools.partial(_gdn_kernel, rows=rows, n_lat=n_lat, tile=tile, heads=heads)
    seq = lambda off: pl.BlockSpec((None, rows, wide), lambda b, h: (b, 0, (col0 + off * n_heads) // heads + h))
    cw = lambda off: pl.BlockSpec((CONV_K, wide), lambda b, h: (0, off * n_heads // heads + h))
    per_head = lambda shape, dt: pltpu.VMEM((heads, 2) + shape, dt)
    return pl.pallas_call(
        kern,
        out_shape=jax.ShapeDtypeStruct((bsz, rows, n_heads * HEAD_DIM), BF16),
        grid=(bsz, n_heads // heads),
        in_specs=[
            pl.BlockSpec(memory_space=pltpu.SMEM),
            pl.BlockSpec((None, 4, heads, rows // GROUP, 1, GROUP), lambda b, h: (b, 0, h, 0, 0, 0)),
            cw(0), cw(1), cw(2),
            pl.BlockSpec((1, HEAD_DIM), lambda b, h: (0, 0)),
            seq(0), seq(1), seq(2), seq(3),
        ],
        out_specs=pl.BlockSpec((None, rows, wide), lambda b, h: (b, 0, h)),
        scratch_shapes=[
            pltpu.VMEM((rows + 2 * CONV_PAD, HEAD_DIM), F32),
            pltpu.VMEM((heads, rows, HEAD_DIM), F32),
            pltpu.VMEM((heads, rows, HEAD_DIM), F32),
            pltpu.VMEM((heads, rows, HEAD_DIM), F32),
            per_head((rows, HEAD_DIM), BF16),
            per_head((rows, HEAD_DIM), F32),
            per_head((rows, HEAD_DIM), BF16),
            per_head((rows, HEAD_DIM), BF16),
            per_head((rows, CHUNK), BF16),
            per_head((n_chunks, 1, HEAD_DIM), F32),
            per_head((rows, HEAD_DIM), F32),
        ],
        compiler_params=_params(("parallel", "arbitrary")),
        name="gdn",
    )(prm, gates, conv_w, conv_w, conv_w, o_gain, proj, proj, proj, proj)


def _mla_kernel(qng_ref, qrg_ref, kng_ref, krg_ref, cq_ref, sq_ref, ck_ref, sk_ref,
                qn_ref, qr_ref, kn_ref, v_ref, kr_ref, o_ref, kb_scr, vb_scr, s_scr, *, scale, rows, tq):
    h = pl.program_id(1)

    @pl.when(pl.program_id(2) == 0)
    def _():
        kb_scr[:, :QK_NOPE] = _rms(kn_ref[...].astype(F32), kng_ref[...]).astype(BF16)
        kr = _rope128(_rms(kr_ref[...], krg_ref[...]), ck_ref[...], sk_ref[...])
        kb_scr[:, QK_NOPE:] = kr.astype(BF16)
        vb_scr[...] = v_ref[...].astype(BF16)

    streams = []
    sub = 2 * Q_SUB
    for r0 in range(0, tq, sub):
        sl = slice(r0, r0 + sub)
        qn = _rms(qn_ref[sl, :].astype(F32), qng_ref[...])
        qr = _rope128(_group_rms(qr_ref[sl, :].astype(F32), qrg_ref[...]), cq_ref[sl, :], sq_ref[sl, :])
        lane = lax.broadcasted_iota(jnp.int32, qr.shape, 1)
        mine = (lane // QK_ROPE) == (h % 2)
        q = jnp.concatenate([qn, jnp.where(mine, qr, 0.0)], axis=-1) * (scale * LOG2E)
        streams.append((q.astype(BF16), 0, rows))
    for i, (o, l) in enumerate(_softmax_pv(streams, kb_scr, vb_scr, s_scr)):
        o_ref[i * sub:(i + 1) * sub, :] = (o * (1.0 / l)).astype(o_ref.dtype)


def _mla_attn(q, kv, proj, qn_gain, qr_gain, kn_gain, kr_gain, cos, sin, *, n_heads, n_lat, kr_block,
              tq_cap=1024):
    bsz, rows, _ = kv.shape
    tq = _pick_q_tile(n_lat, tq_cap)
    kern = functools.partial(_mla_kernel, scale=(QK_NOPE + QK_ROPE) ** -0.5, rows=rows, tq=tq)
    vec = lambda: pl.BlockSpec((1, LANES), lambda b, h, i: (0, 0))
    return pl.pallas_call(
        kern,
        out_shape=jax.ShapeDtypeStruct((bsz, n_lat, n_heads * V_HEAD), BF16),
        grid=(bsz, n_heads, n_lat // tq),
        in_specs=[
            vec(), vec(), vec(), vec(),
            pl.BlockSpec((tq, LANES), lambda b, h, i: (i, 0)),
            pl.BlockSpec((tq, LANES), lambda b, h, i: (i, 0)),
            pl.BlockSpec((rows, LANES), lambda b, h, i: (0, 0)),
            pl.BlockSpec((rows, LANES), lambda b, h, i: (0, 0)),
            pl.BlockSpec((None, tq, QK_NOPE), lambda b, h, i: (b, i, h)),
            pl.BlockSpec((None, tq, LANES), lambda b, h, i: (b, i, n_heads + h // 2)),
            pl.BlockSpec((None, rows, QK_NOPE), lambda b, h, i: (b, 0, 2 * h)),
            pl.BlockSpec((None, rows, V_HEAD), lambda b, h, i: (b, 0, 2 * h + 1)),
            pl.BlockSpec((None, rows, LANES), lambda b, h, i: (b, 0, kr_block)),
        ],
        out_specs=pl.BlockSpec((None, tq, V_HEAD), lambda b, h, i: (b, i, h)),
        scratch_shapes=[pltpu.VMEM((rows, QK_NOPE + LANES), BF16), pltpu.VMEM((rows, V_HEAD), BF16),
                        pltpu.VMEM((2, 2 * Q_SUB, rows), F32)],
        compiler_params=_params(("parallel", "parallel", "arbitrary")),
        name="mla_attn",
    )(qn_gain, qr_gain, kn_gain, kr_gain, cos, sin, cos, sin, q, q, kv, kv, proj)


def _pad_cols(w, n):
    return jnp.pad(w, ((0, 0), (0, n - w.shape[1])))


def kernel(x, c, ctx, c_ctx, ada_w, ada_b, norm1, norm2, mlp_w1, mlp_w2, e_w_in, e_q_norm, e_k_norm, e_lam_q1, e_lam_k1, e_lam_q2, e_lam_k2, e_subln, e_conv, e_alog_f, e_alog_b, e_dtb_f, e_dtb_b, e_o_norm, e_w_out, o_w_in, o_qa_norm, o_w_uq, o_kva_norm, o_w_ukv, o_qn_norm, o_qr_norm, o_kn_norm, o_kr_norm, o_w_out):
    bsz, n_lat, d = x.shape
    n_ctx = ctx.shape[1]
    rows = n_lat + n_ctx
    depth = ada_w.shape[0]
    assert depth == 2 and bsz < 8
    mix_heads = d // HEAD_DIM
    a_heads = mix_heads // 2
    b_heads = mix_heads - a_heads
    c_heads = d // V_HEAD
    dup = lambda g: jnp.tile(g.reshape(1, -1), (1, 2))

    svec = jnp.concatenate([c, c_ctx[None, :], jnp.zeros((8 - bsz - 1, d), F32)], axis=0)
    mod = _adaln_mod(svec, ada_w, ada_b).reshape(depth, 8, 6, d)

    cos, sin = _rope_tables(n_lat, n_ctx)
    xs = jnp.concatenate([x, ctx], axis=1)

    a_qk = a_heads * HEAD_DIM
    main = 7 * a_qk
    n_gate = 4 * b_heads
    w_gate = _pad_cols(e_w_in[0][:, main:], LANES)
    proj, gates = _norm_matmul(xs, 0, d, norm1[0], e_w_in.astype(BF16), n=main, mod=mod[0], which=0,
                               n_lat=n_lat, w_side=w_gate.astype(BF16), out_dtype=BF16, name="in_proj0")
    lam_p = jnp.concatenate([e_lam_q1, e_lam_k1, e_lam_q2, e_lam_k2], axis=0)
    o_a = _diff_attn(proj, lam_p, dup(e_q_norm[0]), dup(e_k_norm[0]), e_subln[0].reshape(1, -1), cos, sin,
                     n_heads=a_heads, n_lat=n_lat, lam_init=0.8 - 0.6 * math.exp(-0.3 * 0))
    gates = gates[:, :, :n_gate].reshape(bsz, rows // GROUP, GROUP, 4, b_heads)
    gates = gates.transpose(0, 3, 4, 1, 2).reshape(bsz, 4, b_heads, rows // GROUP, 1, GROUP)
    prm = jnp.concatenate([e_alog_f, e_alog_b, e_dtb_f, e_dtb_b], axis=0)
    o_b = _gdn(proj, gates, prm, e_conv[0], e_o_norm[0].reshape(1, -1), n_heads=b_heads,
               col0=3 * a_heads, n_lat=n_lat)
    w1 = mlp_w1.astype(BF16)
    w2 = mlp_w2.astype(BF16)
    xs = _out_proj(o_a, 0, o_b, 0, e_w_out.astype(BF16), xs, mod[0], gate_row=2, n_lat=n_lat, rows=rows,
                   name="out_proj0")
    hid = _norm_matmul(xs, 0, d, norm2[0], w1, layer=0, mod=mod[0], which=1, n_lat=n_lat,
                       relu2=True, out_dtype=BF16, name="mlp_up0")
    xs = _down_proj(hid, w2, xs, mod[0], layer=0, gate_row=5, n_lat=n_lat, name="mlp_down0")

    w_in1 = jnp.concatenate([o_w_in[0], o_w_in[0][:, Q_LORA + KV_LORA:]], axis=1)
    proj = _norm_matmul(xs, 0, d, norm1[1], w_in1.astype(BF16), mod=mod[1], which=0, n_lat=n_lat,
                        tm_cap=768, tn_cap=384, name="in_proj1")
    w_uq = o_w_uq[0].reshape(Q_LORA, c_heads, QK_NOPE + QK_ROPE)
    w_uq = jnp.concatenate([w_uq[:, :, :QK_NOPE].reshape(Q_LORA, -1), w_uq[:, :, QK_NOPE:].reshape(Q_LORA, -1)],
                           axis=1)
    q = _norm_matmul(proj, 0, Q_LORA, o_qa_norm[0], w_uq.astype(BF16), tn_cap=1024, out_dtype=BF16,
                     tm_cap=768, name="q_up")
    kv = _norm_matmul(proj, 1, KV_LORA, o_kva_norm[0], o_w_ukv.astype(BF16), tn_cap=1024, out_dtype=BF16,
                      tm_cap=768, name="kv_up")
    o_c = _mla_attn(q, kv, proj, o_qn_norm[0].reshape(1, -1), dup(o_qr_norm[0]), o_kn_norm[0].reshape(1, -1),
                    dup(o_kr_norm[0]), cos, sin, n_heads=c_heads, n_lat=n_lat,
                    kr_block=(Q_LORA + KV_LORA) // LANES)
    xl = _out_proj(o_c, 0, o_c, 1, o_w_out.astype(BF16), xs, mod[1], gate_row=2, n_lat=n_lat, rows=n_lat,
                   tm_cap=1024, name="out_proj1")
    hid = _norm_matmul(xl, 0, d, norm2[1], w1, layer=1, mod=mod[1], which=1, n_lat=n_lat,
                       relu2=True, out_dtype=BF16, tm_cap=1024, name="mlp_up1")
    return _down_proj(hid, w2, xl, mod[1], layer=1, gate_row=5, n_lat=n_lat, tm_cap=1024, name="mlp_down1")
```

```python
import functools
import math

import numpy as np
import jax
import jax.numpy as jnp
from jax import lax
from jax.experimental import pallas as pl
from jax.experimental.pallas import tpu as pltpu

EPS = 1e-6
ROPE_BASE = 10000.0
GRID_W = 64
HEAD_DIM = 128
A_SUB = 64
CHUNK = 64
CONV_K = 5
QK_NOPE = 128
QK_ROPE = 64
V_HEAD = 128
Q_LORA = 512
KV_LORA = 512
LANES = 128
CONV_PAD = 8
KEY_TILE = 256
Q_SUB = 128
GROUP = 256
NORM_ELEMS = 64 * 1024
LOG2E = math.log2(math.e)
VMEM_LIMIT = 56 * 1024 * 1024

F32 = jnp.float32
BF16 = jnp.bfloat16
NT_DIMS = (((1,), (1,)), ((), ()))
TN_DIMS = (((0,), (0,)), ((), ()))


def _params(sem):
    return pltpu.CompilerParams(dimension_semantics=sem, vmem_limit_bytes=VMEM_LIMIT)


def _dot(a, b):
    return jnp.dot(a.astype(BF16), b.astype(BF16), preferred_element_type=F32)


def _dot_nt(a, b):
    return lax.dot_general(a.astype(BF16), b.astype(BF16), NT_DIMS, preferred_element_type=F32)


def _dot_tn(a, b):
    return lax.dot_general(a.astype(BF16), b.astype(BF16), TN_DIMS, preferred_element_type=F32)


def _rms(x, gain):
    return x * lax.rsqrt(jnp.mean(x * x, axis=-1, keepdims=True) + EPS) * gain


def _pick_tile(n, cap):
    t = min(n, cap)
    while n % t or t % 8:
        t -= 8
    return t


def _mod_kernel(s_ref, w_ref, b_ref, o_ref):
    s = s_ref[...]
    s = s * jax.nn.sigmoid(s)
    o_ref[...] = _dot(s, w_ref[...]) + b_ref[...]


def _adaln_mod(svec, ada_w, ada_b):
    depth, d, n = ada_w.shape
    tn = _pick_tile(n, 1024)
    return pl.pallas_call(
        _mod_kernel,
        out_shape=jax.ShapeDtypeStruct((depth, 8, n), F32),
        grid=(depth, n // tn),
        in_specs=[
            pl.BlockSpec((8, d), lambda l, j: (0, 0)),
            pl.BlockSpec((None, d, tn), lambda l, j: (l, 0, j)),
            pl.BlockSpec((None, 1, tn), lambda l, j: (l, 0, j)),
        ],
        out_specs=pl.BlockSpec((None, 8, tn), lambda l, j: (l, 0, j)),
        compiler_params=_params(("parallel", "parallel")),
        name="adaln_mod",
    )(svec, ada_w, ada_b.reshape(depth, 1, n))


def _norm_matmul_kernel(*refs, tm, n_lat, shift_row, scale_row, modulate, relu2, side, cast):
    refs = list(refs)
    h_scr = refs.pop()
    if cast:
        cast_out = refs.pop()
    if side:
        os_ref = refs.pop()
    o_ref = refs.pop()
    if cast:
        cast_out[...] = refs.pop()[...].astype(BF16)
    if side:
        ws_ref = refs.pop()
    w_ref = refs.pop()
    x_ref, g_ref = refs[:2]
    mb_ref, mc_ref = refs[2:4] if modulate else (None, None)

    @pl.when(pl.program_id(2) == 0)
    def _():
        chunk = NORM_ELEMS // x_ref.shape[1]

        def norm_rows(c, carry):
            r0 = pl.multiple_of(c * chunk, chunk)
            y = _rms(x_ref[pl.ds(r0, chunk), :], g_ref[...])
            if modulate:
                is_ctx = pl.program_id(1) * tm + r0 >= n_lat
                sc = jnp.where(is_ctx, mc_ref[scale_row:scale_row + 1, :], mb_ref[scale_row:scale_row + 1, :])
                sh = jnp.where(is_ctx, mc_ref[shift_row:shift_row + 1, :], mb_ref[shift_row:shift_row + 1, :])
                y = y * (1.0 + sc) + sh
            h_scr[pl.ds(r0, chunk), :] = y.astype(BF16)
            return carry

        lax.fori_loop(0, tm // chunk, norm_rows, 0)
        if side:
            os_ref[...] = jnp.dot(h_scr[...], ws_ref[...], preferred_element_type=F32)

    acc = jnp.dot(h_scr[...], w_ref[...], preferred_element_type=F32)
    if relu2:
        acc = jnp.square(jnp.maximum(acc, 0.0))
    o_ref[...] = acc.astype(o_ref.dtype)


def _weight_spec(w, layer, block, index_map):
    if w.ndim == 2:
        return pl.BlockSpec(block, index_map)
    return pl.BlockSpec((None,) + block, lambda *g: (layer,) + index_map(*g))


def _cast_rider(src, layer, grid):
    n_steps = math.prod(grid)
    rows, cols = src.shape[-2:]
    blk = rows // n_steps
    assert rows % n_steps == 0 and blk % 16 == 0, (src.shape, grid)

    def lin(*g):
        idx = g[0]
        for size, gi in zip(grid[1:], g[1:]):
            idx = idx * size + gi
        return idx

    if layer is None:
        in_spec = pl.BlockSpec((blk, cols), lambda *g: (lin(*g), 0))
    else:
        in_spec = pl.BlockSpec((None, blk, cols), lambda *g: (layer, lin(*g), 0))
    out_spec = pl.BlockSpec((blk, cols), lambda *g: (lin(*g), 0))
    return in_spec, out_spec, jax.ShapeDtypeStruct((rows, cols), BF16)


def _norm_matmul(x, col_block, k, gain, w, *, layer=0, n=None, mod=None, which=0, n_lat=None, relu2=False,
                 w_side=None, cast=None, out_dtype=F32, tm_cap=1152, tn_cap=1024, name="norm_matmul"):
    bsz, rows, _ = x.shape
    if n is None:
        n = w.shape[-1]
    tm = _pick_tile(rows, tm_cap)
    tn = _pick_tile(n, tn_cap)
    modulate = mod is not None
    side = w_side is not None
    if n_lat is None:
        n_lat = rows
    in_specs = [
        pl.BlockSpec((None, tm, k), lambda b, i, j: (b, i, col_block)),
        pl.BlockSpec((1, k), lambda b, i, j: (0, 0)),
    ]
    args = [x, gain.reshape(1, k)]
    if modulate:
        in_specs += [
            pl.BlockSpec((None, 6, k), lambda b, i, j: (b, 0, 0)),
            pl.BlockSpec((None, 6, k), lambda b, i, j: (bsz, 0, 0)),
        ]
        args += [mod, mod]
    in_specs.append(_weight_spec(w, layer, (k, tn), lambda b, i, j: (0, j)))
    args.append(w)
    out_shape = [jax.ShapeDtypeStruct((bsz, rows, n), out_dtype)]
    out_specs = [pl.BlockSpec((None, tm, tn), lambda b, i, j: (b, i, j))]
    if side:
        in_specs.append(pl.BlockSpec((k, LANES), lambda b, i, j: (0, 0)))
        args.append(w_side)
        out_shape.append(jax.ShapeDtypeStruct((bsz, rows, LANES), F32))
        out_specs.append(pl.BlockSpec((None, tm, LANES), lambda b, i, j: (b, i, 0)))
    grid = (bsz, rows // tm, n // tn)
    if cast is not None:
        c_in, c_out, c_shape = _cast_rider(cast[0], cast[1], grid)
        in_specs.append(c_in)
        args.append(cast[0])
        out_shape.append(c_shape)
        out_specs.append(c_out)
    kern = functools.partial(_norm_matmul_kernel, tm=tm, n_lat=n_lat, shift_row=3 * which,
                             scale_row=3 * which + 1, modulate=modulate, relu2=relu2, side=side,
                             cast=cast is not None)
    out = pl.pallas_call(
        kern,
        out_shape=out_shape,
        grid=grid,
        in_specs=in_specs,
        out_specs=out_specs,
        scratch_shapes=[pltpu.VMEM((tm, k), BF16)],
        compiler_params=_params(("parallel", "parallel", "arbitrary")),
        name=name,
    )(*args)
    return out if len(out) > 1 else out[0]


def _gate(mb_ref, mc_ref, gate_row, i, tm, n_lat):
    t = i * tm + lax.broadcasted_iota(jnp.int32, (tm, 1), 0)
    return jnp.where(t >= n_lat, mc_ref[gate_row:gate_row + 1, :], mb_ref[gate_row:gate_row + 1, :])


def _out_proj_kernel(a1_ref, a2_ref, w_ref, r_ref, mb_ref, mc_ref, o_ref, *, tm, n_lat, k1, gate_row):
    acc = jnp.dot(a1_ref[...], w_ref[:k1, :], preferred_element_type=F32)
    acc += jnp.dot(a2_ref[...], w_ref[k1:, :], preferred_element_type=F32)
    g = _gate(mb_ref, mc_ref, gate_row, pl.program_id(1), tm, n_lat)
    o_ref[...] = r_ref[...] + g * acc


def _out_proj(a1, c1, a2, c2, w, res, mod, *, gate_row, n_lat, rows, layer=0, tm_cap=1152, tn_cap=1024,
              name="out_proj"):
    bsz = res.shape[0]
    d = w.shape[-1]
    k1 = w.shape[-2] // 2
    tm = _pick_tile(rows, tm_cap)
    tn = _pick_tile(d, tn_cap)
    kern = functools.partial(_out_proj_kernel, tm=tm, n_lat=n_lat, k1=k1, gate_row=gate_row)
    return pl.pallas_call(
        kern,
        out_shape=jax.ShapeDtypeStruct((bsz, rows, d), F32),
        grid=(bsz, rows // tm, d // tn),
        in_specs=[
            pl.BlockSpec((None, tm, k1), lambda b, i, j: (b, i, c1)),
            pl.BlockSpec((None, tm, k1), lambda b, i, j: (b, i, c2)),
            _weight_spec(w, layer, (2 * k1, tn), lambda b, i, j: (0, j)),
            pl.BlockSpec((None, tm, tn), lambda b, i, j: (b, i, j)),
            pl.BlockSpec((None, 6, tn), lambda b, i, j: (b, 0, j)),
            pl.BlockSpec((None, 6, tn), lambda b, i, j: (bsz, 0, j)),
        ],
        out_specs=pl.BlockSpec((None, tm, tn), lambda b, i, j: (b, i, j)),
        compiler_params=_params(("parallel", "parallel", "arbitrary")),
        name=name,
    )(a1, a2, w, res, mod, mod)


def _down_proj_kernel(a_ref, w_ref, r_ref, mb_ref, mc_ref, *rest, tm, n_lat, gate_row, cast):
    if cast:
        cast_in, o_ref, cast_out, acc_scr = rest
        cast_out[...] = cast_in[...].astype(BF16)
    else:
        o_ref, acc_scr = rest
    kk = pl.program_id(3)

    @pl.when(kk == 0)
    def _():
        acc_scr[...] = jnp.zeros_like(acc_scr)

    acc_scr[...] += jnp.dot(a_ref[...], w_ref[...], preferred_element_type=F32)

    @pl.when(kk == pl.num_programs(3) - 1)
    def _():
        g = _gate(mb_ref, mc_ref, gate_row, pl.program_id(1), tm, n_lat)
        o_ref[...] = r_ref[...] + g * acc_scr[...]


def _down_proj(a, w, res, mod, *, gate_row, n_lat, layer=0, cast=None, tm_cap=1152, tn_cap=512, tk=4096,
               name="down_proj"):
    bsz, rows, kdim = a.shape
    d = w.shape[-1]
    tm = _pick_tile(rows, tm_cap)
    tn = _pick_tile(d, tn_cap)
    grid = (bsz, rows // tm, d // tn, kdim // tk)
    in_specs = [
        pl.BlockSpec((None, tm, tk), lambda b, i, j, k: (b, i, k)),
        _weight_spec(w, layer, (tk, tn), lambda b, i, j, k: (k, j)),
        pl.BlockSpec((None, tm, tn), lambda b, i, j, k: (b, i, j)),
        pl.BlockSpec((None, 6, tn), lambda b, i, j, k: (b, 0, j)),
        pl.BlockSpec((None, 6, tn), lambda b, i, j, k: (bsz, 0, j)),
    ]
    args = [a, w, res, mod, mod]
    out_shape = [jax.ShapeDtypeStruct((bsz, rows, d), F32)]
    out_specs = [pl.BlockSpec((None, tm, tn), lambda b, i, j, k: (b, i, j))]
    if cast is not None:
        c_in, c_out, c_shape = _cast_rider(cast[0], cast[1], grid)
        in_specs.append(c_in)
        args.append(cast[0])
        out_shape.append(c_shape)
        out_specs.append(c_out)
    kern = functools.partial(_down_proj_kernel, tm=tm, n_lat=n_lat, gate_row=gate_row, cast=cast is not None)
    out = pl.pallas_call(
        kern,
        out_shape=out_shape,
        grid=grid,
        in_specs=in_specs,
        out_specs=out_specs,
        scratch_shapes=[pltpu.VMEM((tm, tn), F32)],
        compiler_params=_params(("parallel", "parallel", "parallel", "arbitrary")),
        name=name,
    )(*args)
    return out if len(out) > 1 else out[0]


def _rope_tables(n_lat, n_ctx):
    t = np.arange(n_lat)
    row = (t // GRID_W).astype(np.float64)
    col = (t % GRID_W).astype(np.float64)
    n_freq = A_SUB // 4
    inv_freq = ROPE_BASE ** (-np.arange(n_freq, dtype=np.float64) / n_freq)
    ang = np.concatenate([row[:, None] * inv_freq, col[:, None] * inv_freq], axis=-1)
    cos = np.concatenate([np.cos(ang), np.ones((n_ctx, A_SUB // 2))], axis=0)
    sin = np.concatenate([np.sin(ang), np.zeros((n_ctx, A_SUB // 2))], axis=0)
    cos128 = np.tile(cos, (1, 4))
    sin128 = np.tile(np.concatenate([-sin, sin], axis=1), (1, 2))
    return jnp.asarray(cos128, F32), jnp.asarray(sin128, F32)


def _rope128(x, cos, sin):
    lane = lax.broadcasted_iota(jnp.int32, x.shape, 1)
    first = (lane % A_SUB) < (A_SUB // 2)
    partner = jnp.where(first, pltpu.roll(x, LANES - A_SUB // 2, 1), pltpu.roll(x, A_SUB // 2, 1))
    return x * cos + partner * sin


def _group_rms(x, gain):
    lane = lax.broadcasted_iota(jnp.int32, x.shape, 1)
    lo = lane < A_SUB
    xx = x * x
    s_lo = jnp.sum(jnp.where(lo, xx, 0.0), axis=-1, keepdims=True)
    s_hi = jnp.sum(jnp.where(lo, 0.0, xx), axis=-1, keepdims=True)
    inv = jnp.where(lo, lax.rsqrt(s_lo / A_SUB + EPS), lax.rsqrt(s_hi / A_SUB + EPS))
    return x * inv * gain


def _pick_q_tile(n, cap):
    return max(t for t in range(Q_SUB, cap + 1, Q_SUB) if n % t == 0)


def _softmax_pv(streams, kb_scr, vb_scr, s_scr):
    n = len(streams)
    n_rows = streams[0][0].shape[0]
    dv = vb_scr.shape[1]
    outs = []
    m_prev = None
    for i in range(n + 1):
        score_spans = list(range(streams[i][1], streams[i][2], KEY_TILE)) if i < n else []
        value_spans = list(range(streams[i - 1][1], streams[i - 1][2], KEY_TILE)) if i > 0 else []
        m_acc = jnp.full((n_rows, LANES), -jnp.inf, F32)
        l_acc = jnp.zeros((n_rows, LANES), F32)
        o = jnp.zeros((n_rows, dv), F32)
        for j in range(max(len(score_spans), len(value_spans))):
            if j < len(score_spans):
                ks = score_spans[j]
                s = lax.dot_general(streams[i][0], kb_scr[ks:ks + KEY_TILE, :], NT_DIMS,
                                    preferred_element_type=F32)
                s_scr[i % 2, :, j * KEY_TILE:(j + 1) * KEY_TILE] = s
                for t in range(KEY_TILE // LANES):
                    m_acc = jnp.maximum(m_acc, s[:, t * LANES:(t + 1) * LANES])
            if j < len(value_spans):
                ks = value_spans[j]
                e = jnp.exp2(s_scr[(i - 1) % 2, :, j * KEY_TILE:(j + 1) * KEY_TILE] - m_prev)
                for t in range(KEY_TILE // LANES):
                    l_acc = l_acc + e[:, t * LANES:(t + 1) * LANES]
                o = o + jnp.dot(e.astype(BF16), vb_scr[ks:ks + KEY_TILE, :], preferred_element_type=F32)
        if i > 0:
            outs.append((o, jnp.sum(l_acc, axis=-1, keepdims=True)))
        if i < n:
            m_prev = jnp.max(m_acc, axis=-1, keepdims=True)
    return outs


def _diff_attn_kernel(lam_ref, qg_ref, kg_ref, sg_ref, cq_ref, sq_ref, ck_ref, sk_ref,
                      q_ref, k_ref, v_ref, cast_in, o_ref, cast_out, kb_scr, vb_scr, s_scr,
                      *, rows, n_lat, lam_init, tq):
    qi = pl.program_id(2)
    cast_out[...] = cast_in[...].astype(BF16)

    @pl.when(qi == 0)
    def _():
        k = _rope128(_group_rms(k_ref[...].astype(F32), kg_ref[...]), ck_ref[...], sk_ref[...])
        kb_scr[...] = k.astype(BF16)
        vb_scr[...] = v_ref[...].astype(BF16)

    lp = lam_ref[...]
    lam = (jnp.exp(jnp.sum(lp[0:1] * lp[1:2], axis=-1, keepdims=True))
           - jnp.exp(jnp.sum(lp[2:3] * lp[3:4], axis=-1, keepdims=True)) + lam_init)

    def attend(first_key):
        streams = []
        for i, k0 in enumerate(first_key):
            sl = slice(i * Q_SUB, (i + 1) * Q_SUB)
            q = _rope128(_group_rms(q_ref[sl, :].astype(F32), qg_ref[...]), cq_ref[sl, :], sq_ref[sl, :])
            q = q * (A_SUB ** -0.5 * LOG2E)
            lane = lax.broadcasted_iota(jnp.int32, q.shape, 1)
            q12 = jnp.concatenate([jnp.where(lane < A_SUB, q, 0.0), jnp.where(lane < A_SUB, 0.0, q)], axis=0)
            streams.append((q12.astype(BF16), k0, rows))
        outs = _softmax_pv(streams, kb_scr, vb_scr, s_scr)
        for i, (o12, l12) in enumerate(outs):
            o = o12[:Q_SUB] * (1.0 / l12[:Q_SUB]) - o12[Q_SUB:] * (lam / l12[Q_SUB:])
            o_ref[i * Q_SUB:(i + 1) * Q_SUB, :] = (_rms(o, sg_ref[...]) * (1.0 - lam_init)).astype(o_ref.dtype)

    patterns = {}
    for t in range(rows // tq):
        pat = tuple(n_lat if t * tq + r0 >= n_lat else 0 for r0 in range(0, tq, Q_SUB))
        patterns.setdefault(pat, []).append(t)
    for pat, tiles in patterns.items():
        @pl.when((qi >= tiles[0]) & (qi <= tiles[-1]))
        def _(pat=pat):
            attend(pat)


def _diff_attn(proj, lam_p, q_gain, k_gain, sub_gain, cos, sin, cast, *, n_heads, n_lat, lam_init, tq_cap=1152):
    bsz, rows, _ = proj.shape
    tq = _pick_q_tile(rows, tq_cap)
    kern = functools.partial(_diff_attn_kernel, rows=rows, n_lat=n_lat, lam_init=lam_init, tq=tq)
    vec = lambda: pl.BlockSpec((1, LANES), lambda b, h, i: (0, 0))
    grid = (bsz, n_heads, rows // tq)
    c_in, c_out, c_shape = _cast_rider(cast[0], cast[1], grid)
    return pl.pallas_call(
        kern,
        out_shape=[jax.ShapeDtypeStruct((bsz, rows, n_heads * HEAD_DIM), BF16), c_shape],
        grid=grid,
        in_specs=[
            pl.BlockSpec((4, A_SUB), lambda b, h, i: (0, 0)),
            vec(), vec(), vec(),
            pl.BlockSpec((tq, LANES), lambda b, h, i: (i, 0)),
            pl.BlockSpec((tq, LANES), lambda b, h, i: (i, 0)),
            pl.BlockSpec((rows, LANES), lambda b, h, i: (0, 0)),
            pl.BlockSpec((rows, LANES), lambda b, h, i: (0, 0)),
            pl.BlockSpec((None, tq, HEAD_DIM), lambda b, h, i: (b, i, h)),
            pl.BlockSpec((None, rows, HEAD_DIM), lambda b, h, i: (b, 0, n_heads + h)),
            pl.BlockSpec((None, rows, HEAD_DIM), lambda b, h, i: (b, 0, 2 * n_heads + h)),
            c_in,
        ],
        out_specs=[pl.BlockSpec((None, tq, HEAD_DIM), lambda b, h, i: (b, i, h)), c_out],
        scratch_shapes=[pltpu.VMEM((rows, HEAD_DIM), BF16), pltpu.VMEM((rows, HEAD_DIM), BF16),
                        pltpu.VMEM((2, 2 * Q_SUB, rows), F32)],
        compiler_params=_params(("parallel", "parallel", "arbitrary")),
        name="diff_attn",
    )(lam_p, q_gain, k_gain, sub_gain, cos, sin, cos, sin, proj, proj, proj, cast[0])


def _col(row_vec, eye):
    return jnp.sum(jnp.where(eye, row_vec, 0.0), axis=-1, keepdims=True)


def _gdn_prepare(x_refs, cw_refs, dsts, pad_scr, cols, *, rows, n_lat, tile):
    n_tiles = rows // tile
    for x_ref, cw_ref, dst, kind in zip(x_refs, cw_refs, dsts, "qkv"):
        pad_scr[CONV_PAD:CONV_PAD + rows, :] = x_ref[:, cols].astype(F32)
        cw = cw_ref[:, cols]
        for r in range(n_tiles):
            r0 = r * tile
            t = r0 + lax.broadcasted_iota(jnp.int32, (tile, 1), 0)
            acc = jnp.zeros((tile, HEAD_DIM), F32)
            for j in range(CONV_K):
                dd = j - CONV_K // 2
                xs = pad_scr[CONV_PAD + r0 + dd:CONV_PAD + r0 + dd + tile, :]
                if dd > 0 and r0 < n_lat <= r0 + tile + dd and n_lat < rows:
                    xs = jnp.where((t < n_lat) & (t + dd >= n_lat), 0.0, xs)
                if dd < 0 and r0 + dd < n_lat <= r0 + tile and n_lat < rows:
                    xs = jnp.where((t >= n_lat) & (t + dd < n_lat), 0.0, xs)
                acc = acc + xs * cw[j:j + 1, :]
            y = acc * jax.nn.sigmoid(acc)
            if kind != "v":
                y = y * lax.rsqrt(jnp.sum(y * y, axis=-1, keepdims=True) + EPS)
            if kind == "q":
                y = y * (HEAD_DIM ** -0.5)
            dst[r0:r0 + tile, :] = y


def _gdn_local(gidx, head0, heads, prm_ref, gates_ref, q_scr, k_scr, v_scr,
               w_scr, u_scr, qd_scr, kd_scr, qk_scr, gt_scr):
    per = GROUP // CHUNK
    ii = lax.broadcasted_iota(jnp.int32, (GROUP, GROUP), 0)
    jj = lax.broadcasted_iota(jnp.int32, (GROUP, GROUP), 1)
    eye = ii == jj
    same = (ii // CHUNK) == (jj // CHUNK)
    r0 = pl.multiple_of(gidx * GROUP, GROUP)
    rg = pl.ds(r0, GROUP)
    hds = range(heads)
    chains = [(hh, d) for hh in hds for d in (0, 1)]
    idx = range(len(chains))
    kc = [k_scr[hh, rg, :] for hh in hds]
    vc = [v_scr[hh, rg, :] for hh in hds]
    qc = [q_scr[hh, rg, :] for hh in hds]
    kk = [_dot_nt(kc[hh], kc[hh]) for hh in hds]
    qk = [_dot_nt(qc[hh], kc[hh]) for hh in hds]
    big_g, beta = [], []
    for hh, d in chains:
        alog = jnp.full((1, GROUP), prm_ref[d, head0 + hh], F32)
        dtb = jnp.full((1, GROUP), prm_ref[2 + d, head0 + hh], F32)
        g = -jnp.exp(alog) * jax.nn.softplus(gates_ref[d, hh, gidx] + dtb)
        beta.append(jax.nn.sigmoid(gates_ref[2 + d, hh, gidx]))
        incl = same & ((ii <= jj) if d == 0 else (ii >= jj))
        big_g.append(jnp.sum(jnp.where(incl, _col(g, eye), 0.0), axis=0, keepdims=True))
    g_col, beta_col, g_end_col, gamma, p, t_inv = [], [], [], [], [], []
    for ci, (hh, d) in enumerate(chains):
        g_col.append(_col(big_g[ci], eye))
        beta_col.append(_col(beta[ci], eye))
        end_lane = (ii // CHUNK) * CHUNK + (CHUNK - 1 if d == 0 else 0)
        g_end_col.append(jnp.sum(jnp.where(jj == end_lane, big_g[ci], 0.0), axis=-1, keepdims=True))
        after = same & ((ii >= jj) if d == 0 else (ii <= jj))
        strict = same & ((ii > jj) if d == 0 else (ii < jj))
        gamma.append(jnp.exp(jnp.where(after, g_col[ci] - big_g[ci], -jnp.inf)))
        p.append(jnp.where(strict, -(beta_col[ci] * kk[hh] * gamma[ci]), 0.0))
        t_inv.append(jnp.where(eye, 1.0, 0.0) + p[ci])
    for _ in range(int(math.log2(CHUNK)) - 1):
        p = [_dot(p[ci], p[ci]) for ci in idx]
        t_inv = [t_inv[ci] + _dot(t_inv[ci], p[ci]) for ci in idx]
    e_col = [jnp.exp(g_col[ci]) for ci in idx]
    w = [_dot(t_inv[ci], kc[hh] * (beta_col[ci] * e_col[ci])) for ci, (hh, d) in enumerate(chains)]
    u = [_dot(t_inv[ci], vc[hh] * beta_col[ci]) for ci, (hh, d) in enumerate(chains)]
    for ci, (hh, d) in enumerate(chains):
        w_scr[hh, d, rg, :] = w[ci].astype(BF16)
        u_scr[hh, d, rg, :] = u[ci]
        qd_scr[hh, d, rg, :] = (qc[hh] * e_col[ci]).astype(BF16)
        kd_scr[hh, d, rg, :] = (kc[hh] * jnp.exp(g_end_col[ci] - g_col[ci])).astype(BF16)
        qkg = (qk[hh] * gamma[ci]).astype(BF16)
        for a in range(per):
            lo = a * CHUNK
            qk_scr[hh, d, pl.ds(r0 + lo, CHUNK), :] = qkg[lo:lo + CHUNK, lo:lo + CHUNK]
            end = lo + (CHUNK - 1 if d == 0 else 0)
            gt_scr[hh, d, gidx * per + a] = jnp.broadcast_to(jnp.exp(big_g[ci][:, end:end + 1]), (1, HEAD_DIM))


def _gdn_kernel(prm_ref, gates_ref, cwq_ref, cwk_ref, cwv_ref, og_ref, xq_ref, xk_ref, xv_ref, z_ref,
                o_ref, pad_scr, q_scr, k_scr, v_scr, w_scr, u_scr, qd_scr, kd_scr, qk_scr, gt_scr, oacc_scr,
                *, rows, n_lat, tile, heads):
    n_chunks = rows // CHUNK
    lat_chunks = n_lat // CHUNK
    head0 = pl.program_id(1) * heads
    chains = [(hh, d) for hh in range(heads) for d in (0, 1)]
    pad_scr[0:CONV_PAD, :] = jnp.zeros((CONV_PAD, HEAD_DIM), F32)
    pad_scr[CONV_PAD + rows:, :] = jnp.zeros((CONV_PAD, HEAD_DIM), F32)

    for hh in range(heads):
        cols = slice(hh * HEAD_DIM, (hh + 1) * HEAD_DIM)
        _gdn_prepare((xq_ref, xk_ref, xv_ref), (cwq_ref, cwk_ref, cwv_ref),
                     (q_scr.at[hh], k_scr.at[hh], v_scr.at[hh]), pad_scr, cols, rows=rows, n_lat=n_lat, tile=tile)

    def local_body(gidx, carry):
        _gdn_local(gidx, head0, heads, prm_ref, gates_ref, q_scr, k_scr, v_scr,
                   w_scr, u_scr, qd_scr, kd_scr, qk_scr, gt_scr)
        return carry

    lax.fori_loop(0, rows // GROUP, local_body, 0)

    def scan_body(s, states):
        cidx = [lax.rem(s + lat_chunks, n_chunks) if d == 0 else n_chunks - 1 - s for hh, d in chains]
        rc = [pl.ds(pl.multiple_of(c * CHUNK, CHUNK), CHUNK) for c in cidx]
        sb = [st.astype(BF16) for st in states]
        ws = [jnp.dot(w_scr[hh, d, rc[ci], :], sb[ci], preferred_element_type=F32)
              for ci, (hh, d) in enumerate(chains)]
        vb = [(u_scr[hh, d, rc[ci], :] - ws[ci]).astype(BF16) for ci, (hh, d) in enumerate(chains)]
        new_states = [gt_scr[hh, d, cidx[ci]] * states[ci] + lax.dot_general(
            kd_scr[hh, d, rc[ci], :], vb[ci], TN_DIMS, preferred_element_type=F32)
            for ci, (hh, d) in enumerate(chains)]
        for ci, (hh, d) in enumerate(chains):
            oacc_scr[hh, d, rc[ci], :] = (jnp.dot(qd_scr[hh, d, rc[ci], :], sb[ci], preferred_element_type=F32)
                                          + jnp.dot(qk_scr[hh, d, rc[ci], :], vb[ci], preferred_element_type=F32))
        return tuple(new_states)

    zero = jnp.zeros((HEAD_DIM, HEAD_DIM), F32)
    lax.fori_loop(0, n_chunks, scan_body, (zero,) * len(chains))

    for hh in range(heads):
        cols = slice(hh * HEAD_DIM, (hh + 1) * HEAD_DIM)
        for r0 in range(0, rows, tile):
            o = oacc_scr[hh, 0, r0:r0 + tile, :] + oacc_scr[hh, 1, r0:r0 + tile, :]
            z = z_ref[r0:r0 + tile, cols].astype(F32)
            o_ref[r0:r0 + tile, cols] = (_rms(o, og_ref[...]) * (z * jax.nn.sigmoid(z))).astype(o_ref.dtype)


def _gdn(proj, gates, prm, conv_w, o_gain, *, n_heads, col0, n_lat, tile=256, heads=2):
    bsz, rows, _ = proj.shape
    n_chunks = rows // CHUNK
    wide = heads * HEAD_DIM
    kern = functools.partial(_gdn_kernel, rows=rows, n_lat=n_lat, tile=tile, heads=heads)
    seq = lambda off: pl.BlockSpec((None, rows, wide), lambda b, h: (b, 0, (col0 + off * n_heads) // heads + h))
    cw = lambda off: pl.BlockSpec((CONV_K, wide), lambda b, h: (0, off * n_heads // heads + h))
    per_head = lambda shape, dt: pltpu.VMEM((heads, 2) + shape, dt)
    return pl.pallas_call(
        kern,
        out_shape=jax.ShapeDtypeStruct((bsz, rows, n_heads * HEAD_DIM), BF16),
        grid=(bsz, n_heads // heads),
        in_specs=[
            pl.BlockSpec(memory_space=pltpu.SMEM),
            pl.BlockSpec((None, 4, heads, rows // GROUP, 1, GROUP), lambda b, h: (b, 0, h, 0, 0, 0)),
            cw(0), cw(1), cw(2),
            pl.BlockSpec((1, HEAD_DIM), lambda b, h: (0, 0)),
            seq(0), seq(1), seq(2), seq(3),
        ],
        out_specs=pl.BlockSpec((None, rows, wide), lambda b, h: (b, 0, h)),
        scratch_shapes=[
            pltpu.VMEM((rows + 2 * CONV_PAD, HEAD_DIM), F32),
            pltpu.VMEM((heads, rows, HEAD_DIM), F32),
            pltpu.VMEM((heads, rows, HEAD_DIM), F32),
            pltpu.VMEM((heads, rows, HEAD_DIM), F32),
            per_head((rows, HEAD_DIM), BF16),
            per_head((rows, HEAD_DIM), F32),
            per_head((rows, HEAD_DIM), BF16),
            per_head((rows, HEAD_DIM), BF16),
            per_head((rows, CHUNK), BF16),
            per_head((n_chunks, 1, HEAD_DIM), F32),
            per_head((rows, HEAD_DIM), F32),
        ],
        compiler_params=_params(("parallel", "arbitrary")),
        name="gdn",
    )(prm, gates, conv_w, conv_w, conv_w, o_gain, proj, proj, proj, proj)


def _mla_kernel(qng_ref, qrg_ref, kng_ref, krg_ref, cq_ref, sq_ref, ck_ref, sk_ref,
                qn_ref, qr_ref, kn_ref, v_ref, kr_ref, o_ref, kb_scr, vb_scr, s_scr, *, scale, rows, tq):
    h = pl.program_id(1)

    @pl.when(pl.program_id(2) == 0)
    def _():
        kb_scr[:, :QK_NOPE] = _rms(kn_ref[...].astype(F32), kng_ref[...]).astype(BF16)
        kr = _rope128(_rms(kr_ref[...], krg_ref[...]), ck_ref[...], sk_ref[...])
        kb_scr[:, QK_NOPE:] = kr.astype(BF16)
        vb_scr[...] = v_ref[...].astype(BF16)

    streams = []
    sub = 2 * Q_SUB
    for r0 in range(0, tq, sub):
        sl = slice(r0, r0 + sub)
        qn = _rms(qn_ref[sl, :].astype(F32), qng_ref[...])
        qr = _rope128(_group_rms(qr_ref[sl, :].astype(F32), qrg_ref[...]), cq_ref[sl, :], sq_ref[sl, :])
        lane = lax.broadcasted_iota(jnp.int32, qr.shape, 1)
        mine = (lane // QK_ROPE) == (h % 2)
        q = jnp.concatenate([qn, jnp.where(mine, qr, 0.0)], axis=-1) * (scale * LOG2E)
        streams.append((q.astype(BF16), 0, rows))
    for i, (o, l) in enumerate(_softmax_pv(streams, kb_scr, vb_scr, s_scr)):
        o_ref[i * sub:(i + 1) * sub, :] = (o * (1.0 / l)).astype(o_ref.dtype)


def _mla_attn(q, kv, proj, qn_gain, qr_gain, kn_gain, kr_gain, cos, sin, *, n_heads, n_lat, kr_block,
              tq_cap=1024):
    bsz, rows, _ = kv.shape
    tq = _pick_q_tile(n_lat, tq_cap)
    kern = functools.partial(_mla_kernel, scale=(QK_NOPE + QK_ROPE) ** -0.5, rows=rows, tq=tq)
    vec = lambda: pl.BlockSpec((1, LANES), lambda b, h, i: (0, 0))
    return pl.pallas_call(
        kern,
        out_shape=jax.ShapeDtypeStruct((bsz, n_lat, n_heads * V_HEAD), BF16),
        grid=(bsz, n_heads, n_lat // tq),
        in_specs=[
            vec(), vec(), vec(), vec(),
            pl.BlockSpec((tq, LANES), lambda b, h, i: (i, 0)),
            pl.BlockSpec((tq, LANES), lambda b, h, i: (i, 0)),
            pl.BlockSpec((rows, LANES), lambda b, h, i: (0, 0)),
            pl.BlockSpec((rows, LANES), lambda b, h, i: (0, 0)),
            pl.BlockSpec((None, tq, QK_NOPE), lambda b, h, i: (b, i, h)),
            pl.BlockSpec((None, tq, LANES), lambda b, h, i: (b, i, n_heads + h // 2)),
            pl.BlockSpec((None, rows, QK_NOPE), lambda b, h, i: (b, 0, 2 * h)),
            pl.BlockSpec((None, rows, V_HEAD), lambda b, h, i: (b, 0, 2 * h + 1)),
            pl.BlockSpec((None, rows, LANES), lambda b, h, i: (b, 0, kr_block)),
        ],
        out_specs=pl.BlockSpec((None, tq, V_HEAD), lambda b, h, i: (b, i, h)),
        scratch_shapes=[pltpu.VMEM((rows, QK_NOPE + LANES), BF16), pltpu.VMEM((rows, V_HEAD), BF16),
                        pltpu.VMEM((2, 2 * Q_SUB, rows), F32)],
        compiler_params=_params(("parallel", "parallel", "arbitrary")),
        name="mla_attn",
    )(qn_gain, qr_gain, kn_gain, kr_gain, cos, sin, cos, sin, q, q, kv, kv, proj)


def _pad_cols(w, n):
    return jnp.pad(w, ((0, 0), (0, n - w.shape[1])))


def kernel(x, c, ctx, c_ctx, ada_w, ada_b, norm1, norm2, mlp_w1, mlp_w2, e_w_in, e_q_norm, e_k_norm, e_lam_q1, e_lam_k1, e_lam_q2, e_lam_k2, e_subln, e_conv, e_alog_f, e_alog_b, e_dtb_f, e_dtb_b, e_o_norm, e_w_out, o_w_in, o_qa_norm, o_w_uq, o_kva_norm, o_w_ukv, o_qn_norm, o_qr_norm, o_kn_norm, o_kr_norm, o_w_out):
    bsz, n_lat, d = x.shape
    n_ctx = ctx.shape[1]
    rows = n_lat + n_ctx
    depth = ada_w.shape[0]
    assert depth == 2 and bsz < 8
    mix_heads = d // HEAD_DIM
    a_heads = mix_heads // 2
    b_heads = mix_heads - a_heads
    c_heads = d // V_HEAD
    dup = lambda g: jnp.tile(g.reshape(1, -1), (1, 2))

    svec = jnp.concatenate([c, c_ctx[None, :], jnp.zeros((8 - bsz - 1, d), F32)], axis=0)
    mod = _adaln_mod(svec, ada_w, ada_b).reshape(depth, 8, 6, d)

    cos, sin = _rope_tables(n_lat, n_ctx)
    xs = jnp.concatenate([x, ctx], axis=1)

    a_qk = a_heads * HEAD_DIM
    main = 7 * a_qk
    n_gate = 4 * b_heads
    w_gate = _pad_cols(e_w_in[0][:, main:], LANES)
    proj, gates = _norm_matmul(xs, 0, d, norm1[0], e_w_in.astype(BF16), n=main, mod=mod[0], which=0,
                               n_lat=n_lat, w_side=w_gate.astype(BF16), out_dtype=BF16, name="in_proj0")
    lam_p = jnp.concatenate([e_lam_q1, e_lam_k1, e_lam_q2, e_lam_k2], axis=0)
    o_a, w1_0 = _diff_attn(proj, lam_p, dup(e_q_norm[0]), dup(e_k_norm[0]), e_subln[0].reshape(1, -1), cos, sin,
                           (mlp_w1, 0), n_heads=a_heads, n_lat=n_lat, lam_init=0.8 - 0.6 * math.exp(-0.3 * 0))
    gates = gates[:, :, :n_gate].reshape(bsz, rows // GROUP, GROUP, 4, b_heads)
    gates = gates.transpose(0, 3, 4, 1, 2).reshape(bsz, 4, b_heads, rows // GROUP, 1, GROUP)
    prm = jnp.concatenate([e_alog_f, e_alog_b, e_dtb_f, e_dtb_b], axis=0)
    o_b = _gdn(proj, gates, prm, e_conv[0], e_o_norm[0].reshape(1, -1), n_heads=b_heads,
               col0=3 * a_heads, n_lat=n_lat)
    xs = _out_proj(o_a, 0, o_b, 0, e_w_out.astype(BF16), xs, mod[0], gate_row=2, n_lat=n_lat, rows=rows,
                   name="out_proj0")
    d_ff = mlp_w2.shape[1]
    hid, w2 = _norm_matmul(xs, 0, d, norm2[0], w1_0, mod=mod[0], which=1, n_lat=n_lat, relu2=True,
                           cast=(mlp_w2.reshape(depth * d_ff, d), None), out_dtype=BF16, name="mlp_up0")
    w2 = w2.reshape(depth, d_ff, d)
    xs, w1_1 = _down_proj(hid, w2, xs, mod[0], layer=0, cast=(mlp_w1, 1), gate_row=5, n_lat=n_lat,
                          name="mlp_down0")

    w_in1 = jnp.concatenate([o_w_in[0], o_w_in[0][:, Q_LORA + KV_LORA:]], axis=1)
    proj = _norm_matmul(xs, 0, d, norm1[1], w_in1.astype(BF16), mod=mod[1], which=0, n_lat=n_lat,
                        tm_cap=768, tn_cap=w_in1.shape[1], name="in_proj1")
    w_uq = o_w_uq[0].reshape(Q_LORA, c_heads, QK_NOPE + QK_ROPE)
    w_uq = jnp.concatenate([w_uq[:, :, :QK_NOPE].reshape(Q_LORA, -1), w_uq[:, :, QK_NOPE:].reshape(Q_LORA, -1)],
                           axis=1)
    q = _norm_matmul(proj, 0, Q_LORA, o_qa_norm[0], w_uq.astype(BF16), tn_cap=1024, out_dtype=BF16,
                     tm_cap=768, name="q_up")
    kv = _norm_matmul(proj, 1, KV_LORA, o_kva_norm[0], o_w_ukv.astype(BF16), tn_cap=1024, out_dtype=BF16,
                      tm_cap=768, name="kv_up")
    o_c = _mla_attn(q, kv, proj, o_qn_norm[0].reshape(1, -1), dup(o_qr_norm[0]), o_kn_norm[0].reshape(1, -1),
                    dup(o_kr_norm[0]), cos, sin, n_heads=c_heads, n_lat=n_lat,
                    kr_block=(Q_LORA + KV_LORA) // LANES)
    xl = _out_proj(o_c, 0, o_c, 1, o_w_out.astype(BF16), xs, mod[1], gate_row=2, n_lat=n_lat, rows=n_lat,
                   tm_cap=1024, name="out_proj1")
    hid = _norm_matmul(xl, 0, d, norm2[1], w1_1, mod=mod[1], which=1, n_lat=n_lat,
                       relu2=True, out_dtype=BF16, tm_cap=1024, name="mlp_up1")
    return _down_proj(hid, w2, xl, mod[1], layer=1, gate_row=5, n_lat=n_lat, tm_cap=1024, name="mlp_down1")
```

```python
import functools
import math

import numpy as np
import jax
import jax.numpy as jnp
from jax import lax
from jax.experimental import pallas as pl
from jax.experimental.pallas import tpu as pltpu

EPS = 1e-6
ROPE_BASE = 10000.0
GRID_W = 64
HEAD_DIM = 128
A_SUB = 64
CHUNK = 64
CONV_K = 5
QK_NOPE = 128
QK_ROPE = 64
V_HEAD = 128
Q_LORA = 512
KV_LORA = 512
LANES = 128
CONV_PAD = 8
KEY_TILE = 256
Q_SUB = 128
GROUP = 256
NORM_ELEMS = 64 * 1024
NORM_SUB = 128
LOG2E = math.log2(math.e)
VMEM_LIMIT = 56 * 1024 * 1024

F32 = jnp.float32
BF16 = jnp.bfloat16
NT_DIMS = (((1,), (1,)), ((), ()))
TN_DIMS = (((0,), (0,)), ((), ()))


def _params(sem):
    return pltpu.CompilerParams(dimension_semantics=sem, vmem_limit_bytes=VMEM_LIMIT)


def _dot(a, b):
    return jnp.dot(a.astype(BF16), b.astype(BF16), preferred_element_type=F32)


def _dot_nt(a, b):
    return lax.dot_general(a.astype(BF16), b.astype(BF16), NT_DIMS, preferred_element_type=F32)


def _dot_tn(a, b):
    return lax.dot_general(a.astype(BF16), b.astype(BF16), TN_DIMS, preferred_element_type=F32)


def _rms(x, gain):
    return x * lax.rsqrt(jnp.mean(x * x, axis=-1, keepdims=True) + EPS) * gain


def _pick_tile(n, cap):
    t = min(n, cap)
    while n % t or t % 8:
        t -= 8
    return t


def _mod_kernel(s_ref, w_ref, b_ref, o_ref):
    s = s_ref[...]
    s = s * jax.nn.sigmoid(s)
    o_ref[...] = _dot(s, w_ref[...]) + b_ref[...]


def _adaln_mod(svec, ada_w, ada_b):
    depth, d, n = ada_w.shape
    tn = _pick_tile(n, 1024)
    return pl.pallas_call(
        _mod_kernel,
        out_shape=jax.ShapeDtypeStruct((depth, 8, n), F32),
        grid=(depth, n // tn),
        in_specs=[
            pl.BlockSpec((8, d), lambda l, j: (0, 0)),
            pl.BlockSpec((None, d, tn), lambda l, j: (l, 0, j)),
            pl.BlockSpec((None, 1, tn), lambda l, j: (l, 0, j)),
        ],
        out_specs=pl.BlockSpec((None, 8, tn), lambda l, j: (l, 0, j)),
        compiler_params=_params(("parallel", "parallel")),
        name="adaln_mod",
    )(svec, ada_w, ada_b.reshape(depth, 1, n))


def _norm_matmul_kernel(*refs, tm, n_lat, shift_row, scale_row, modulate, relu2, side, cast):
    refs = list(refs)
    h_scr = refs.pop()
    if cast:
        cast_out = refs.pop()
    if side:
        os_ref = refs.pop()
    o_ref = refs.pop()
    if cast:
        cast_out[...] = refs.pop()[...].astype(BF16)
    if side:
        ws_ref = refs.pop()
    w_ref = refs.pop()
    x_ref, g_ref = refs[:2]
    mb_ref, mc_ref = refs[2:4] if modulate else (None, None)

    def project(rows):
        acc = jnp.dot(h_scr[rows, :], w_ref[...], preferred_element_type=F32)
        if relu2:
            acc = jnp.square(jnp.maximum(acc, 0.0))
        o_ref[rows, :] = acc.astype(o_ref.dtype)

    @pl.when(pl.program_id(2) == 0)
    def _():
        chunk = NORM_ELEMS // x_ref.shape[1]
        sub = max(chunk, NORM_SUB)
        for s0 in range(0, tm, sub):
            for r0 in range(s0, s0 + sub, chunk):
                y = _rms(x_ref[r0:r0 + chunk, :], g_ref[...])
                if modulate:
                    is_ctx = pl.program_id(1) * tm + r0 >= n_lat
                    sc = jnp.where(is_ctx, mc_ref[scale_row:scale_row + 1, :], mb_ref[scale_row:scale_row + 1, :])
                    sh = jnp.where(is_ctx, mc_ref[shift_row:shift_row + 1, :], mb_ref[shift_row:shift_row + 1, :])
                    y = y * (1.0 + sc) + sh
                h_scr[r0:r0 + chunk, :] = y.astype(BF16)
            rows = slice(s0, s0 + sub)
            project(rows)
            if side:
                os_ref[rows, :] = jnp.dot(h_scr[rows, :], ws_ref[...], preferred_element_type=F32)

    @pl.when(pl.program_id(2) != 0)
    def _():
        project(slice(None))


def _weight_spec(w, layer, block, index_map):
    if w.ndim == 2:
        return pl.BlockSpec(block, index_map)
    return pl.BlockSpec((None,) + block, lambda *g: (layer,) + index_map(*g))


def _cast_rider(src, layer, grid):
    n_steps = math.prod(grid)
    rows, cols = src.shape[-2:]
    blk = rows // n_steps
    assert rows % n_steps == 0 and blk % 16 == 0, (src.shape, grid)

    def lin(*g):
        idx = g[0]
        for size, gi in zip(grid[1:], g[1:]):
            idx = idx * size + gi
        return idx

    if layer is None:
        in_spec = pl.BlockSpec((blk, cols), lambda *g: (lin(*g), 0))
    else:
        in_spec = pl.BlockSpec((None, blk, cols), lambda *g: (layer, lin(*g), 0))
    out_spec = pl.BlockSpec((blk, cols), lambda *g: (lin(*g), 0))
    return in_spec, out_spec, jax.ShapeDtypeStruct((rows, cols), BF16)


def _norm_matmul(x, col_block, k, gain, w, *, layer=0, n=None, mod=None, which=0, n_lat=None, relu2=False,
                 w_side=None, cast=None, out_dtype=F32, tm_cap=1152, tn_cap=1024, name="norm_matmul"):
    bsz, rows, _ = x.shape
    if n is None:
        n = w.shape[-1]
    tm = _pick_tile(rows, tm_cap)
    tn = _pick_tile(n, tn_cap)
    modulate = mod is not None
    side = w_side is not None
    if n_lat is None:
        n_lat = rows
    in_specs = [
        pl.BlockSpec((None, tm, k), lambda b, i, j: (b, i, col_block)),
        pl.BlockSpec((1, k), lambda b, i, j: (0, 0)),
    ]
    args = [x, gain.reshape(1, k)]
    if modulate:
        in_specs += [
            pl.BlockSpec((None, 6, k), lambda b, i, j: (b, 0, 0)),
            pl.BlockSpec((None, 6, k), lambda b, i, j: (bsz, 0, 0)),
        ]
        args += [mod, mod]
    in_specs.append(_weight_spec(w, layer, (k, tn), lambda b, i, j: (0, j)))
    args.append(w)
    out_shape = [jax.ShapeDtypeStruct((bsz, rows, n), out_dtype)]
    out_specs = [pl.BlockSpec((None, tm, tn), lambda b, i, j: (b, i, j))]
    if side:
        in_specs.append(pl.BlockSpec((k, LANES), lambda b, i, j: (0, 0)))
        args.append(w_side)
        out_shape.append(jax.ShapeDtypeStruct((bsz, rows, LANES), F32))
        out_specs.append(pl.BlockSpec((None, tm, LANES), lambda b, i, j: (b, i, 0)))
    grid = (bsz, rows // tm, n // tn)
    if cast is not None:
        c_in, c_out, c_shape = _cast_rider(cast[0], cast[1], grid)
        in_specs.append(c_in)
        args.append(cast[0])
        out_shape.append(c_shape)
        out_specs.append(c_out)
    kern = functools.partial(_norm_matmul_kernel, tm=tm, n_lat=n_lat, shift_row=3 * which,
                             scale_row=3 * which + 1, modulate=modulate, relu2=relu2, side=side,
                             cast=cast is not None)
    out = pl.pallas_call(
        kern,
        out_shape=out_shape,
        grid=grid,
        in_specs=in_specs,
        out_specs=out_specs,
        scratch_shapes=[pltpu.VMEM((tm, k), BF16)],
        compiler_params=_params(("parallel", "parallel", "arbitrary")),
        name=name,
    )(*args)
    return out if len(out) > 1 else out[0]


def _gate(mb_ref, mc_ref, gate_row, i, tm, n_lat):
    t = i * tm + lax.broadcasted_iota(jnp.int32, (tm, 1), 0)
    return jnp.where(t >= n_lat, mc_ref[gate_row:gate_row + 1, :], mb_ref[gate_row:gate_row + 1, :])


def _out_proj_kernel(a1_ref, a2_ref, w_ref, r_ref, mb_ref, mc_ref, o_ref, *, tm, n_lat, k1, gate_row):
    acc = jnp.dot(a1_ref[...], w_ref[:k1, :], preferred_element_type=F32)
    acc += jnp.dot(a2_ref[...], w_ref[k1:, :], preferred_element_type=F32)
    g = _gate(mb_ref, mc_ref, gate_row, pl.program_id(1), tm, n_lat)
    o_ref[...] = r_ref[...] + g * acc


def _out_proj(a1, c1, a2, c2, w, res, mod, *, gate_row, n_lat, rows, layer=0, tm_cap=1152, tn_cap=1024,
              name="out_proj"):
    bsz = res.shape[0]
    d = w.shape[-1]
    k1 = w.shape[-2] // 2
    tm = _pick_tile(rows, tm_cap)
    tn = _pick_tile(d, tn_cap)
    kern = functools.partial(_out_proj_kernel, tm=tm, n_lat=n_lat, k1=k1, gate_row=gate_row)
    return pl.pallas_call(
        kern,
        out_shape=jax.ShapeDtypeStruct((bsz, rows, d), F32),
        grid=(bsz, rows // tm, d // tn),
        in_specs=[
            pl.BlockSpec((None, tm, k1), lambda b, i, j: (b, i, c1)),
            pl.BlockSpec((None, tm, k1), lambda b, i, j: (b, i, c2)),
            _weight_spec(w, layer, (2 * k1, tn), lambda b, i, j: (0, j)),
            pl.BlockSpec((None, tm, tn), lambda b, i, j: (b, i, j)),
            pl.BlockSpec((None, 6, tn), lambda b, i, j: (b, 0, j)),
            pl.BlockSpec((None, 6, tn), lambda b, i, j: (bsz, 0, j)),
        ],
        out_specs=pl.BlockSpec((None, tm, tn), lambda b, i, j: (b, i, j)),
        compiler_params=_params(("parallel", "parallel", "arbitrary")),
        name=name,
    )(a1, a2, w, res, mod, mod)


def _down_proj_kernel(a_ref, w_ref, r_ref, mb_ref, mc_ref, *rest, tm, n_lat, gate_row, cast):
    if cast:
        cast_in, o_ref, cast_out, acc_scr = rest
        cast_out[...] = cast_in[...].astype(BF16)
    else:
        o_ref, acc_scr = rest
    kk = pl.program_id(3)

    @pl.when(kk == 0)
    def _():
        acc_scr[...] = jnp.zeros_like(acc_scr)

    acc_scr[...] += jnp.dot(a_ref[...], w_ref[...], preferred_element_type=F32)

    @pl.when(kk == pl.num_programs(3) - 1)
    def _():
        g = _gate(mb_ref, mc_ref, gate_row, pl.program_id(1), tm, n_lat)
        o_ref[...] = r_ref[...] + g * acc_scr[...]


def _down_proj(a, w, res, mod, *, gate_row, n_lat, layer=0, cast=None, tm_cap=1152, tn_cap=512, tk=4096,
               name="down_proj"):
    bsz, rows, kdim = a.shape
    d = w.shape[-1]
    tm = _pick_tile(rows, tm_cap)
    tn = _pick_tile(d, tn_cap)
    grid = (bsz, rows // tm, d // tn, kdim // tk)
    in_specs = [
        pl.BlockSpec((None, tm, tk), lambda b, i, j, k: (b, i, k)),
        _weight_spec(w, layer, (tk, tn), lambda b, i, j, k: (k, j)),
        pl.BlockSpec((None, tm, tn), lambda b, i, j, k: (b, i, j)),
        pl.BlockSpec((None, 6, tn), lambda b, i, j, k: (b, 0, j)),
        pl.BlockSpec((None, 6, tn), lambda b, i, j, k: (bsz, 0, j)),
    ]
    args = [a, w, res, mod, mod]
    out_shape = [jax.ShapeDtypeStruct((bsz, rows, d), F32)]
    out_specs = [pl.BlockSpec((None, tm, tn), lambda b, i, j, k: (b, i, j))]
    if cast is not None:
        c_in, c_out, c_shape = _cast_rider(cast[0], cast[1], grid)
        in_specs.append(c_in)
        args.append(cast[0])
        out_shape.append(c_shape)
        out_specs.append(c_out)
    kern = functools.partial(_down_proj_kernel, tm=tm, n_lat=n_lat, gate_row=gate_row, cast=cast is not None)
    out = pl.pallas_call(
        kern,
        out_shape=out_shape,
        grid=grid,
        in_specs=in_specs,
        out_specs=out_specs,
        scratch_shapes=[pltpu.VMEM((tm, tn), F32)],
        compiler_params=_params(("parallel", "parallel", "parallel", "arbitrary")),
        name=name,
    )(*args)
    return out if len(out) > 1 else out[0]


def _rope_tables(n_lat, n_ctx):
    t = np.arange(n_lat)
    row = (t // GRID_W).astype(np.float64)
    col = (t % GRID_W).astype(np.float64)
    n_freq = A_SUB // 4
    inv_freq = ROPE_BASE ** (-np.arange(n_freq, dtype=np.float64) / n_freq)
    ang = np.concatenate([row[:, None] * inv_freq, col[:, None] * inv_freq], axis=-1)
    cos = np.concatenate([np.cos(ang), np.ones((n_ctx, A_SUB // 2))], axis=0)
    sin = np.concatenate([np.sin(ang), np.zeros((n_ctx, A_SUB // 2))], axis=0)
    cos128 = np.tile(cos, (1, 4))
    sin128 = np.tile(np.concatenate([-sin, sin], axis=1), (1, 2))
    return jnp.asarray(cos128, F32), jnp.asarray(sin128, F32)


def _rope128(x, cos, sin):
    lane = lax.broadcasted_iota(jnp.int32, x.shape, 1)
    first = (lane % A_SUB) < (A_SUB // 2)
    partner = jnp.where(first, pltpu.roll(x, LANES - A_SUB // 2, 1), pltpu.roll(x, A_SUB // 2, 1))
    return x * cos + partner * sin


def _group_rms(x, gain):
    lane = lax.broadcasted_iota(jnp.int32, x.shape, 1)
    lo = lane < A_SUB
    xx = x * x
    s_lo = jnp.sum(jnp.where(lo, xx, 0.0), axis=-1, keepdims=True)
    s_hi = jnp.sum(jnp.where(lo, 0.0, xx), axis=-1, keepdims=True)
    inv = jnp.where(lo, lax.rsqrt(s_lo / A_SUB + EPS), lax.rsqrt(s_hi / A_SUB + EPS))
    return x * inv * gain


def _pick_q_tile(n, cap):
    return max(t for t in range(Q_SUB, cap + 1, Q_SUB) if n % t == 0)


def _softmax_pv(streams, kb_scr, vb_scr, s_scr):
    n = len(streams)
    n_rows = streams[0][0].shape[0]
    dv = vb_scr.shape[1]
    outs = []
    m_prev = None
    for i in range(n + 1):
        score_spans = list(range(streams[i][1], streams[i][2], KEY_TILE)) if i < n else []
        value_spans = list(range(streams[i - 1][1], streams[i - 1][2], KEY_TILE)) if i > 0 else []
        m_acc = jnp.full((n_rows, LANES), -jnp.inf, F32)
        l_acc = jnp.zeros((n_rows, LANES), F32)
        o = jnp.zeros((n_rows, dv), F32)
        for j in range(max(len(score_spans), len(value_spans))):
            if j < len(score_spans):
                ks = score_spans[j]
                s = lax.dot_general(streams[i][0], kb_scr[ks:ks + KEY_TILE, :], NT_DIMS,
                                    preferred_element_type=F32)
                s_scr[i % 2, :, j * KEY_TILE:(j + 1) * KEY_TILE] = s
                for t in range(KEY_TILE // LANES):
                    m_acc = jnp.maximum(m_acc, s[:, t * LANES:(t + 1) * LANES])
            if j < len(value_spans):
                ks = value_spans[j]
                e = jnp.exp2(s_scr[(i - 1) % 2, :, j * KEY_TILE:(j + 1) * KEY_TILE] - m_prev)
                for t in range(KEY_TILE // LANES):
                    l_acc = l_acc + e[:, t * LANES:(t + 1) * LANES]
                o = o + jnp.dot(e.astype(BF16), vb_scr[ks:ks + KEY_TILE, :], preferred_element_type=F32)
        if i > 0:
            outs.append((o, jnp.sum(l_acc, axis=-1, keepdims=True)))
        if i < n:
            m_prev = jnp.max(m_acc, axis=-1, keepdims=True)
    return outs


def _diff_attn_kernel(lam_ref, qg_ref, kg_ref, sg_ref, cq_ref, sq_ref, ck_ref, sk_ref,
                      q_ref, k_ref, v_ref, cast_in, o_ref, cast_out, kb_scr, vb_scr, s_scr,
                      *, rows, n_lat, lam_init, tq):
    qi = pl.program_id(2)
    cast_out[...] = cast_in[...].astype(BF16)

    @pl.when(qi == 0)
    def _():
        k = _rope128(_group_rms(k_ref[...].astype(F32), kg_ref[...]), ck_ref[...], sk_ref[...])
        kb_scr[...] = k.astype(BF16)
        vb_scr[...] = v_ref[...].astype(BF16)

    lp = lam_ref[...]
    lam = (jnp.exp(jnp.sum(lp[0:1] * lp[1:2], axis=-1, keepdims=True))
           - jnp.exp(jnp.sum(lp[2:3] * lp[3:4], axis=-1, keepdims=True)) + lam_init)

    def attend(first_key):
        streams = []
        for i, k0 in enumerate(first_key):
            sl = slice(i * Q_SUB, (i + 1) * Q_SUB)
            q = _rope128(_group_rms(q_ref[sl, :].astype(F32), qg_ref[...]), cq_ref[sl, :], sq_ref[sl, :])
            q = q * (A_SUB ** -0.5 * LOG2E)
            lane = lax.broadcasted_iota(jnp.int32, q.shape, 1)
            q12 = jnp.concatenate([jnp.where(lane < A_SUB, q, 0.0), jnp.where(lane < A_SUB, 0.0, q)], axis=0)
            streams.append((q12.astype(BF16), k0, rows))
        outs = _softmax_pv(streams, kb_scr, vb_scr, s_scr)
        for i, (o12, l12) in enumerate(outs):
            o = o12[:Q_SUB] * (1.0 / l12[:Q_SUB]) - o12[Q_SUB:] * (lam / l12[Q_SUB:])
            o_ref[i * Q_SUB:(i + 1) * Q_SUB, :] = (_rms(o, sg_ref[...]) * (1.0 - lam_init)).astype(o_ref.dtype)

    patterns = {}
    for t in range(rows // tq):
        pat = tuple(n_lat if t * tq + r0 >= n_lat else 0 for r0 in range(0, tq, Q_SUB))
        patterns.setdefault(pat, []).append(t)
    for pat, tiles in patterns.items():
        @pl.when((qi >= tiles[0]) & (qi <= tiles[-1]))
        def _(pat=pat):
            attend(pat)


def _diff_attn(proj, lam_p, q_gain, k_gain, sub_gain, cos, sin, cast, *, n_heads, n_lat, lam_init, tq_cap=1152):
    bsz, rows, _ = proj.shape
    tq = _pick_q_tile(rows, tq_cap)
    kern = functools.partial(_diff_attn_kernel, rows=rows, n_lat=n_lat, lam_init=lam_init, tq=tq)
    vec = lambda: pl.BlockSpec((1, LANES), lambda b, h, i: (0, 0))
    grid = (bsz, n_heads, rows // tq)
    c_in, c_out, c_shape = _cast_rider(cast[0], cast[1], grid)
    return pl.pallas_call(
        kern,
        out_shape=[jax.ShapeDtypeStruct((bsz, rows, n_heads * HEAD_DIM), BF16), c_shape],
        grid=grid,
        in_specs=[
            pl.BlockSpec((4, A_SUB), lambda b, h, i: (0, 0)),
            vec(), vec(), vec(),
            pl.BlockSpec((tq, LANES), lambda b, h, i: (i, 0)),
            pl.BlockSpec((tq, LANES), lambda b, h, i: (i, 0)),
            pl.BlockSpec((rows, LANES), lambda b, h, i: (0, 0)),
            pl.BlockSpec((rows, LANES), lambda b, h, i: (0, 0)),
            pl.BlockSpec((None, tq, HEAD_DIM), lambda b, h, i: (b, i, h)),
            pl.BlockSpec((None, rows, HEAD_DIM), lambda b, h, i: (b, 0, n_heads + h)),
            pl.BlockSpec((None, rows, HEAD_DIM), lambda b, h, i: (b, 0, 2 * n_heads + h)),
            c_in,
        ],
        out_specs=[pl.BlockSpec((None, tq, HEAD_DIM), lambda b, h, i: (b, i, h)), c_out],
        scratch_shapes=[pltpu.VMEM((rows, HEAD_DIM), BF16), pltpu.VMEM((rows, HEAD_DIM), BF16),
                        pltpu.VMEM((2, 2 * Q_SUB, rows), F32)],
        compiler_params=_params(("parallel", "parallel", "arbitrary")),
        name="diff_attn",
    )(lam_p, q_gain, k_gain, sub_gain, cos, sin, cos, sin, proj, proj, proj, cast[0])


def _col(row_vec, eye):
    return jnp.sum(jnp.where(eye, row_vec, 0.0), axis=-1, keepdims=True)


def _gdn_prepare(x_refs, cw_refs, dsts, pad_scr, cols, *, rows, n_lat, tile):
    n_tiles = rows // tile
    for x_ref, cw_ref, dst, kind in zip(x_refs, cw_refs, dsts, "qkv"):
        pad_scr[CONV_PAD:CONV_PAD + rows, :] = x_ref[:, cols].astype(F32)
        cw = cw_ref[:, cols]
        for r in range(n_tiles):
            r0 = r * tile
            t = r0 + lax.broadcasted_iota(jnp.int32, (tile, 1), 0)
            acc = jnp.zeros((tile, HEAD_DIM), F32)
            for j in range(CONV_K):
                dd = j - CONV_K // 2
                xs = pad_scr[CONV_PAD + r0 + dd:CONV_PAD + r0 + dd + tile, :]
                if dd > 0 and r0 < n_lat <= r0 + tile + dd and n_lat < rows:
                    xs = jnp.where((t < n_lat) & (t + dd >= n_lat), 0.0, xs)
                if dd < 0 and r0 + dd < n_lat <= r0 + tile and n_lat < rows:
                    xs = jnp.where((t >= n_lat) & (t + dd < n_lat), 0.0, xs)
                acc = acc + xs * cw[j:j + 1, :]
            y = acc * jax.nn.sigmoid(acc)
            if kind != "v":
                y = y * lax.rsqrt(jnp.sum(y * y, axis=-1, keepdims=True) + EPS)
            if kind == "q":
                y = y * (HEAD_DIM ** -0.5)
            dst[r0:r0 + tile, :] = y


def _gdn_local(gidx, head0, heads, prm_ref, gates_ref, q_scr, k_scr, v_scr,
               w_scr, u_scr, qd_scr, kd_scr, qk_scr, gt_scr):
    per = GROUP // CHUNK
    ii = lax.broadcasted_iota(jnp.int32, (GROUP, GROUP), 0)
    jj = lax.broadcasted_iota(jnp.int32, (GROUP, GROUP), 1)
    eye = ii == jj
    same = (ii // CHUNK) == (jj // CHUNK)
    r0 = pl.multiple_of(gidx * GROUP, GROUP)
    rg = pl.ds(r0, GROUP)
    hds = range(heads)
    chains = [(hh, d) for hh in hds for d in (0, 1)]
    idx = range(len(chains))
    kc = [k_scr[hh, rg, :] for hh in hds]
    vc = [v_scr[hh, rg, :] for hh in hds]
    qc = [q_scr[hh, rg, :] for hh in hds]
    kk = [_dot_nt(kc[hh], kc[hh]) for hh in hds]
    qk = [_dot_nt(qc[hh], kc[hh]) for hh in hds]
    big_g, beta = [], []
    for hh, d in chains:
        alog = jnp.full((1, GROUP), prm_ref[d, head0 + hh], F32)
        dtb = jnp.full((1, GROUP), prm_ref[2 + d, head0 + hh], F32)
        g = -jnp.exp(alog) * jax.nn.softplus(gates_ref[d, hh, gidx] + dtb)
        beta.append(jax.nn.sigmoid(gates_ref[2 + d, hh, gidx]))
        incl = same & ((ii <= jj) if d == 0 else (ii >= jj))
        big_g.append(jnp.sum(jnp.where(incl, _col(g, eye), 0.0), axis=0, keepdims=True))
    g_col, beta_col, g_end_col, gamma, p, t_inv = [], [], [], [], [], []
    for ci, (hh, d) in enumerate(chains):
        g_col.append(_col(big_g[ci], eye))
        beta_col.append(_col(beta[ci], eye))
        end_lane = (ii // CHUNK) * CHUNK + (CHUNK - 1 if d == 0 else 0)
        g_end_col.append(jnp.sum(jnp.where(jj == end_lane, big_g[ci], 0.0), axis=-1, keepdims=True))
        after = same & ((ii >= jj) if d == 0 else (ii <= jj))
        strict = same & ((ii > jj) if d == 0 else (ii < jj))
        gamma.append(jnp.exp(jnp.where(after, g_col[ci] - big_g[ci], -jnp.inf)))
        p.append(jnp.where(strict, -(beta_col[ci] * kk[hh] * gamma[ci]), 0.0))
        t_inv.append(jnp.where(eye, 1.0, 0.0) + p[ci])
    for _ in range(int(math.log2(CHUNK)) - 1):
        p = [_dot(p[ci], p[ci]) for ci in idx]
        t_inv = [t_inv[ci] + _dot(t_inv[ci], p[ci]) for ci in idx]
    e_col = [jnp.exp(g_col[ci]) for ci in idx]
    w = [_dot(t_inv[ci], kc[hh] * (beta_col[ci] * e_col[ci])) for ci, (hh, d) in enumerate(chains)]
    u = [_dot(t_inv[ci], vc[hh] * beta_col[ci]) for ci, (hh, d) in enumerate(chains)]
    for ci, (hh, d) in enumerate(chains):
        w_scr[hh, d, rg, :] = w[ci].astype(BF16)
        u_scr[hh, d, rg, :] = u[ci]
        qd_scr[hh, d, rg, :] = (qc[hh] * e_col[ci]).astype(BF16)
        kd_scr[hh, d, rg, :] = (kc[hh] * jnp.exp(g_end_col[ci] - g_col[ci])).astype(BF16)
        qkg = (qk[hh] * gamma[ci]).astype(BF16)
        for a in range(per):
            lo = a * CHUNK
            qk_scr[hh, d, pl.ds(r0 + lo, CHUNK), :] = qkg[lo:lo + CHUNK, lo:lo + CHUNK]
            end = lo + (CHUNK - 1 if d == 0 else 0)
            gt_scr[hh, d, gidx * per + a] = jnp.broadcast_to(jnp.exp(big_g[ci][:, end:end + 1]), (1, HEAD_DIM))


def _gdn_kernel(prm_ref, gates_ref, cwq_ref, cwk_ref, cwv_ref, og_ref, xq_ref, xk_ref, xv_ref, z_ref,
                o_ref, pad_scr, q_scr, k_scr, v_scr, w_scr, u_scr, qd_scr, kd_scr, qk_scr, gt_scr, oacc_scr,
                *, rows, n_lat, tile, heads):
    n_chunks = rows // CHUNK
    lat_chunks = n_lat // CHUNK
    head0 = pl.program_id(1) * heads
    chains = [(hh, d) for hh in range(heads) for d in (0, 1)]
    pad_scr[0:CONV_PAD, :] = jnp.zeros((CONV_PAD, HEAD_DIM), F32)
    pad_scr[CONV_PAD + rows:, :] = jnp.zeros((CONV_PAD, HEAD_DIM), F32)

    for hh in range(heads):
        cols = slice(hh * HEAD_DIM, (hh + 1) * HEAD_DIM)
        _gdn_prepare((xq_ref, xk_ref, xv_ref), (cwq_ref, cwk_ref, cwv_ref),
                     (q_scr.at[hh], k_scr.at[hh], v_scr.at[hh]), pad_scr, cols, rows=rows, n_lat=n_lat, tile=tile)

    def local_body(gidx, carry):
        _gdn_local(gidx, head0, heads, prm_ref, gates_ref, q_scr, k_scr, v_scr,
                   w_scr, u_scr, qd_scr, kd_scr, qk_scr, gt_scr)
        return carry

    lax.fori_loop(0, rows // GROUP, local_body, 0)

    def scan_body(s, states):
        cidx = [lax.rem(s + lat_chunks, n_chunks) if d == 0 else n_chunks - 1 - s for hh, d in chains]
        rc = [pl.ds(pl.multiple_of(c * CHUNK, CHUNK), CHUNK) for c in cidx]
        sb = [st.astype(BF16) for st in states]
        ws = [jnp.dot(w_scr[hh, d, rc[ci], :], sb[ci], preferred_element_type=F32)
              for ci, (hh, d) in enumerate(chains)]
        vb = [(u_scr[hh, d, rc[ci], :] - ws[ci]).astype(BF16) for ci, (hh, d) in enumerate(chains)]
        new_states = [gt_scr[hh, d, cidx[ci]] * states[ci] + lax.dot_general(
            kd_scr[hh, d, rc[ci], :], vb[ci], TN_DIMS, preferred_element_type=F32)
            for ci, (hh, d) in enumerate(chains)]
        for ci, (hh, d) in enumerate(chains):
            oacc_scr[hh, d, rc[ci], :] = (jnp.dot(qd_scr[hh, d, rc[ci], :], sb[ci], preferred_element_type=F32)
                                          + jnp.dot(qk_scr[hh, d, rc[ci], :], vb[ci], preferred_element_type=F32))
        return tuple(new_states)

    zero = jnp.zeros((HEAD_DIM, HEAD_DIM), F32)
    lax.fori_loop(0, n_chunks, scan_body, (zero,) * len(chains))

    for hh in range(heads):
        cols = slice(hh * HEAD_DIM, (hh + 1) * HEAD_DIM)
        for r0 in range(0, rows, tile):
            o = oacc_scr[hh, 0, r0:r0 + tile, :] + oacc_scr[hh, 1, r0:r0 + tile, :]
            z = z_ref[r0:r0 + tile, cols].astype(F32)
            o_ref[r0:r0 + tile, cols] = (_rms(o, og_ref[...]) * (z * jax.nn.sigmoid(z))).astype(o_ref.dtype)


def _gdn(proj, gates, prm, conv_w, o_gain, *, n_heads, col0, n_lat, tile=256, heads=2):
    bsz, rows, _ = proj.shape
    n_chunks = rows // CHUNK
    wide = heads * HEAD_DIM
    kern = functools.partial(_gdn_kernel, rows=rows, n_lat=n_lat, tile=tile, heads=heads)
    seq = lambda off: pl.BlockSpec((None, rows, wide), lambda b, h: (b, 0, (col0 + off * n_heads) // heads + h))
    cw = lambda off: pl.BlockSpec((CONV_K, wide), lambda b, h: (0, off * n_heads // heads + h))
    per_head = lambda shape, dt: pltpu.VMEM((heads, 2) + shape, dt)
    return pl.pallas_call(
        kern,
        out_shape=jax.ShapeDtypeStruct((bsz, rows, n_heads * HEAD_DIM), BF16),
        grid=(bsz, n_heads // heads),
        in_specs=[
            pl.BlockSpec(memory_space=pltpu.SMEM),
            pl.BlockSpec((None, 4, heads, rows // GROUP, 1, GROUP), lambda b, h: (b, 0, h, 0, 0, 0)),
            cw(0), cw(1), cw(2),
            pl.BlockSpec((1, HEAD_DIM), lambda b, h: (0, 0)),
            seq(0), seq(1), seq(2), seq(3),
        ],
        out_specs=pl.BlockSpec((None, rows, wide), lambda b, h: (b, 0, h)),
        scratch_shapes=[
            pltpu.VMEM((rows + 2 * CONV_PAD, HEAD_DIM), F32),
            pltpu.VMEM((heads, rows, HEAD_DIM), F32),
            pltpu.VMEM((heads, rows, HEAD_DIM), F32),
            pltpu.VMEM((heads, rows, HEAD_DIM), F32),
            per_head((rows, HEAD_DIM), BF16),
            per_head((rows, HEAD_DIM), F32),
            per_head((rows, HEAD_DIM), BF16),
            per_head((rows, HEAD_DIM), BF16),
            per_head((rows, CHUNK), BF16),
            per_head((n_chunks, 1, HEAD_DIM), F32),
            per_head((rows, HEAD_DIM), F32),
        ],
        compiler_params=_params(("parallel", "arbitrary")),
        name="gdn",
    )(prm, gates, conv_w, conv_w, conv_w, o_gain, proj, proj, proj, proj)


def _mla_kernel(qng_ref, qrg_ref, kng_ref, krg_ref, cq_ref, sq_ref, ck_ref, sk_ref,
                qn_ref, qr_ref, kn_ref, v_ref, kr_ref, o_ref, kb_scr, vb_scr, s_scr, *, scale, rows, tq):
    h = pl.program_id(1)

    @pl.when(pl.program_id(2) == 0)
    def _():
        kb_scr[:, :QK_NOPE] = _rms(kn_ref[...].astype(F32), kng_ref[...]).astype(BF16)
        kr = _rope128(_rms(kr_ref[...], krg_ref[...]), ck_ref[...], sk_ref[...])
        kb_scr[:, QK_NOPE:] = kr.astype(BF16)
        vb_scr[...] = v_ref[...].astype(BF16)

    streams = []
    sub = 2 * Q_SUB
    for r0 in range(0, tq, sub):
        sl = slice(r0, r0 + sub)
        qn = _rms(qn_ref[sl, :].astype(F32), qng_ref[...])
        qr = _rope128(_group_rms(qr_ref[sl, :].astype(F32), qrg_ref[...]), cq_ref[sl, :], sq_ref[sl, :])
        lane = lax.broadcasted_iota(jnp.int32, qr.shape, 1)
        mine = (lane // QK_ROPE) == (h % 2)
        q = jnp.concatenate([qn, jnp.where(mine, qr, 0.0)], axis=-1) * (scale * LOG2E)
        streams.append((q.astype(BF16), 0, rows))
    for i, (o, l) in enumerate(_softmax_pv(streams, kb_scr, vb_scr, s_scr)):
        o_ref[i * sub:(i + 1) * sub, :] = (o * (1.0 / l)).astype(o_ref.dtype)


def _mla_attn(q, kv, proj, qn_gain, qr_gain, kn_gain, kr_gain, cos, sin, *, n_heads, n_lat, kr_block,
              tq_cap=1024):
    bsz, rows, _ = kv.shape
    tq = _pick_q_tile(n_lat, tq_cap)
    kern = functools.partial(_mla_kernel, scale=(QK_NOPE + QK_ROPE) ** -0.5, rows=rows, tq=tq)
    vec = lambda: pl.BlockSpec((1, LANES), lambda b, h, i: (0, 0))
    return pl.pallas_call(
        kern,
        out_shape=jax.ShapeDtypeStruct((bsz, n_lat, n_heads * V_HEAD), BF16),
        grid=(bsz, n_heads, n_lat // tq),
        in_specs=[
            vec(), vec(), vec(), vec(),
            pl.BlockSpec((tq, LANES), lambda b, h, i: (i, 0)),
            pl.BlockSpec((tq, LANES), lambda b, h, i: (i, 0)),
            pl.BlockSpec((rows, LANES), lambda b, h, i: (0, 0)),
            pl.BlockSpec((rows, LANES), lambda b, h, i: (0, 0)),
            pl.BlockSpec((None, tq, QK_NOPE), lambda b, h, i: (b, i, h)),
            pl.BlockSpec((None, tq, LANES), lambda b, h, i: (b, i, n_heads + h // 2)),
            pl.BlockSpec((None, rows, QK_NOPE), lambda b, h, i: (b, 0, 2 * h)),
            pl.BlockSpec((None, rows, V_HEAD), lambda b, h, i: (b, 0, 2 * h + 1)),
            pl.BlockSpec((None, rows, LANES), lambda b, h, i: (b, 0, kr_block)),
        ],
        out_specs=pl.BlockSpec((None, tq, V_HEAD), lambda b, h, i: (b, i, h)),
        scratch_shapes=[pltpu.VMEM((rows, QK_NOPE + LANES), BF16), pltpu.VMEM((rows, V_HEAD), BF16),
                        pltpu.VMEM((2, 2 * Q_SUB, rows), F32)],
        compiler_params=_params(("parallel", "parallel", "arbitrary")),
        name="mla_attn",
    )(qn_gain, qr_gain, kn_gain, kr_gain, cos, sin, cos, sin, q, q, kv, kv, proj)


def _pad_cols(w, n):
    return jnp.pad(w, ((0, 0), (0, n - w.shape[1])))


def kernel(x, c, ctx, c_ctx, ada_w, ada_b, norm1, norm2, mlp_w1, mlp_w2, e_w_in, e_q_norm, e_k_norm, e_lam_q1, e_lam_k1, e_lam_q2, e_lam_k2, e_subln, e_conv, e_alog_f, e_alog_b, e_dtb_f, e_dtb_b, e_o_norm, e_w_out, o_w_in, o_qa_norm, o_w_uq, o_kva_norm, o_w_ukv, o_qn_norm, o_qr_norm, o_kn_norm, o_kr_norm, o_w_out):
    bsz, n_lat, d = x.shape
    n_ctx = ctx.shape[1]
    rows = n_lat + n_ctx
    depth = ada_w.shape[0]
    assert depth == 2 and bsz < 8
    mix_heads = d // HEAD_DIM
    a_heads = mix_heads // 2
    b_heads = mix_heads - a_heads
    c_heads = d // V_HEAD
    dup = lambda g: jnp.tile(g.reshape(1, -1), (1, 2))

    svec = jnp.concatenate([c, c_ctx[None, :], jnp.zeros((8 - bsz - 1, d), F32)], axis=0)
    mod = _adaln_mod(svec, ada_w, ada_b).reshape(depth, 8, 6, d)

    cos, sin = _rope_tables(n_lat, n_ctx)
    xs = jnp.concatenate([x, ctx], axis=1)

    a_qk = a_heads * HEAD_DIM
    main = 7 * a_qk
    n_gate = 4 * b_heads
    w_gate = _pad_cols(e_w_in[0][:, main:], LANES)
    proj, gates = _norm_matmul(xs, 0, d, norm1[0], e_w_in.astype(BF16), n=main, mod=mod[0], which=0,
                               n_lat=n_lat, w_side=w_gate.astype(BF16), out_dtype=BF16, name="in_proj0")
    lam_p = jnp.concatenate([e_lam_q1, e_lam_k1, e_lam_q2, e_lam_k2], axis=0)
    o_a, w1_0 = _diff_attn(proj, lam_p, dup(e_q_norm[0]), dup(e_k_norm[0]), e_subln[0].reshape(1, -1), cos, sin,
                           (mlp_w1, 0), n_heads=a_heads, n_lat=n_lat, lam_init=0.8 - 0.6 * math.exp(-0.3 * 0))
    gates = gates[:, :, :n_gate].reshape(bsz, rows // GROUP, GROUP, 4, b_heads)
    gates = gates.transpose(0, 3, 4, 1, 2).reshape(bsz, 4, b_heads, rows // GROUP, 1, GROUP)
    prm = jnp.concatenate([e_alog_f, e_alog_b, e_dtb_f, e_dtb_b], axis=0)
    o_b = _gdn(proj, gates, prm, e_conv[0], e_o_norm[0].reshape(1, -1), n_heads=b_heads,
               col0=3 * a_heads, n_lat=n_lat)
    xs = _out_proj(o_a, 0, o_b, 0, e_w_out.astype(BF16), xs, mod[0], gate_row=2, n_lat=n_lat, rows=rows,
                   name="out_proj0")
    d_ff = mlp_w2.shape[1]
    hid, w2 = _norm_matmul(xs, 0, d, norm2[0], w1_0, mod=mod[0], which=1, n_lat=n_lat, relu2=True,
                           cast=(mlp_w2.reshape(depth * d_ff, d), None), out_dtype=BF16, name="mlp_up0")
    w2 = w2.reshape(depth, d_ff, d)
    xs, w1_1 = _down_proj(hid, w2, xs, mod[0], layer=0, cast=(mlp_w1, 1), gate_row=5, n_lat=n_lat,
                          name="mlp_down0")

    w_in1 = jnp.concatenate([o_w_in[0], o_w_in[0][:, Q_LORA + KV_LORA:]], axis=1)
    proj = _norm_matmul(xs, 0, d, norm1[1], w_in1.astype(BF16), mod=mod[1], which=0, n_lat=n_lat,
                        tm_cap=768, tn_cap=w_in1.shape[1], name="in_proj1")
    w_uq = o_w_uq[0].reshape(Q_LORA, c_heads, QK_NOPE + QK_ROPE)
    w_uq = jnp.concatenate([w_uq[:, :, :QK_NOPE].reshape(Q_LORA, -1), w_uq[:, :, QK_NOPE:].reshape(Q_LORA, -1)],
                           axis=1)
    q = _norm_matmul(proj, 0, Q_LORA, o_qa_norm[0], w_uq.astype(BF16), tn_cap=1024, out_dtype=BF16,
                     tm_cap=768, name="q_up")
    kv = _norm_matmul(proj, 1, KV_LORA, o_kva_norm[0], o_w_ukv.astype(BF16), tn_cap=1024, out_dtype=BF16,
                      tm_cap=768, name="kv_up")
    o_c = _mla_attn(q, kv, proj, o_qn_norm[0].reshape(1, -1), dup(o_qr_norm[0]), o_kn_norm[0].reshape(1, -1),
                    dup(o_kr_norm[0]), cos, sin, n_heads=c_heads, n_lat=n_lat,
                    kr_block=(Q_LORA + KV_LORA) // LANES)
    xl = _out_proj(o_c, 0, o_c, 1, o_w_out.astype(BF16), xs, mod[1], gate_row=2, n_lat=n_lat, rows=n_lat,
                   tm_cap=1024, name="out_proj1")
    hid = _norm_matmul(xl, 0, d, norm2[1], w1_1, mod=mod[1], which=1, n_lat=n_lat,
                       relu2=True, out_dtype=BF16, tm_cap=1024, name="mlp_up1")
    return _down_proj(hid, w2, xl, mod[1], layer=1, gate_row=5, n_lat=n_lat, tm_cap=1024, name="mlp_down1")
```

```python
import functools
import math

import numpy as np
import jax
import jax.numpy as jnp
from jax import lax
from jax.experimental import pallas as pl
from jax.experimental.pallas import tpu as pltpu

EPS = 1e-6
ROPE_BASE = 10000.0
GRID_W = 64
HEAD_DIM = 128
A_SUB = 64
CHUNK = 64
CONV_K = 5
QK_NOPE = 128
QK_ROPE = 64
V_HEAD = 128
Q_LORA = 512
KV_LORA = 512
LANES = 128
CONV_PAD = 8
KEY_TILE = 256
Q_SUB = 128
GROUP = 256
NORM_ELEMS = 64 * 1024
NORM_SUB = 128
LOG2E = math.log2(math.e)
VMEM_LIMIT = 56 * 1024 * 1024

F32 = jnp.float32
BF16 = jnp.bfloat16
NT_DIMS = (((1,), (1,)), ((), ()))
TN_DIMS = (((0,), (0,)), ((), ()))


def _params(sem):
    return pltpu.CompilerParams(dimension_semantics=sem, vmem_limit_bytes=VMEM_LIMIT)


def _dot(a, b):
    return jnp.dot(a.astype(BF16), b.astype(BF16), preferred_element_type=F32)


def _dot_nt(a, b):
    return lax.dot_general(a.astype(BF16), b.astype(BF16), NT_DIMS, preferred_element_type=F32)


def _dot_tn(a, b):
    return lax.dot_general(a.astype(BF16), b.astype(BF16), TN_DIMS, preferred_element_type=F32)


def _rms(x, gain):
    return x * lax.rsqrt(jnp.mean(x * x, axis=-1, keepdims=True) + EPS) * gain


def _pick_tile(n, cap):
    t = min(n, cap)
    while n % t or t % 8:
        t -= 8
    return t


def _mod_kernel(s_ref, w_ref, b_ref, o_ref):
    s = s_ref[...]
    s = s * jax.nn.sigmoid(s)
    o_ref[...] = _dot(s, w_ref[...]) + b_ref[...]


def _adaln_mod(svec, ada_w, ada_b):
    depth, d, n = ada_w.shape
    tn = _pick_tile(n, 1024)
    return pl.pallas_call(
        _mod_kernel,
        out_shape=jax.ShapeDtypeStruct((depth, 8, n), F32),
        grid=(depth, n // tn),
        in_specs=[
            pl.BlockSpec((8, d), lambda l, j: (0, 0)),
            pl.BlockSpec((None, d, tn), lambda l, j: (l, 0, j)),
            pl.BlockSpec((None, 1, tn), lambda l, j: (l, 0, j)),
        ],
        out_specs=pl.BlockSpec((None, 8, tn), lambda l, j: (l, 0, j)),
        compiler_params=_params(("parallel", "parallel")),
        name="adaln_mod",
    )(svec, ada_w, ada_b.reshape(depth, 1, n))


def _norm_matmul_kernel(*refs, tm, n_lat, shift_row, scale_row, modulate, relu2, side, cast):
    refs = list(refs)
    h_scr = refs.pop()
    if cast:
        cast_out = refs.pop()
    if side:
        os_ref = refs.pop()
    o_ref = refs.pop()
    if cast:
        cast_out[...] = refs.pop()[...].astype(BF16)
    if side:
        ws_ref = refs.pop()
    w_ref = refs.pop()
    x_ref, g_ref = refs[:2]
    mb_ref, mc_ref = refs[2:4] if modulate else (None, None)

    def project(rows):
        acc = jnp.dot(h_scr[rows, :], w_ref[...], preferred_element_type=F32)
        if relu2:
            acc = jnp.square(jnp.maximum(acc, 0.0))
        o_ref[rows, :] = acc.astype(o_ref.dtype)

    @pl.when(pl.program_id(2) == 0)
    def _():
        chunk = NORM_ELEMS // x_ref.shape[1]
        sub = max(chunk, NORM_SUB)
        for s0 in range(0, tm, sub):
            for r0 in range(s0, s0 + sub, chunk):
                y = _rms(x_ref[r0:r0 + chunk, :], g_ref[...])
                if modulate:
                    is_ctx = pl.program_id(1) * tm + r0 >= n_lat
                    sc = jnp.where(is_ctx, mc_ref[scale_row:scale_row + 1, :], mb_ref[scale_row:scale_row + 1, :])
                    sh = jnp.where(is_ctx, mc_ref[shift_row:shift_row + 1, :], mb_ref[shift_row:shift_row + 1, :])
                    y = y * (1.0 + sc) + sh
                h_scr[r0:r0 + chunk, :] = y.astype(BF16)
            rows = slice(s0, s0 + sub)
            project(rows)
            if side:
                os_ref[rows, :] = jnp.dot(h_scr[rows, :], ws_ref[...], preferred_element_type=F32)

    @pl.when(pl.program_id(2) != 0)
    def _():
        project(slice(None))


def _weight_spec(w, layer, block, index_map):
    if w.ndim == 2:
        return pl.BlockSpec(block, index_map)
    return pl.BlockSpec((None,) + block, lambda *g: (layer,) + index_map(*g))


def _cast_rider(src, layer, grid):
    n_steps = math.prod(grid)
    rows, cols = src.shape[-2:]
    blk = rows // n_steps
    assert rows % n_steps == 0 and blk % 16 == 0, (src.shape, grid)

    def lin(*g):
        idx = g[0]
        for size, gi in zip(grid[1:], g[1:]):
            idx = idx * size + gi
        return idx

    if layer is None:
        in_spec = pl.BlockSpec((blk, cols), lambda *g: (lin(*g), 0))
    else:
        in_spec = pl.BlockSpec((None, blk, cols), lambda *g: (layer, lin(*g), 0))
    out_spec = pl.BlockSpec((blk, cols), lambda *g: (lin(*g), 0))
    return in_spec, out_spec, jax.ShapeDtypeStruct((rows, cols), BF16)


def _norm_matmul(x, col_block, k, gain, w, *, layer=0, n=None, mod=None, which=0, n_lat=None, relu2=False,
                 w_side=None, cast=None, out_dtype=F32, tm_cap=1152, tn_cap=1024, name="norm_matmul"):
    bsz, rows, _ = x.shape
    if n is None:
        n = w.shape[-1]
    tm = _pick_tile(rows, tm_cap)
    tn = _pick_tile(n, tn_cap)
    modulate = mod is not None
    side = w_side is not None
    if n_lat is None:
        n_lat = rows
    in_specs = [
        pl.BlockSpec((None, tm, k), lambda b, i, j: (b, i, col_block)),
        pl.BlockSpec((1, k), lambda b, i, j: (0, 0)),
    ]
    args = [x, gain.reshape(1, k)]
    if modulate:
        in_specs += [
            pl.BlockSpec((None, 6, k), lambda b, i, j: (b, 0, 0)),
            pl.BlockSpec((None, 6, k), lambda b, i, j: (bsz, 0, 0)),
        ]
        args += [mod, mod]
    in_specs.append(_weight_spec(w, layer, (k, tn), lambda b, i, j: (0, j)))
    args.append(w)
    out_shape = [jax.ShapeDtypeStruct((bsz, rows, n), out_dtype)]
    out_specs = [pl.BlockSpec((None, tm, tn), lambda b, i, j: (b, i, j))]
    if side:
        in_specs.append(pl.BlockSpec((k, LANES), lambda b, i, j: (0, 0)))
        args.append(w_side)
        out_shape.append(jax.ShapeDtypeStruct((bsz, rows, LANES), F32))
        out_specs.append(pl.BlockSpec((None, tm, LANES), lambda b, i, j: (b, i, 0)))
    grid = (bsz, rows // tm, n // tn)
    if cast is not None:
        c_in, c_out, c_shape = _cast_rider(cast[0], cast[1], grid)
        in_specs.append(c_in)
        args.append(cast[0])
        out_shape.append(c_shape)
        out_specs.append(c_out)
    kern = functools.partial(_norm_matmul_kernel, tm=tm, n_lat=n_lat, shift_row=3 * which,
                             scale_row=3 * which + 1, modulate=modulate, relu2=relu2, side=side,
                             cast=cast is not None)
    out = pl.pallas_call(
        kern,
        out_shape=out_shape,
        grid=grid,
        in_specs=in_specs,
        out_specs=out_specs,
        scratch_shapes=[pltpu.VMEM((tm, k), BF16)],
        compiler_params=_params(("parallel", "parallel", "arbitrary")),
        name=name,
    )(*args)
    return out if len(out) > 1 else out[0]


def _gate(mb_ref, mc_ref, gate_row, i, tm, n_lat):
    t = i * tm + lax.broadcasted_iota(jnp.int32, (tm, 1), 0)
    return jnp.where(t >= n_lat, mc_ref[gate_row:gate_row + 1, :], mb_ref[gate_row:gate_row + 1, :])


def _out_proj_kernel(a1_ref, a2_ref, w_ref, r_ref, mb_ref, mc_ref, o_ref, *, tm, n_lat, k1, gate_row):
    acc = jnp.dot(a1_ref[...], w_ref[:k1, :], preferred_element_type=F32)
    acc += jnp.dot(a2_ref[...], w_ref[k1:, :], preferred_element_type=F32)
    g = _gate(mb_ref, mc_ref, gate_row, pl.program_id(1), tm, n_lat)
    o_ref[...] = r_ref[...] + g * acc


def _out_proj(a1, c1, a2, c2, w, res, mod, *, gate_row, n_lat, rows, layer=0, tm_cap=1152, tn_cap=1024,
              name="out_proj"):
    bsz = res.shape[0]
    d = w.shape[-1]
    k1 = w.shape[-2] // 2
    tm = _pick_tile(rows, tm_cap)
    tn = _pick_tile(d, tn_cap)
    kern = functools.partial(_out_proj_kernel, tm=tm, n_lat=n_lat, k1=k1, gate_row=gate_row)
    return pl.pallas_call(
        kern,
        out_shape=jax.ShapeDtypeStruct((bsz, rows, d), F32),
        grid=(bsz, rows // tm, d // tn),
        in_specs=[
            pl.BlockSpec((None, tm, k1), lambda b, i, j: (b, i, c1)),
            pl.BlockSpec((None, tm, k1), lambda b, i, j: (b, i, c2)),
            _weight_spec(w, layer, (2 * k1, tn), lambda b, i, j: (0, j)),
            pl.BlockSpec((None, tm, tn), lambda b, i, j: (b, i, j)),
            pl.BlockSpec((None, 6, tn), lambda b, i, j: (b, 0, j)),
            pl.BlockSpec((None, 6, tn), lambda b, i, j: (bsz, 0, j)),
        ],
        out_specs=pl.BlockSpec((None, tm, tn), lambda b, i, j: (b, i, j)),
        compiler_params=_params(("parallel", "parallel", "arbitrary")),
        name=name,
    )(a1, a2, w, res, mod, mod)


def _down_proj_kernel(a_ref, w_ref, r_ref, mb_ref, mc_ref, *rest, tm, n_lat, gate_row, cast):
    if cast:
        cast_in, o_ref, cast_out, acc_scr = rest
        cast_out[...] = cast_in[...].astype(BF16)
    else:
        o_ref, acc_scr = rest
    kk = pl.program_id(3)

    @pl.when(kk == 0)
    def _():
        acc_scr[...] = jnp.zeros_like(acc_scr)

    acc_scr[...] += jnp.dot(a_ref[...], w_ref[...], preferred_element_type=F32)

    @pl.when(kk == pl.num_programs(3) - 1)
    def _():
        g = _gate(mb_ref, mc_ref, gate_row, pl.program_id(1), tm, n_lat)
        o_ref[...] = r_ref[...] + g * acc_scr[...]


def _down_proj(a, w, res, mod, *, gate_row, n_lat, layer=0, cast=None, tm_cap=1152, tn_cap=512, tk=4096,
               name="down_proj"):
    bsz, rows, kdim = a.shape
    d = w.shape[-1]
    tm = _pick_tile(rows, tm_cap)
    tn = _pick_tile(d, tn_cap)
    grid = (bsz, rows // tm, d // tn, kdim // tk)
    in_specs = [
        pl.BlockSpec((None, tm, tk), lambda b, i, j, k: (b, i, k)),
        _weight_spec(w, layer, (tk, tn), lambda b, i, j, k: (k, j)),
        pl.BlockSpec((None, tm, tn), lambda b, i, j, k: (b, i, j)),
        pl.BlockSpec((None, 6, tn), lambda b, i, j, k: (b, 0, j)),
        pl.BlockSpec((None, 6, tn), lambda b, i, j, k: (bsz, 0, j)),
    ]
    args = [a, w, res, mod, mod]
    out_shape = [jax.ShapeDtypeStruct((bsz, rows, d), F32)]
    out_specs = [pl.BlockSpec((None, tm, tn), lambda b, i, j, k: (b, i, j))]
    if cast is not None:
        c_in, c_out, c_shape = _cast_rider(cast[0], cast[1], grid)
        in_specs.append(c_in)
        args.append(cast[0])
        out_shape.append(c_shape)
        out_specs.append(c_out)
    kern = functools.partial(_down_proj_kernel, tm=tm, n_lat=n_lat, gate_row=gate_row, cast=cast is not None)
    out = pl.pallas_call(
        kern,
        out_shape=out_shape,
        grid=grid,
        in_specs=in_specs,
        out_specs=out_specs,
        scratch_shapes=[pltpu.VMEM((tm, tn), F32)],
        compiler_params=_params(("parallel", "parallel", "parallel", "arbitrary")),
        name=name,
    )(*args)
    return out if len(out) > 1 else out[0]


def _rope_tables(n_lat, n_ctx):
    t = np.arange(n_lat)
    row = (t // GRID_W).astype(np.float64)
    col = (t % GRID_W).astype(np.float64)
    n_freq = A_SUB // 4
    inv_freq = ROPE_BASE ** (-np.arange(n_freq, dtype=np.float64) / n_freq)
    ang = np.concatenate([row[:, None] * inv_freq, col[:, None] * inv_freq], axis=-1)
    cos = np.concatenate([np.cos(ang), np.ones((n_ctx, A_SUB // 2))], axis=0)
    sin = np.concatenate([np.sin(ang), np.zeros((n_ctx, A_SUB // 2))], axis=0)
    cos128 = np.tile(cos, (1, 4))
    sin128 = np.tile(np.concatenate([-sin, sin], axis=1), (1, 2))
    return jnp.asarray(cos128, F32), jnp.asarray(sin128, F32)


def _rope128(x, cos, sin):
    lane = lax.broadcasted_iota(jnp.int32, x.shape, 1)
    first = (lane % A_SUB) < (A_SUB // 2)
    partner = jnp.where(first, pltpu.roll(x, LANES - A_SUB // 2, 1), pltpu.roll(x, A_SUB // 2, 1))
    return x * cos + partner * sin


def _group_rms(x, gain):
    lane = lax.broadcasted_iota(jnp.int32, x.shape, 1)
    lo = lane < A_SUB
    xx = x * x
    s_lo = jnp.sum(jnp.where(lo, xx, 0.0), axis=-1, keepdims=True)
    s_hi = jnp.sum(jnp.where(lo, 0.0, xx), axis=-1, keepdims=True)
    inv = jnp.where(lo, lax.rsqrt(s_lo / A_SUB + EPS), lax.rsqrt(s_hi / A_SUB + EPS))
    return x * inv * gain


def _pick_q_tile(n, cap):
    return max(t for t in range(Q_SUB, cap + 1, Q_SUB) if n % t == 0)


def _softmax_pv(streams, kb_scr, vb_scr, s_scr):
    n = len(streams)
    n_rows = streams[0][0].shape[0]
    dv = vb_scr.shape[1]
    outs = []
    m_prev = None
    for i in range(n + 1):
        score_spans = list(range(streams[i][1], streams[i][2], KEY_TILE)) if i < n else []
        value_spans = list(range(streams[i - 1][1], streams[i - 1][2], KEY_TILE)) if i > 0 else []
        m_acc = jnp.full((n_rows, LANES), -jnp.inf, F32)
        l_acc = jnp.zeros((n_rows, LANES), F32)
        o = jnp.zeros((n_rows, dv), F32)
        for j in range(max(len(score_spans), len(value_spans))):
            if j < len(score_spans):
                ks = score_spans[j]
                s = lax.dot_general(streams[i][0], kb_scr[ks:ks + KEY_TILE, :], NT_DIMS,
                                    preferred_element_type=F32)
                s_scr[i % 2, :, j * KEY_TILE:(j + 1) * KEY_TILE] = s
                for t in range(KEY_TILE // LANES):
                    m_acc = jnp.maximum(m_acc, s[:, t * LANES:(t + 1) * LANES])
            if j < len(value_spans):
                ks = value_spans[j]
                e = jnp.exp2(s_scr[(i - 1) % 2, :, j * KEY_TILE:(j + 1) * KEY_TILE] - m_prev)
                for t in range(KEY_TILE // LANES):
                    l_acc = l_acc + e[:, t * LANES:(t + 1) * LANES]
                o = o + jnp.dot(e.astype(BF16), vb_scr[ks:ks + KEY_TILE, :], preferred_element_type=F32)
        if i > 0:
            outs.append((o, jnp.sum(l_acc, axis=-1, keepdims=True)))
        if i < n:
            m_prev = jnp.max(m_acc, axis=-1, keepdims=True)
    return outs


def _diff_attn_kernel(lam_ref, qg_ref, kg_ref, sg_ref, cq_ref, sq_ref, ck_ref, sk_ref,
                      q_ref, k_ref, v_ref, cast_in, o_ref, cast_out, kb_scr, vb_scr, s_scr,
                      *, rows, n_lat, lam_init, tq):
    qi = pl.program_id(2)
    cast_out[...] = cast_in[...].astype(BF16)

    @pl.when(qi == 0)
    def _():
        k = _rope128(_group_rms(k_ref[...].astype(F32), kg_ref[...]), ck_ref[...], sk_ref[...])
        kb_scr[...] = k.astype(BF16)
        vb_scr[...] = v_ref[...].astype(BF16)

    lp = lam_ref[...]
    lam = (jnp.exp(jnp.sum(lp[0:1] * lp[1:2], axis=-1, keepdims=True))
           - jnp.exp(jnp.sum(lp[2:3] * lp[3:4], axis=-1, keepdims=True)) + lam_init)

    def attend(first_key):
        streams = []
        for i, k0 in enumerate(first_key):
            sl = slice(i * Q_SUB, (i + 1) * Q_SUB)
            q = _rope128(_group_rms(q_ref[sl, :].astype(F32), qg_ref[...]), cq_ref[sl, :], sq_ref[sl, :])
            q = q * (A_SUB ** -0.5 * LOG2E)
            lane = lax.broadcasted_iota(jnp.int32, q.shape, 1)
            q12 = jnp.concatenate([jnp.where(lane < A_SUB, q, 0.0), jnp.where(lane < A_SUB, 0.0, q)], axis=0)
            streams.append((q12.astype(BF16), k0, rows))
        outs = _softmax_pv(streams, kb_scr, vb_scr, s_scr)
        for i, (o12, l12) in enumerate(outs):
            o = o12[:Q_SUB] * (1.0 / l12[:Q_SUB]) - o12[Q_SUB:] * (lam / l12[Q_SUB:])
            o_ref[i * Q_SUB:(i + 1) * Q_SUB, :] = (_rms(o, sg_ref[...]) * (1.0 - lam_init)).astype(o_ref.dtype)

    patterns = {}
    for t in range(rows // tq):
        pat = tuple(n_lat if t * tq + r0 >= n_lat else 0 for r0 in range(0, tq, Q_SUB))
        patterns.setdefault(pat, []).append(t)
    for pat, tiles in patterns.items():
        @pl.when((qi >= tiles[0]) & (qi <= tiles[-1]))
        def _(pat=pat):
            attend(pat)


def _diff_attn(proj, lam_p, q_gain, k_gain, sub_gain, cos, sin, cast, *, n_heads, n_lat, lam_init, tq_cap=1152):
    bsz, rows, _ = proj.shape
    tq = _pick_q_tile(rows, tq_cap)
    kern = functools.partial(_diff_attn_kernel, rows=rows, n_lat=n_lat, lam_init=lam_init, tq=tq)
    vec = lambda: pl.BlockSpec((1, LANES), lambda b, h, i: (0, 0))
    grid = (bsz, n_heads, rows // tq)
    c_in, c_out, c_shape = _cast_rider(cast[0], cast[1], grid)
    return pl.pallas_call(
        kern,
        out_shape=[jax.ShapeDtypeStruct((bsz, rows, n_heads * HEAD_DIM), BF16), c_shape],
        grid=grid,
        in_specs=[
            pl.BlockSpec((4, A_SUB), lambda b, h, i: (0, 0)),
            vec(), vec(), vec(),
            pl.BlockSpec((tq, LANES), lambda b, h, i: (i, 0)),
            pl.BlockSpec((tq, LANES), lambda b, h, i: (i, 0)),
            pl.BlockSpec((rows, LANES), lambda b, h, i: (0, 0)),
            pl.BlockSpec((rows, LANES), lambda b, h, i: (0, 0)),
            pl.BlockSpec((None, tq, HEAD_DIM), lambda b, h, i: (b, i, h)),
            pl.BlockSpec((None, rows, HEAD_DIM), lambda b, h, i: (b, 0, n_heads + h)),
            pl.BlockSpec((None, rows, HEAD_DIM), lambda b, h, i: (b, 0, 2 * n_heads + h)),
            c_in,
        ],
        out_specs=[pl.BlockSpec((None, tq, HEAD_DIM), lambda b, h, i: (b, i, h)), c_out],
        scratch_shapes=[pltpu.VMEM((rows, HEAD_DIM), BF16), pltpu.VMEM((rows, HEAD_DIM), BF16),
                        pltpu.VMEM((2, 2 * Q_SUB, rows), F32)],
        compiler_params=_params(("parallel", "parallel", "arbitrary")),
        name="diff_attn",
    )(lam_p, q_gain, k_gain, sub_gain, cos, sin, cos, sin, proj, proj, proj, cast[0])


def _col(row_vec, eye):
    return jnp.sum(jnp.where(eye, row_vec, 0.0), axis=-1, keepdims=True)


def _gdn_prepare(x_refs, cw_refs, dsts, pad_scr, cols, *, rows, n_lat, tile):
    n_tiles = rows // tile
    for x_ref, cw_ref, dst, kind in zip(x_refs, cw_refs, dsts, "qkv"):
        pad_scr[CONV_PAD:CONV_PAD + rows, :] = x_ref[:, cols].astype(F32)
        cw = cw_ref[:, cols]
        for r in range(n_tiles):
            r0 = r * tile
            t = r0 + lax.broadcasted_iota(jnp.int32, (tile, 1), 0)
            acc = jnp.zeros((tile, HEAD_DIM), F32)
            for j in range(CONV_K):
                dd = j - CONV_K // 2
                xs = pad_scr[CONV_PAD + r0 + dd:CONV_PAD + r0 + dd + tile, :]
                if dd > 0 and r0 < n_lat <= r0 + tile + dd and n_lat < rows:
                    xs = jnp.where((t < n_lat) & (t + dd >= n_lat), 0.0, xs)
                if dd < 0 and r0 + dd < n_lat <= r0 + tile and n_lat < rows:
                    xs = jnp.where((t >= n_lat) & (t + dd < n_lat), 0.0, xs)
                acc = acc + xs * cw[j:j + 1, :]
            y = acc * jax.nn.sigmoid(acc)
            if kind != "v":
                y = y * lax.rsqrt(jnp.sum(y * y, axis=-1, keepdims=True) + EPS)
            if kind == "q":
                y = y * (HEAD_DIM ** -0.5)
            dst[r0:r0 + tile, :] = y


def _gdn_local(groups, head0, heads, prm_ref, gates_ref, q_scr, k_scr, v_scr,
               w_scr, u_scr, qd_scr, kd_scr, qk_scr, gt_scr):
    per = GROUP // CHUNK
    ii = lax.broadcasted_iota(jnp.int32, (GROUP, GROUP), 0)
    jj = lax.broadcasted_iota(jnp.int32, (GROUP, GROUP), 1)
    eye = ii == jj
    same = (ii // CHUNK) == (jj // CHUNK)
    hds = range(heads)
    chains = [(hh, d) for hh in hds for d in (0, 1)]
    idx = range(len(chains))
    r0s = [pl.multiple_of(groups[d] * GROUP, GROUP) for hh, d in chains]
    rgs = [pl.ds(r0, GROUP) for r0 in r0s]
    kc = [k_scr[hh, rgs[ci], :] for ci, (hh, d) in enumerate(chains)]
    vc = [v_scr[hh, rgs[ci], :] for ci, (hh, d) in enumerate(chains)]
    qc = [q_scr[hh, rgs[ci], :] for ci, (hh, d) in enumerate(chains)]
    kk = [_dot_nt(kc[ci], kc[ci]) for ci in idx]
    qk = [_dot_nt(qc[ci], kc[ci]) for ci in idx]
    yield
    big_g, beta = [], []
    for hh, d in chains:
        alog = jnp.full((1, GROUP), prm_ref[d, head0 + hh], F32)
        dtb = jnp.full((1, GROUP), prm_ref[2 + d, head0 + hh], F32)
        g = -jnp.exp(alog) * jax.nn.softplus(gates_ref[d, hh, groups[d]] + dtb)
        beta.append(jax.nn.sigmoid(gates_ref[2 + d, hh, groups[d]]))
        incl = same & ((ii <= jj) if d == 0 else (ii >= jj))
        big_g.append(jnp.sum(jnp.where(incl, _col(g, eye), 0.0), axis=0, keepdims=True))
    yield
    g_col, beta_col, g_end_col, gamma, p, t_inv = [], [], [], [], [], []
    for ci, (hh, d) in enumerate(chains):
        g_col.append(_col(big_g[ci], eye))
        beta_col.append(_col(beta[ci], eye))
        end_lane = (ii // CHUNK) * CHUNK + (CHUNK - 1 if d == 0 else 0)
        g_end_col.append(jnp.sum(jnp.where(jj == end_lane, big_g[ci], 0.0), axis=-1, keepdims=True))
        after = same & ((ii >= jj) if d == 0 else (ii <= jj))
        strict = same & ((ii > jj) if d == 0 else (ii < jj))
        gamma.append(jnp.exp(jnp.where(after, g_col[ci] - big_g[ci], -jnp.inf)))
        p.append(jnp.where(strict, -(beta_col[ci] * kk[ci] * gamma[ci]), 0.0))
        t_inv.append(jnp.where(eye, 1.0, 0.0) + p[ci])
    yield
    for _ in range(int(math.log2(CHUNK)) - 1):
        p = [_dot(p[ci], p[ci]) for ci in idx]
        yield
        t_inv = [t_inv[ci] + _dot(t_inv[ci], p[ci]) for ci in idx]
        yield
    e_col = [jnp.exp(g_col[ci]) for ci in idx]
    wu = [_dot(t_inv[ci], jnp.concatenate([kc[ci] * (beta_col[ci] * e_col[ci]), vc[ci] * beta_col[ci]], axis=1))
          for ci in idx]
    yield
    for ci, (hh, d) in enumerate(chains):
        w_scr[hh, d, rgs[ci], :] = wu[ci][:, :HEAD_DIM].astype(BF16)
        u_scr[hh, d, rgs[ci], :] = wu[ci][:, HEAD_DIM:]
        qd_scr[hh, d, rgs[ci], :] = (qc[ci] * e_col[ci]).astype(BF16)
        kd_scr[hh, d, rgs[ci], :] = (kc[ci] * jnp.exp(g_end_col[ci] - g_col[ci])).astype(BF16)
        qkg = (qk[ci] * gamma[ci]).astype(BF16)
        for a in range(per):
            lo = a * CHUNK
            qk_scr[hh, d, pl.ds(r0s[ci] + lo, CHUNK), :] = qkg[lo:lo + CHUNK, lo:lo + CHUNK]
            end = lo + (CHUNK - 1 if d == 0 else 0)
            gt_scr[hh, d, groups[d] * per + a] = jnp.broadcast_to(jnp.exp(big_g[ci][:, end:end + 1]),
                                                                   (1, HEAD_DIM))
    yield


def _gdn_scan(states, groups, heads, w_scr, u_scr, qd_scr, kd_scr, qk_scr, gt_scr, oacc_scr):
    per = GROUP // CHUNK
    chains = [(hh, d) for hh in range(heads) for d in (0, 1)]
    for a in range(per):
        cidx = [groups[d] * per + (a if d == 0 else per - 1 - a) for hh, d in chains]
        rc = [pl.ds(pl.multiple_of(c * CHUNK, CHUNK), CHUNK) for c in cidx]
        sb = [st.astype(BF16) for st in states]
        ws = [jnp.dot(w_scr[hh, d, rc[ci], :], sb[ci], preferred_element_type=F32)
              for ci, (hh, d) in enumerate(chains)]
        yield
        vb = [(u_scr[hh, d, rc[ci], :] - ws[ci]).astype(BF16) for ci, (hh, d) in enumerate(chains)]
        new_states = [gt_scr[hh, d, cidx[ci]] * states[ci] + lax.dot_general(
            kd_scr[hh, d, rc[ci], :], vb[ci], TN_DIMS, preferred_element_type=F32)
            for ci, (hh, d) in enumerate(chains)]
        yield
        for ci, (hh, d) in enumerate(chains):
            oacc_scr[hh, d, rc[ci], :] = (jnp.dot(qd_scr[hh, d, rc[ci], :], sb[ci], preferred_element_type=F32)
                                          + jnp.dot(qk_scr[hh, d, rc[ci], :], vb[ci], preferred_element_type=F32))
        states[:] = new_states
        yield


def _interleave(*gens):
    live = list(gens)
    while live:
        for g in list(live):
            try:
                next(g)
            except StopIteration:
                live.remove(g)


def _gdn_kernel(prm_ref, gates_ref, cwq_ref, cwk_ref, cwv_ref, og_ref, xq_ref, xk_ref, xv_ref, z_ref,
                o_ref, pad_scr, q_scr, k_scr, v_scr, w_scr, u_scr, qd_scr, kd_scr, qk_scr, gt_scr, oacc_scr,
                *, rows, n_lat, tile, heads):
    n_chunks = rows // CHUNK
    lat_chunks = n_lat // CHUNK
    head0 = pl.program_id(1) * heads
    chains = [(hh, d) for hh in range(heads) for d in (0, 1)]
    pad_scr[0:CONV_PAD, :] = jnp.zeros((CONV_PAD, HEAD_DIM), F32)
    pad_scr[CONV_PAD + rows:, :] = jnp.zeros((CONV_PAD, HEAD_DIM), F32)

    for hh in range(heads):
        cols = slice(hh * HEAD_DIM, (hh + 1) * HEAD_DIM)
        _gdn_prepare((xq_ref, xk_ref, xv_ref), (cwq_ref, cwk_ref, cwv_ref),
                     (q_scr.at[hh], k_scr.at[hh], v_scr.at[hh]), pad_scr, cols, rows=rows, n_lat=n_lat, tile=tile)

    n_groups = rows // GROUP
    lat_groups = n_lat // GROUP

    def groups_at(t):
        return (lax.rem(t + lat_groups, n_groups), n_groups - 1 - t)

    def local(t):
        return _gdn_local(groups_at(t), head0, heads, prm_ref, gates_ref, q_scr, k_scr, v_scr,
                          w_scr, u_scr, qd_scr, kd_scr, qk_scr, gt_scr)

    def scan(states, t):
        return _gdn_scan(states, groups_at(t), heads, w_scr, u_scr, qd_scr, kd_scr, qk_scr, gt_scr, oacc_scr)

    def step(t, states):
        states = list(states)
        _interleave(local(t), scan(states, t - 1))
        return tuple(states)

    _interleave(local(jnp.int32(0)))
    zero = jnp.zeros((HEAD_DIM, HEAD_DIM), F32)
    states = list(lax.fori_loop(1, n_groups, step, (zero,) * len(chains)))
    _interleave(scan(states, jnp.int32(n_groups - 1)))

    for hh in range(heads):
        cols = slice(hh * HEAD_DIM, (hh + 1) * HEAD_DIM)
        for r0 in range(0, rows, tile):
            o = oacc_scr[hh, 0, r0:r0 + tile, :] + oacc_scr[hh, 1, r0:r0 + tile, :]
            z = z_ref[r0:r0 + tile, cols].astype(F32)
            o_ref[r0:r0 + tile, cols] = (_rms(o, og_ref[...]) * (z * jax.nn.sigmoid(z))).astype(o_ref.dtype)


def _gdn(proj, gates, prm, conv_w, o_gain, *, n_heads, col0, n_lat, tile=256, heads=2):
    bsz, rows, _ = proj.shape
    n_chunks = rows // CHUNK
    wide = heads * HEAD_DIM
    kern = functools.partial(_gdn_kernel, rows=rows, n_lat=n_lat, tile=tile, heads=heads)
    seq = lambda off: pl.BlockSpec((None, rows, wide), lambda b, h: (b, 0, (col0 + off * n_heads) // heads + h))
    cw = lambda off: pl.BlockSpec((CONV_K, wide), lambda b, h: (0, off * n_heads // heads + h))
    per_head = lambda shape, dt: pltpu.VMEM((heads, 2) + shape, dt)
    return pl.pallas_call(
        kern,
        out_shape=jax.ShapeDtypeStruct((bsz, rows, n_heads * HEAD_DIM), BF16),
        grid=(bsz, n_heads // heads),
        in_specs=[
            pl.BlockSpec(memory_space=pltpu.SMEM),
            pl.BlockSpec((None, 4, heads, rows // GROUP, 1, GROUP), lambda b, h: (b, 0, h, 0, 0, 0)),
            cw(0), cw(1), cw(2),
            pl.BlockSpec((1, HEAD_DIM), lambda b, h: (0, 0)),
            seq(0), seq(1), seq(2), seq(3),
        ],
        out_specs=pl.BlockSpec((None, rows, wide), lambda b, h: (b, 0, h)),
        scratch_shapes=[
            pltpu.VMEM((rows + 2 * CONV_PAD, HEAD_DIM), F32),
            pltpu.VMEM((heads, rows, HEAD_DIM), F32),
            pltpu.VMEM((heads, rows, HEAD_DIM), F32),
            pltpu.VMEM((heads, rows, HEAD_DIM), F32),
            per_head((rows, HEAD_DIM), BF16),
            per_head((rows, HEAD_DIM), F32),
            per_head((rows, HEAD_DIM), BF16),
            per_head((rows, HEAD_DIM), BF16),
            per_head((rows, CHUNK), BF16),
            per_head((n_chunks, 1, HEAD_DIM), F32),
            per_head((rows, HEAD_DIM), F32),
        ],
        compiler_params=_params(("parallel", "arbitrary")),
        name="gdn",
    )(prm, gates, conv_w, conv_w, conv_w, o_gain, proj, proj, proj, proj)


def _mla_kernel(qng_ref, qrg_ref, kng_ref, krg_ref, cq_ref, sq_ref, ck_ref, sk_ref,
                qn_ref, qr_ref, kn_ref, v_ref, kr_ref, o_ref, kb_scr, vb_scr, s_scr, *, scale, rows, tq):
    h = pl.program_id(1)

    @pl.when(pl.program_id(2) == 0)
    def _():
        kb_scr[:, :QK_NOPE] = _rms(kn_ref[...].astype(F32), kng_ref[...]).astype(BF16)
        kr = _rope128(_rms(kr_ref[...], krg_ref[...]), ck_ref[...], sk_ref[...])
        kb_scr[:, QK_NOPE:] = kr.astype(BF16)
        vb_scr[...] = v_ref[...].astype(BF16)

    streams = []
    sub = 2 * Q_SUB
    for r0 in range(0, tq, sub):
        sl = slice(r0, r0 + sub)
        qn = _rms(qn_ref[sl, :].astype(F32), qng_ref[...])
        qr = _rope128(_group_rms(qr_ref[sl, :].astype(F32), qrg_ref[...]), cq_ref[sl, :], sq_ref[sl, :])
        lane = lax.broadcasted_iota(jnp.int32, qr.shape, 1)
        mine = (lane // QK_ROPE) == (h % 2)
        q = jnp.concatenate([qn, jnp.where(mine, qr, 0.0)], axis=-1) * (scale * LOG2E)
        streams.append((q.astype(BF16), 0, rows))
    for i, (o, l) in enumerate(_softmax_pv(streams, kb_scr, vb_scr, s_scr)):
        o_ref[i * sub:(i + 1) * sub, :] = (o * (1.0 / l)).astype(o_ref.dtype)


def _mla_attn(q, kv, proj, qn_gain, qr_gain, kn_gain, kr_gain, cos, sin, *, n_heads, n_lat, kr_block,
              tq_cap=1024):
    bsz, rows, _ = kv.shape
    tq = _pick_q_tile(n_lat, tq_cap)
    kern = functools.partial(_mla_kernel, scale=(QK_NOPE + QK_ROPE) ** -0.5, rows=rows, tq=tq)
    vec = lambda: pl.BlockSpec((1, LANES), lambda b, h, i: (0, 0))
    return pl.pallas_call(
        kern,
        out_shape=jax.ShapeDtypeStruct((bsz, n_lat, n_heads * V_HEAD), BF16),
        grid=(bsz, n_heads, n_lat // tq),
        in_specs=[
            vec(), vec(), vec(), vec(),
            pl.BlockSpec((tq, LANES), lambda b, h, i: (i, 0)),
            pl.BlockSpec((tq, LANES), lambda b, h, i: (i, 0)),
            pl.BlockSpec((rows, LANES), lambda b, h, i: (0, 0)),
            pl.BlockSpec((rows, LANES), lambda b, h, i: (0, 0)),
            pl.BlockSpec((None, tq, QK_NOPE), lambda b, h, i: (b, i, h)),
            pl.BlockSpec((None, tq, LANES), lambda b, h, i: (b, i, n_heads + h // 2)),
            pl.BlockSpec((None, rows, QK_NOPE), lambda b, h, i: (b, 0, 2 * h)),
            pl.BlockSpec((None, rows, V_HEAD), lambda b, h, i: (b, 0, 2 * h + 1)),
            pl.BlockSpec((None, rows, LANES), lambda b, h, i: (b, 0, kr_block)),
        ],
        out_specs=pl.BlockSpec((None, tq, V_HEAD), lambda b, h, i: (b, i, h)),
        scratch_shapes=[pltpu.VMEM((rows, QK_NOPE + LANES), BF16), pltpu.VMEM((rows, V_HEAD), BF16),
                        pltpu.VMEM((2, 2 * Q_SUB, rows), F32)],
        compiler_params=_params(("parallel", "parallel", "arbitrary")),
        name="mla_attn",
    )(qn_gain, qr_gain, kn_gain, kr_gain, cos, sin, cos, sin, q, q, kv, kv, proj)


def _pad_cols(w, n):
    return jnp.pad(w, ((0, 0), (0, n - w.shape[1])))


def kernel(x, c, ctx, c_ctx, ada_w, ada_b, norm1, norm2, mlp_w1, mlp_w2, e_w_in, e_q_norm, e_k_norm, e_lam_q1, e_lam_k1, e_lam_q2, e_lam_k2, e_subln, e_conv, e_alog_f, e_alog_b, e_dtb_f, e_dtb_b, e_o_norm, e_w_out, o_w_in, o_qa_norm, o_w_uq, o_kva_norm, o_w_ukv, o_qn_norm, o_qr_norm, o_kn_norm, o_kr_norm, o_w_out):
    bsz, n_lat, d = x.shape
    n_ctx = ctx.shape[1]
    rows = n_lat + n_ctx
    depth = ada_w.shape[0]
    assert depth == 2 and bsz < 8
    mix_heads = d // HEAD_DIM
    a_heads = mix_heads // 2
    b_heads = mix_heads - a_heads
    c_heads = d // V_HEAD
    dup = lambda g: jnp.tile(g.reshape(1, -1), (1, 2))

    svec = jnp.concatenate([c, c_ctx[None, :], jnp.zeros((8 - bsz - 1, d), F32)], axis=0)
    mod = _adaln_mod(svec, ada_w, ada_b).reshape(depth, 8, 6, d)

    cos, sin = _rope_tables(n_lat, n_ctx)
    xs = jnp.concatenate([x, ctx], axis=1)

    a_qk = a_heads * HEAD_DIM
    main = 7 * a_qk
    n_gate = 4 * b_heads
    w_gate = _pad_cols(e_w_in[0][:, main:], LANES)
    proj, gates = _norm_matmul(xs, 0, d, norm1[0], e_w_in.astype(BF16), n=main, mod=mod[0], which=0,
                               n_lat=n_lat, w_side=w_gate.astype(BF16), out_dtype=BF16, name="in_proj0")
    lam_p = jnp.concatenate([e_lam_q1, e_lam_k1, e_lam_q2, e_lam_k2], axis=0)
    o_a, w1_0 = _diff_attn(proj, lam_p, dup(e_q_norm[0]), dup(e_k_norm[0]), e_subln[0].reshape(1, -1), cos, sin,
                           (mlp_w1, 0), n_heads=a_heads, n_lat=n_lat, lam_init=0.8 - 0.6 * math.exp(-0.3 * 0))
    gates = gates[:, :, :n_gate].reshape(bsz, rows // GROUP, GROUP, 4, b_heads)
    gates = gates.transpose(0, 3, 4, 1, 2).reshape(bsz, 4, b_heads, rows // GROUP, 1, GROUP)
    prm = jnp.concatenate([e_alog_f, e_alog_b, e_dtb_f, e_dtb_b], axis=0)
    o_b = _gdn(proj, gates, prm, e_conv[0], e_o_norm[0].reshape(1, -1), n_heads=b_heads,
               col0=3 * a_heads, n_lat=n_lat)
    xs = _out_proj(o_a, 0, o_b, 0, e_w_out.astype(BF16), xs, mod[0], gate_row=2, n_lat=n_lat, rows=rows,
                   name="out_proj0")
    d_ff = mlp_w2.shape[1]
    hid, w2 = _norm_matmul(xs, 0, d, norm2[0], w1_0, mod=mod[0], which=1, n_lat=n_lat, relu2=True,
                           cast=(mlp_w2.reshape(depth * d_ff, d), None), out_dtype=BF16, name="mlp_up0")
    w2 = w2.reshape(depth, d_ff, d)
    xs, w1_1 = _down_proj(hid, w2, xs, mod[0], layer=0, cast=(mlp_w1, 1), gate_row=5, n_lat=n_lat,
                          name="mlp_down0")

    w_in1 = jnp.concatenate([o_w_in[0], o_w_in[0][:, Q_LORA + KV_LORA:]], axis=1)
    proj = _norm_matmul(xs, 0, d, norm1[1], w_in1.astype(BF16), mod=mod[1], which=0, n_lat=n_lat,
                        tm_cap=768, tn_cap=w_in1.shape[1], name="in_proj1")
    w_uq = o_w_uq[0].reshape(Q_LORA, c_heads, QK_NOPE + QK_ROPE)
    w_uq = jnp.concatenate([w_uq[:, :, :QK_NOPE].reshape(Q_LORA, -1), w_uq[:, :, QK_NOPE:].reshape(Q_LORA, -1)],
                           axis=1)
    q = _norm_matmul(proj, 0, Q_LORA, o_qa_norm[0], w_uq.astype(BF16), tn_cap=1024, out_dtype=BF16,
                     tm_cap=768, name="q_up")
    kv = _norm_matmul(proj, 1, KV_LORA, o_kva_norm[0], o_w_ukv.astype(BF16), tn_cap=1024, out_dtype=BF16,
                      tm_cap=768, name="kv_up")
    o_c = _mla_attn(q, kv, proj, o_qn_norm[0].reshape(1, -1), dup(o_qr_norm[0]), o_kn_norm[0].reshape(1, -1),
                    dup(o_kr_norm[0]), cos, sin, n_heads=c_heads, n_lat=n_lat,
                    kr_block=(Q_LORA + KV_LORA) // LANES)
    xl = _out_proj(o_c, 0, o_c, 1, o_w_out.astype(BF16), xs, mod[1], gate_row=2, n_lat=n_lat, rows=n_lat,
                   tm_cap=1024, name="out_proj1")
    hid = _norm_matmul(xl, 0, d, norm2[1], w1_1, mod=mod[1], which=1, n_lat=n_lat,
                       relu2=True, out_dtype=BF16, tm_cap=1024, name="mlp_up1")
    return _down_proj(hid, w2, xl, mod[1], layer=1, gate_row=5, n_lat=n_lat, tm_cap=1024, name="mlp_down1")
```

```python
import functools
import math

import numpy as np
import jax
import jax.numpy as jnp
from jax import lax
from jax.experimental import pallas as pl
from jax.experimental.pallas import tpu as pltpu

EPS = 1e-6
ROPE_BASE = 10000.0
GRID_W = 64
HEAD_DIM = 128
A_SUB = 64
CHUNK = 64
CONV_K = 5
QK_NOPE = 128
QK_ROPE = 64
V_HEAD = 128
Q_LORA = 512
KV_LORA = 512
LANES = 128
CONV_PAD = 8
KEY_TILE = 256
Q_SUB = 128
GROUP = 256
NORM_ELEMS = 64 * 1024
NORM_SUB = 128
LOG2E = math.log2(math.e)
VMEM_LIMIT = 56 * 1024 * 1024

F32 = jnp.float32
BF16 = jnp.bfloat16
NT_DIMS = (((1,), (1,)), ((), ()))
TN_DIMS = (((0,), (0,)), ((), ()))


def _params(sem):
    return pltpu.CompilerParams(dimension_semantics=sem, vmem_limit_bytes=VMEM_LIMIT)


def _dot(a, b):
    return jnp.dot(a.astype(BF16), b.astype(BF16), preferred_element_type=F32)


def _dot_nt(a, b):
    return lax.dot_general(a.astype(BF16), b.astype(BF16), NT_DIMS, preferred_element_type=F32)


def _dot_tn(a, b):
    return lax.dot_general(a.astype(BF16), b.astype(BF16), TN_DIMS, preferred_element_type=F32)


def _rms(x, gain):
    return x * lax.rsqrt(jnp.mean(x * x, axis=-1, keepdims=True) + EPS) * gain


def _pick_tile(n, cap):
    t = min(n, cap)
    while n % t or t % 8:
        t -= 8
    return t


def _mod_kernel(s_ref, w_ref, b_ref, o_ref):
    s = s_ref[...]
    s = s * jax.nn.sigmoid(s)
    o_ref[...] = _dot(s, w_ref[...]) + b_ref[...]


def _adaln_mod(svec, ada_w, ada_b):
    depth, d, n = ada_w.shape
    tn = _pick_tile(n, 1024)
    return pl.pallas_call(
        _mod_kernel,
        out_shape=jax.ShapeDtypeStruct((depth, 8, n), F32),
        grid=(depth, n // tn),
        in_specs=[
            pl.BlockSpec((8, d), lambda l, j: (0, 0)),
            pl.BlockSpec((None, d, tn), lambda l, j: (l, 0, j)),
            pl.BlockSpec((None, 1, tn), lambda l, j: (l, 0, j)),
        ],
        out_specs=pl.BlockSpec((None, 8, tn), lambda l, j: (l, 0, j)),
        compiler_params=_params(("parallel", "parallel")),
        name="adaln_mod",
    )(svec, ada_w, ada_b.reshape(depth, 1, n))


def _norm_matmul_kernel(*refs, tm, n_lat, n_tiles, shift_row, scale_row, modulate, relu2, side, cast, tail):
    refs = list(refs)
    h_scr = refs.pop()
    if cast:
        cast_out = refs.pop()
    if side:
        os_ref = refs.pop()
    o_ref = refs.pop()
    if cast:
        cast_out[...] = refs.pop()[...].astype(BF16)
    if side:
        ws_ref = refs.pop()
    w_ref = refs.pop()
    x_ref, g_ref = refs[:2]
    mb_ref, mc_ref = refs[2:4] if modulate else (None, None)
    tail_ref = refs[4] if tail else None

    def project(rows):
        acc = jnp.dot(h_scr[rows, :], w_ref[...], preferred_element_type=F32)
        if relu2:
            acc = jnp.square(jnp.maximum(acc, 0.0))
        o_ref[rows, :] = acc.astype(o_ref.dtype)

    def first_step(tile):
        sub = math.gcd(tm, NORM_SUB)
        chunk = math.gcd(sub, NORM_ELEMS // x_ref.shape[1])
        for s0 in range(0, tm, sub):
            for r0 in range(s0, s0 + sub, chunk):
                if tile is None:
                    is_ctx = pl.program_id(1) * tm + r0 >= n_lat
                    src = x_ref[r0:r0 + chunk, :]
                else:
                    is_ctx = tile * tm + r0 >= n_lat
                    t0 = tile * tm + r0 - n_lat
                    src = tail_ref[t0:t0 + chunk, :] if is_ctx else x_ref[r0:r0 + chunk, :]
                y = _rms(src, g_ref[...])
                if modulate:
                    pick = lambda row: (jnp.where(is_ctx, mc_ref[row:row + 1, :], mb_ref[row:row + 1, :])
                                        if tile is None else (mc_ref if is_ctx else mb_ref)[row:row + 1, :])
                    y = y * (1.0 + pick(scale_row)) + pick(shift_row)
                h_scr[r0:r0 + chunk, :] = y.astype(BF16)
            rows = slice(s0, s0 + sub)
            project(rows)
            if side:
                os_ref[rows, :] = jnp.dot(h_scr[rows, :], ws_ref[...], preferred_element_type=F32)

    first = pl.program_id(2) == 0
    if tail:
        lat_tiles = n_lat // tm
        if lat_tiles:
            pl.when(first & (pl.program_id(1) < lat_tiles))(lambda: first_step(0))
        for t in range(lat_tiles, n_tiles):
            pl.when(first & (pl.program_id(1) == t))(functools.partial(first_step, t))
    else:
        pl.when(first)(lambda: first_step(None))

    @pl.when(pl.program_id(2) != 0)
    def _():
        project(slice(None))


def _weight_spec(w, layer, block, index_map):
    if w.ndim == 2:
        return pl.BlockSpec(block, index_map)
    return pl.BlockSpec((None,) + block, lambda *g: (layer,) + index_map(*g))


def _cast_rider(src, layer, grid):
    n_steps = math.prod(grid)
    rows, cols = src.shape[-2:]
    blk = rows // n_steps
    assert rows % n_steps == 0 and blk % 16 == 0, (src.shape, grid)

    def lin(*g):
        idx = g[0]
        for size, gi in zip(grid[1:], g[1:]):
            idx = idx * size + gi
        return idx

    if layer is None:
        in_spec = pl.BlockSpec((blk, cols), lambda *g: (lin(*g), 0))
    else:
        in_spec = pl.BlockSpec((None, blk, cols), lambda *g: (layer, lin(*g), 0))
    out_spec = pl.BlockSpec((blk, cols), lambda *g: (lin(*g), 0))
    return in_spec, out_spec, jax.ShapeDtypeStruct((rows, cols), BF16)


def _norm_matmul(x, col_block, k, gain, w, *, layer=0, n=None, mod=None, which=0, n_lat=None, relu2=False,
                 w_side=None, cast=None, x_tail=None, out_dtype=F32, tm_cap=1152, tn_cap=1024,
                 name="norm_matmul"):
    bsz, rows, _ = x.shape
    if x_tail is not None:
        rows += x_tail.shape[1]
    if n is None:
        n = w.shape[-1]
    tm = _pick_tile(rows, tm_cap)
    tn = _pick_tile(n, tn_cap)
    modulate = mod is not None
    side = w_side is not None
    if n_lat is None:
        n_lat = rows
    in_specs = [
        pl.BlockSpec((None, tm, k), lambda b, i, j: (b, i, col_block)),
        pl.BlockSpec((1, k), lambda b, i, j: (0, 0)),
    ]
    args = [x, gain.reshape(1, k)]
    if modulate:
        in_specs += [
            pl.BlockSpec((None, 6, k), lambda b, i, j: (b, 0, 0)),
            pl.BlockSpec((None, 6, k), lambda b, i, j: (bsz, 0, 0)),
        ]
        args += [mod, mod]
    if x_tail is not None:
        in_specs.append(pl.BlockSpec((None, x_tail.shape[1], k), lambda b, i, j: (b, 0, col_block)))
        args.append(x_tail)
    in_specs.append(_weight_spec(w, layer, (k, tn), lambda b, i, j: (0, j)))
    args.append(w)
    out_shape = [jax.ShapeDtypeStruct((bsz, rows, n), out_dtype)]
    out_specs = [pl.BlockSpec((None, tm, tn), lambda b, i, j: (b, i, j))]
    if side:
        in_specs.append(pl.BlockSpec((k, LANES), lambda b, i, j: (0, 0)))
        args.append(w_side)
        out_shape.append(jax.ShapeDtypeStruct((bsz, rows, LANES), F32))
        out_specs.append(pl.BlockSpec((None, tm, LANES), lambda b, i, j: (b, i, 0)))
    grid = (bsz, rows // tm, n // tn)
    if cast is not None:
        c_in, c_out, c_shape = _cast_rider(cast[0], cast[1], grid)
        in_specs.append(c_in)
        args.append(cast[0])
        out_shape.append(c_shape)
        out_specs.append(c_out)
    kern = functools.partial(_norm_matmul_kernel, tm=tm, n_lat=n_lat, n_tiles=rows // tm, shift_row=3 * which,
                             scale_row=3 * which + 1, modulate=modulate, relu2=relu2, side=side,
                             cast=cast is not None, tail=x_tail is not None)
    out = pl.pallas_call(
        kern,
        out_shape=out_shape,
        grid=grid,
        in_specs=in_specs,
        out_specs=out_specs,
        scratch_shapes=[pltpu.VMEM((tm, k), BF16)],
        compiler_params=_params(("parallel", "parallel", "arbitrary")),
        name=name,
    )(*args)
    return out if len(out) > 1 else out[0]


def _gate(mb_ref, mc_ref, gate_row, i, tm, n_lat):
    t = i * tm + lax.broadcasted_iota(jnp.int32, (tm, 1), 0)
    return jnp.where(t >= n_lat, mc_ref[gate_row:gate_row + 1, :], mb_ref[gate_row:gate_row + 1, :])


def _out_proj_kernel(a1_ref, a2_ref, w_ref, r_ref, mb_ref, mc_ref, *rest, tm, n_lat, n_tiles, k1, gate_row):
    acc = jnp.dot(a1_ref[...], w_ref[:k1, :], preferred_element_type=F32)
    acc += jnp.dot(a2_ref[...], w_ref[k1:, :], preferred_element_type=F32)
    if len(rest) == 1:
        o_ref, = rest
        g = _gate(mb_ref, mc_ref, gate_row, pl.program_id(1), tm, n_lat)
        o_ref[...] = r_ref[...] + g * acc
        return
    tail_ref, o_ref = rest
    gate_b = mb_ref[gate_row:gate_row + 1, :]
    gate_c = mc_ref[gate_row:gate_row + 1, :]

    def finish(tile):
        n_own = min(max(n_lat - tile * tm, 0), tm)
        if n_own:
            o_ref[:n_own, :] = r_ref[:n_own, :] + gate_b * acc[:n_own]
        if n_own < tm:
            t0 = tile * tm + n_own - n_lat
            o_ref[n_own:, :] = tail_ref[t0:t0 + tm - n_own, :] + gate_c * acc[n_own:]

    lat_tiles = n_lat // tm
    if lat_tiles:
        pl.when(pl.program_id(1) < lat_tiles)(lambda: finish(0))
    for t in range(lat_tiles, n_tiles):
        pl.when(pl.program_id(1) == t)(functools.partial(finish, t))


def _out_proj(a1, c1, a2, c2, w, res, mod, *, gate_row, n_lat, rows, layer=0, res_tail=None, tm_cap=1152,
              tn_cap=1024, name="out_proj"):
    bsz = res.shape[0]
    d = w.shape[-1]
    k1 = w.shape[-2] // 2
    tm = _pick_tile(rows, tm_cap)
    tn = _pick_tile(d, tn_cap)
    kern = functools.partial(_out_proj_kernel, tm=tm, n_lat=n_lat, n_tiles=rows // tm, k1=k1, gate_row=gate_row)
    in_specs = [
        pl.BlockSpec((None, tm, k1), lambda b, i, j: (b, i, c1)),
        pl.BlockSpec((None, tm, k1), lambda b, i, j: (b, i, c2)),
        _weight_spec(w, layer, (2 * k1, tn), lambda b, i, j: (0, j)),
        pl.BlockSpec((None, tm, tn), lambda b, i, j: (b, i, j)),
        pl.BlockSpec((None, 6, tn), lambda b, i, j: (b, 0, j)),
        pl.BlockSpec((None, 6, tn), lambda b, i, j: (bsz, 0, j)),
    ]
    args = [a1, a2, w, res, mod, mod]
    if res_tail is not None:
        in_specs.append(pl.BlockSpec((None, res_tail.shape[1], tn), lambda b, i, j: (b, 0, j)))
        args.append(res_tail)
    return pl.pallas_call(
        kern,
        out_shape=jax.ShapeDtypeStruct((bsz, rows, d), F32),
        grid=(bsz, rows // tm, d // tn),
        in_specs=in_specs,
        out_specs=pl.BlockSpec((None, tm, tn), lambda b, i, j: (b, i, j)),
        compiler_params=_params(("parallel", "parallel", "arbitrary")),
        name=name,
    )(*args)


def _down_proj_kernel(a_ref, w_ref, r_ref, mb_ref, mc_ref, *rest, tm, n_lat, gate_row, cast):
    if cast:
        cast_in, o_ref, cast_out, acc_scr = rest
        cast_out[...] = cast_in[...].astype(BF16)
    else:
        o_ref, acc_scr = rest
    kk = pl.program_id(3)

    @pl.when(kk == 0)
    def _():
        acc_scr[...] = jnp.zeros_like(acc_scr)

    acc_scr[...] += jnp.dot(a_ref[...], w_ref[...], preferred_element_type=F32)

    @pl.when(kk == pl.num_programs(3) - 1)
    def _():
        g = _gate(mb_ref, mc_ref, gate_row, pl.program_id(1), tm, n_lat)
        o_ref[...] = r_ref[...] + g * acc_scr[...]


def _down_proj(a, w, res, mod, *, gate_row, n_lat, layer=0, cast=None, tm_cap=1152, tn_cap=512, tk=4096,
               name="down_proj"):
    bsz, rows, kdim = a.shape
    d = w.shape[-1]
    tm = _pick_tile(rows, tm_cap)
    tn = _pick_tile(d, tn_cap)
    grid = (bsz, rows // tm, d // tn, kdim // tk)
    in_specs = [
        pl.BlockSpec((None, tm, tk), lambda b, i, j, k: (b, i, k)),
        _weight_spec(w, layer, (tk, tn), lambda b, i, j, k: (k, j)),
        pl.BlockSpec((None, tm, tn), lambda b, i, j, k: (b, i, j)),
        pl.BlockSpec((None, 6, tn), lambda b, i, j, k: (b, 0, j)),
        pl.BlockSpec((None, 6, tn), lambda b, i, j, k: (bsz, 0, j)),
    ]
    args = [a, w, res, mod, mod]
    out_shape = [jax.ShapeDtypeStruct((bsz, rows, d), F32)]
    out_specs = [pl.BlockSpec((None, tm, tn), lambda b, i, j, k: (b, i, j))]
    if cast is not None:
        c_in, c_out, c_shape = _cast_rider(cast[0], cast[1], grid)
        in_specs.append(c_in)
        args.append(cast[0])
        out_shape.append(c_shape)
        out_specs.append(c_out)
    kern = functools.partial(_down_proj_kernel, tm=tm, n_lat=n_lat, gate_row=gate_row, cast=cast is not None)
    out = pl.pallas_call(
        kern,
        out_shape=out_shape,
        grid=grid,
        in_specs=in_specs,
        out_specs=out_specs,
        scratch_shapes=[pltpu.VMEM((tm, tn), F32)],
        compiler_params=_params(("parallel", "parallel", "parallel", "arbitrary")),
        name=name,
    )(*args)
    return out if len(out) > 1 else out[0]


def _rope_tables(n_lat, n_ctx):
    t = np.arange(n_lat)
    row = (t // GRID_W).astype(np.float64)
    col = (t % GRID_W).astype(np.float64)
    n_freq = A_SUB // 4
    inv_freq = ROPE_BASE ** (-np.arange(n_freq, dtype=np.float64) / n_freq)
    ang = np.concatenate([row[:, None] * inv_freq, col[:, None] * inv_freq], axis=-1)
    cos = np.concatenate([np.cos(ang), np.ones((n_ctx, A_SUB // 2))], axis=0)
    sin = np.concatenate([np.sin(ang), np.zeros((n_ctx, A_SUB // 2))], axis=0)
    cos128 = np.tile(cos, (1, 4))
    sin128 = np.tile(np.concatenate([-sin, sin], axis=1), (1, 2))
    return jnp.asarray(cos128, F32), jnp.asarray(sin128, F32)


def _rope128(x, cos, sin):
    lane = lax.broadcasted_iota(jnp.int32, x.shape, 1)
    first = (lane % A_SUB) < (A_SUB // 2)
    partner = jnp.where(first, pltpu.roll(x, LANES - A_SUB // 2, 1), pltpu.roll(x, A_SUB // 2, 1))
    return x * cos + partner * sin


def _group_rms(x, gain):
    lane = lax.broadcasted_iota(jnp.int32, x.shape, 1)
    lo = lane < A_SUB
    xx = x * x
    s_lo = jnp.sum(jnp.where(lo, xx, 0.0), axis=-1, keepdims=True)
    s_hi = jnp.sum(jnp.where(lo, 0.0, xx), axis=-1, keepdims=True)
    inv = jnp.where(lo, lax.rsqrt(s_lo / A_SUB + EPS), lax.rsqrt(s_hi / A_SUB + EPS))
    return x * inv * gain


def _pick_q_tile(n, cap):
    return max(t for t in range(Q_SUB, cap + 1, Q_SUB) if n % t == 0)


def _softmax_pv(streams, kb_scr, vb_scr, s_scr):
    n = len(streams)
    n_rows = streams[0][0].shape[0]
    dv = vb_scr.shape[1]
    outs = []
    m_prev = None
    for i in range(n + 1):
        score_spans = list(range(streams[i][1], streams[i][2], KEY_TILE)) if i < n else []
        value_spans = list(range(streams[i - 1][1], streams[i - 1][2], KEY_TILE)) if i > 0 else []
        m_acc = jnp.full((n_rows, LANES), -jnp.inf, F32)
        l_acc = jnp.zeros((n_rows, LANES), F32)
        o = jnp.zeros((n_rows, dv), F32)
        for j in range(max(len(score_spans), len(value_spans))):
            if j < len(score_spans):
                ks = score_spans[j]
                s = lax.dot_general(streams[i][0], kb_scr[ks:ks + KEY_TILE, :], NT_DIMS,
                                    preferred_element_type=F32)
                s_scr[i % 2, :, j * KEY_TILE:(j + 1) * KEY_TILE] = s
                for t in range(KEY_TILE // LANES):
                    m_acc = jnp.maximum(m_acc, s[:, t * LANES:(t + 1) * LANES])
            if j < len(value_spans):
                ks = value_spans[j]
                e = jnp.exp2(s_scr[(i - 1) % 2, :, j * KEY_TILE:(j + 1) * KEY_TILE] - m_prev)
                for t in range(KEY_TILE // LANES):
                    l_acc = l_acc + e[:, t * LANES:(t + 1) * LANES]
                o = o + jnp.dot(e.astype(BF16), vb_scr[ks:ks + KEY_TILE, :], preferred_element_type=F32)
        if i > 0:
            outs.append((o, jnp.sum(l_acc, axis=-1, keepdims=True)))
        if i < n:
            m_prev = jnp.max(m_acc, axis=-1, keepdims=True)
    return outs


def _diff_attn_kernel(lam_ref, qg_ref, kg_ref, sg_ref, cq_ref, sq_ref, ck_ref, sk_ref,
                      q_ref, k_ref, v_ref, cast_in, o_ref, cast_out, kb_scr, vb_scr, s_scr,
                      *, rows, n_lat, lam_init, tq):
    qi = pl.program_id(2)
    cast_out[...] = cast_in[...].astype(BF16)

    @pl.when(qi == 0)
    def _():
        k = _rope128(_group_rms(k_ref[...].astype(F32), kg_ref[...]), ck_ref[...], sk_ref[...])
        kb_scr[...] = k.astype(BF16)
        vb_scr[...] = v_ref[...].astype(BF16)

    lp = lam_ref[...]
    lam = (jnp.exp(jnp.sum(lp[0:1] * lp[1:2], axis=-1, keepdims=True))
           - jnp.exp(jnp.sum(lp[2:3] * lp[3:4], axis=-1, keepdims=True)) + lam_init)

    def attend(first_key):
        streams = []
        for i, k0 in enumerate(first_key):
            sl = slice(i * Q_SUB, (i + 1) * Q_SUB)
            q = _rope128(_group_rms(q_ref[sl, :].astype(F32), qg_ref[...]), cq_ref[sl, :], sq_ref[sl, :])
            q = q * (A_SUB ** -0.5 * LOG2E)
            lane = lax.broadcasted_iota(jnp.int32, q.shape, 1)
            q12 = jnp.concatenate([jnp.where(lane < A_SUB, q, 0.0), jnp.where(lane < A_SUB, 0.0, q)], axis=0)
            streams.append((q12.astype(BF16), k0, rows))
        outs = _softmax_pv(streams, kb_scr, vb_scr, s_scr)
        for i, (o12, l12) in enumerate(outs):
            o = o12[:Q_SUB] * (1.0 / l12[:Q_SUB]) - o12[Q_SUB:] * (lam / l12[Q_SUB:])
            o_ref[i * Q_SUB:(i + 1) * Q_SUB, :] = (_rms(o, sg_ref[...]) * (1.0 - lam_init)).astype(o_ref.dtype)

    patterns = {}
    for t in range(rows // tq):
        pat = tuple(n_lat if t * tq + r0 >= n_lat else 0 for r0 in range(0, tq, Q_SUB))
        patterns.setdefault(pat, []).append(t)
    for pat, tiles in patterns.items():
        @pl.when((qi >= tiles[0]) & (qi <= tiles[-1]))
        def _(pat=pat):
            attend(pat)


def _diff_attn(proj, lam_p, q_gain, k_gain, sub_gain, cos, sin, cast, *, n_heads, n_lat, lam_init, tq_cap=1152):
    bsz, rows, _ = proj.shape
    tq = _pick_q_tile(rows, tq_cap)
    kern = functools.partial(_diff_attn_kernel, rows=rows, n_lat=n_lat, lam_init=lam_init, tq=tq)
    vec = lambda: pl.BlockSpec((1, LANES), lambda b, h, i: (0, 0))
    grid = (bsz, n_heads, rows // tq)
    c_in, c_out, c_shape = _cast_rider(cast[0], cast[1], grid)
    return pl.pallas_call(
        kern,
        out_shape=[jax.ShapeDtypeStruct((bsz, rows, n_heads * HEAD_DIM), BF16), c_shape],
        grid=grid,
        in_specs=[
            pl.BlockSpec((4, A_SUB), lambda b, h, i: (0, 0)),
            vec(), vec(), vec(),
            pl.BlockSpec((tq, LANES), lambda b, h, i: (i, 0)),
            pl.BlockSpec((tq, LANES), lambda b, h, i: (i, 0)),
            pl.BlockSpec((rows, LANES), lambda b, h, i: (0, 0)),
            pl.BlockSpec((rows, LANES), lambda b, h, i: (0, 0)),
            pl.BlockSpec((None, tq, HEAD_DIM), lambda b, h, i: (b, i, h)),
            pl.BlockSpec((None, rows, HEAD_DIM), lambda b, h, i: (b, 0, n_heads + h)),
            pl.BlockSpec((None, rows, HEAD_DIM), lambda b, h, i: (b, 0, 2 * n_heads + h)),
            c_in,
        ],
        out_specs=[pl.BlockSpec((None, tq, HEAD_DIM), lambda b, h, i: (b, i, h)), c_out],
        scratch_shapes=[pltpu.VMEM((rows, HEAD_DIM), BF16), pltpu.VMEM((rows, HEAD_DIM), BF16),
                        pltpu.VMEM((2, 2 * Q_SUB, rows), F32)],
        compiler_params=_params(("parallel", "parallel", "arbitrary")),
        name="diff_attn",
    )(lam_p, q_gain, k_gain, sub_gain, cos, sin, cos, sin, proj, proj, proj, cast[0])


def _col(row_vec, eye):
    return jnp.sum(jnp.where(eye, row_vec, 0.0), axis=-1, keepdims=True)


def _gdn_prepare(x_refs, cw_refs, dsts, pad_scr, cols, *, rows, n_lat, tile):
    n_tiles = rows // tile
    for x_ref, cw_ref, dst, kind in zip(x_refs, cw_refs, dsts, "qkv"):
        pad_scr[CONV_PAD:CONV_PAD + rows, :] = x_ref[:, cols].astype(F32)
        cw = cw_ref[:, cols]
        for r in range(n_tiles):
            r0 = r * tile
            t = r0 + lax.broadcasted_iota(jnp.int32, (tile, 1), 0)
            acc = jnp.zeros((tile, HEAD_DIM), F32)
            for j in range(CONV_K):
                dd = j - CONV_K // 2
                xs = pad_scr[CONV_PAD + r0 + dd:CONV_PAD + r0 + dd + tile, :]
                if dd > 0 and r0 < n_lat <= r0 + tile + dd and n_lat < rows:
                    xs = jnp.where((t < n_lat) & (t + dd >= n_lat), 0.0, xs)
                if dd < 0 and r0 + dd < n_lat <= r0 + tile and n_lat < rows:
                    xs = jnp.where((t >= n_lat) & (t + dd < n_lat), 0.0, xs)
                acc = acc + xs * cw[j:j + 1, :]
            y = acc * jax.nn.sigmoid(acc)
            if kind != "v":
                y = y * lax.rsqrt(jnp.sum(y * y, axis=-1, keepdims=True) + EPS)
            if kind == "q":
                y = y * (HEAD_DIM ** -0.5)
            dst[r0:r0 + tile, :] = y


def _gdn_local(groups, head0, heads, prm_ref, gates_ref, q_scr, k_scr, v_scr,
               w_scr, u_scr, qd_scr, kd_scr, qk_scr, gt_scr):
    per = GROUP // CHUNK
    ii = lax.broadcasted_iota(jnp.int32, (GROUP, GROUP), 0)
    jj = lax.broadcasted_iota(jnp.int32, (GROUP, GROUP), 1)
    eye = ii == jj
    same = (ii // CHUNK) == (jj // CHUNK)
    hds = range(heads)
    chains = [(hh, d) for hh in hds for d in (0, 1)]
    idx = range(len(chains))
    r0s = [pl.multiple_of(groups[d] * GROUP, GROUP) for hh, d in chains]
    rgs = [pl.ds(r0, GROUP) for r0 in r0s]
    kc = [k_scr[hh, rgs[ci], :] for ci, (hh, d) in enumerate(chains)]
    vc = [v_scr[hh, rgs[ci], :] for ci, (hh, d) in enumerate(chains)]
    qc = [q_scr[hh, rgs[ci], :] for ci, (hh, d) in enumerate(chains)]
    kk = [_dot_nt(kc[ci], kc[ci]) for ci in idx]
    qk = [_dot_nt(qc[ci], kc[ci]) for ci in idx]
    yield
    big_g, beta = [], []
    for hh, d in chains:
        alog = jnp.full((1, GROUP), prm_ref[d, head0 + hh], F32)
        dtb = jnp.full((1, GROUP), prm_ref[2 + d, head0 + hh], F32)
        g = -jnp.exp(alog) * jax.nn.softplus(gates_ref[d, hh, groups[d]] + dtb)
        beta.append(jax.nn.sigmoid(gates_ref[2 + d, hh, groups[d]]))
        incl = same & ((ii <= jj) if d == 0 else (ii >= jj))
        big_g.append(jnp.sum(jnp.where(incl, _col(g, eye), 0.0), axis=0, keepdims=True))
    yield
    g_col, beta_col, g_end_col, gamma, p, t_inv = [], [], [], [], [], []
    for ci, (hh, d) in enumerate(chains):
        g_col.append(_col(big_g[ci], eye))
        beta_col.append(_col(beta[ci], eye))
        end_lane = (ii // CHUNK) * CHUNK + (CHUNK - 1 if d == 0 else 0)
        g_end_col.append(jnp.sum(jnp.where(jj == end_lane, big_g[ci], 0.0), axis=-1, keepdims=True))
        after = same & ((ii >= jj) if d == 0 else (ii <= jj))
        strict = same & ((ii > jj) if d == 0 else (ii < jj))
        gamma.append(jnp.exp(jnp.where(after, g_col[ci] - big_g[ci], -jnp.inf)))
        p.append(jnp.where(strict, -(beta_col[ci] * kk[ci] * gamma[ci]), 0.0))
        t_inv.append(jnp.where(eye, 1.0, 0.0) + p[ci])
    yield
    for _ in range(int(math.log2(CHUNK)) - 1):
        p = [_dot(p[ci], p[ci]) for ci in idx]
        yield
        t_inv = [t_inv[ci] + _dot(t_inv[ci], p[ci]) for ci in idx]
        yield
    e_col = [jnp.exp(g_col[ci]) for ci in idx]
    wu = [_dot(t_inv[ci], jnp.concatenate([kc[ci] * (beta_col[ci] * e_col[ci]), vc[ci] * beta_col[ci]], axis=1))
          for ci in idx]
    yield
    for ci, (hh, d) in enumerate(chains):
        w_scr[hh, d, rgs[ci], :] = wu[ci][:, :HEAD_DIM].astype(BF16)
        u_scr[hh, d, rgs[ci], :] = wu[ci][:, HEAD_DIM:]
        qd_scr[hh, d, rgs[ci], :] = (qc[ci] * e_col[ci]).astype(BF16)
        kd_scr[hh, d, rgs[ci], :] = (kc[ci] * jnp.exp(g_end_col[ci] - g_col[ci])).astype(BF16)
        qkg = (qk[ci] * gamma[ci]).astype(BF16)
        for a in range(per):
            lo = a * CHUNK
            qk_scr[hh, d, pl.ds(r0s[ci] + lo, CHUNK), :] = qkg[lo:lo + CHUNK, lo:lo + CHUNK]
            end = lo + (CHUNK - 1 if d == 0 else 0)
            gt_scr[hh, d, groups[d] * per + a] = jnp.broadcast_to(jnp.exp(big_g[ci][:, end:end + 1]),
                                                                   (1, HEAD_DIM))
    yield


def _gdn_scan(states, groups, heads, w_scr, u_scr, qd_scr, kd_scr, qk_scr, gt_scr, oacc_scr):
    per = GROUP // CHUNK
    chains = [(hh, d) for hh in range(heads) for d in (0, 1)]
    for a in range(per):
        cidx = [groups[d] * per + (a if d == 0 else per - 1 - a) for hh, d in chains]
        rc = [pl.ds(pl.multiple_of(c * CHUNK, CHUNK), CHUNK) for c in cidx]
        sb = [st.astype(BF16) for st in states]
        ws = [jnp.dot(w_scr[hh, d, rc[ci], :], sb[ci], preferred_element_type=F32)
              for ci, (hh, d) in enumerate(chains)]
        yield
        vb = [(u_scr[hh, d, rc[ci], :] - ws[ci]).astype(BF16) for ci, (hh, d) in enumerate(chains)]
        new_states = [gt_scr[hh, d, cidx[ci]] * states[ci] + lax.dot_general(
            kd_scr[hh, d, rc[ci], :], vb[ci], TN_DIMS, preferred_element_type=F32)
            for ci, (hh, d) in enumerate(chains)]
        yield
        for ci, (hh, d) in enumerate(chains):
            oacc_scr[hh, d, rc[ci], :] = (jnp.dot(qd_scr[hh, d, rc[ci], :], sb[ci], preferred_element_type=F32)
                                          + jnp.dot(qk_scr[hh, d, rc[ci], :], vb[ci], preferred_element_type=F32))
        states[:] = new_states
        yield


def _interleave(*gens):
    live = list(gens)
    while live:
        for g in list(live):
            try:
                next(g)
            except StopIteration:
                live.remove(g)


def _gdn_kernel(prm_ref, gates_ref, cwq_ref, cwk_ref, cwv_ref, og_ref, xq_ref, xk_ref, xv_ref, z_ref,
                o_ref, pad_scr, q_scr, k_scr, v_scr, w_scr, u_scr, qd_scr, kd_scr, qk_scr, gt_scr, oacc_scr,
                *, rows, n_lat, tile, heads):
    n_chunks = rows // CHUNK
    lat_chunks = n_lat // CHUNK
    head0 = pl.program_id(1) * heads
    chains = [(hh, d) for hh in range(heads) for d in (0, 1)]
    pad_scr[0:CONV_PAD, :] = jnp.zeros((CONV_PAD, HEAD_DIM), F32)
    pad_scr[CONV_PAD + rows:, :] = jnp.zeros((CONV_PAD, HEAD_DIM), F32)

    for hh in range(heads):
        cols = slice(hh * HEAD_DIM, (hh + 1) * HEAD_DIM)
        _gdn_prepare((xq_ref, xk_ref, xv_ref), (cwq_ref, cwk_ref, cwv_ref),
                     (q_scr.at[hh], k_scr.at[hh], v_scr.at[hh]), pad_scr, cols, rows=rows, n_lat=n_lat, tile=tile)

    n_groups = rows // GROUP
    lat_groups = n_lat // GROUP

    def groups_at(t):
        return (lax.rem(t + lat_groups, n_groups), n_groups - 1 - t)

    def local(t):
        return _gdn_local(groups_at(t), head0, heads, prm_ref, gates_ref, q_scr, k_scr, v_scr,
                          w_scr, u_scr, qd_scr, kd_scr, qk_scr, gt_scr)

    def scan(states, t):
        return _gdn_scan(states, groups_at(t), heads, w_scr, u_scr, qd_scr, kd_scr, qk_scr, gt_scr, oacc_scr)

    def step(t, states):
        states = list(states)
        _interleave(local(t), scan(states, t - 1))
        return tuple(states)

    _interleave(local(jnp.int32(0)))
    zero = jnp.zeros((HEAD_DIM, HEAD_DIM), F32)
    states = list(lax.fori_loop(1, n_groups, step, (zero,) * len(chains)))
    _interleave(scan(states, jnp.int32(n_groups - 1)))

    for hh in range(heads):
        cols = slice(hh * HEAD_DIM, (hh + 1) * HEAD_DIM)
        for r0 in range(0, rows, tile):
            o = oacc_scr[hh, 0, r0:r0 + tile, :] + oacc_scr[hh, 1, r0:r0 + tile, :]
            z = z_ref[r0:r0 + tile, cols].astype(F32)
            o_ref[r0:r0 + tile, cols] = (_rms(o, og_ref[...]) * (z * jax.nn.sigmoid(z))).astype(o_ref.dtype)


def _gdn(proj, gates, prm, conv_w, o_gain, *, n_heads, col0, n_lat, tile=256, heads=2):
    bsz, rows, _ = proj.shape
    n_chunks = rows // CHUNK
    wide = heads * HEAD_DIM
    kern = functools.partial(_gdn_kernel, rows=rows, n_lat=n_lat, tile=tile, heads=heads)
    seq = lambda off: pl.BlockSpec((None, rows, wide), lambda b, h: (b, 0, (col0 + off * n_heads) // heads + h))
    cw = lambda off: pl.BlockSpec((CONV_K, wide), lambda b, h: (0, off * n_heads // heads + h))
    per_head = lambda shape, dt: pltpu.VMEM((heads, 2) + shape, dt)
    return pl.pallas_call(
        kern,
        out_shape=jax.ShapeDtypeStruct((bsz, rows, n_heads * HEAD_DIM), BF16),
        grid=(bsz, n_heads // heads),
        in_specs=[
            pl.BlockSpec(memory_space=pltpu.SMEM),
            pl.BlockSpec((None, 4, heads, rows // GROUP, 1, GROUP), lambda b, h: (b, 0, h, 0, 0, 0)),
            cw(0), cw(1), cw(2),
            pl.BlockSpec((1, HEAD_DIM), lambda b, h: (0, 0)),
            seq(0), seq(1), seq(2), seq(3),
        ],
        out_specs=pl.BlockSpec((None, rows, wide), lambda b, h: (b, 0, h)),
        scratch_shapes=[
            pltpu.VMEM((rows + 2 * CONV_PAD, HEAD_DIM), F32),
            pltpu.VMEM((heads, rows, HEAD_DIM), F32),
            pltpu.VMEM((heads, rows, HEAD_DIM), F32),
            pltpu.VMEM((heads, rows, HEAD_DIM), F32),
            per_head((rows, HEAD_DIM), BF16),
            per_head((rows, HEAD_DIM), F32),
            per_head((rows, HEAD_DIM), BF16),
            per_head((rows, HEAD_DIM), BF16),
            per_head((rows, CHUNK), BF16),
            per_head((n_chunks, 1, HEAD_DIM), F32),
            per_head((rows, HEAD_DIM), F32),
        ],
        compiler_params=_params(("parallel", "arbitrary")),
        name="gdn",
    )(prm, gates, conv_w, conv_w, conv_w, o_gain, proj, proj, proj, proj)


def _mla_kernel(qng_ref, qrg_ref, kng_ref, krg_ref, cq_ref, sq_ref, ck_ref, sk_ref,
                qn_ref, qr_ref, kn_ref, v_ref, kr_ref, o_ref, kb_scr, vb_scr, s_scr, *, scale, rows, tq):
    h = pl.program_id(1)

    @pl.when(pl.program_id(2) == 0)
    def _():
        kb_scr[:, :QK_NOPE] = _rms(kn_ref[...].astype(F32), kng_ref[...]).astype(BF16)
        kr = _rope128(_rms(kr_ref[...], krg_ref[...]), ck_ref[...], sk_ref[...])
        kb_scr[:, QK_NOPE:] = kr.astype(BF16)
        vb_scr[...] = v_ref[...].astype(BF16)

    streams = []
    sub = 2 * Q_SUB
    for r0 in range(0, tq, sub):
        sl = slice(r0, r0 + sub)
        qn = _rms(qn_ref[sl, :].astype(F32), qng_ref[...])
        qr = _rope128(_group_rms(qr_ref[sl, :].astype(F32), qrg_ref[...]), cq_ref[sl, :], sq_ref[sl, :])
        lane = lax.broadcasted_iota(jnp.int32, qr.shape, 1)
        mine = (lane // QK_ROPE) == (h % 2)
        q = jnp.concatenate([qn, jnp.where(mine, qr, 0.0)], axis=-1) * (scale * LOG2E)
        streams.append((q.astype(BF16), 0, rows))
    for i, (o, l) in enumerate(_softmax_pv(streams, kb_scr, vb_scr, s_scr)):
        o_ref[i * sub:(i + 1) * sub, :] = (o * (1.0 / l)).astype(o_ref.dtype)


def _mla_attn(q, kv, proj, qn_gain, qr_gain, kn_gain, kr_gain, cos, sin, *, n_heads, n_lat, kr_block,
              tq_cap=1024):
    bsz, rows, _ = kv.shape
    tq = _pick_q_tile(n_lat, tq_cap)
    kern = functools.partial(_mla_kernel, scale=(QK_NOPE + QK_ROPE) ** -0.5, rows=rows, tq=tq)
    vec = lambda: pl.BlockSpec((1, LANES), lambda b, h, i: (0, 0))
    return pl.pallas_call(
        kern,
        out_shape=jax.ShapeDtypeStruct((bsz, n_lat, n_heads * V_HEAD), BF16),
        grid=(bsz, n_heads, n_lat // tq),
        in_specs=[
            vec(), vec(), vec(), vec(),
            pl.BlockSpec((tq, LANES), lambda b, h, i: (i, 0)),
            pl.BlockSpec((tq, LANES), lambda b, h, i: (i, 0)),
            pl.BlockSpec((rows, LANES), lambda b, h, i: (0, 0)),
            pl.BlockSpec((rows, LANES), lambda b, h, i: (0, 0)),
            pl.BlockSpec((None, tq, QK_NOPE), lambda b, h, i: (b, i, h)),
            pl.BlockSpec((None, tq, LANES), lambda b, h, i: (b, i, n_heads + h // 2)),
            pl.BlockSpec((None, rows, QK_NOPE), lambda b, h, i: (b, 0, 2 * h)),
            pl.BlockSpec((None, rows, V_HEAD), lambda b, h, i: (b, 0, 2 * h + 1)),
            pl.BlockSpec((None, rows, LANES), lambda b, h, i: (b, 0, kr_block)),
        ],
        out_specs=pl.BlockSpec((None, tq, V_HEAD), lambda b, h, i: (b, i, h)),
        scratch_shapes=[pltpu.VMEM((rows, QK_NOPE + LANES), BF16), pltpu.VMEM((rows, V_HEAD), BF16),
                        pltpu.VMEM((2, 2 * Q_SUB, rows), F32)],
        compiler_params=_params(("parallel", "parallel", "arbitrary")),
        name="mla_attn",
    )(qn_gain, qr_gain, kn_gain, kr_gain, cos, sin, cos, sin, q, q, kv, kv, proj)


def _pad_cols(w, n):
    return jnp.pad(w, ((0, 0), (0, n - w.shape[1])))


def kernel(x, c, ctx, c_ctx, ada_w, ada_b, norm1, norm2, mlp_w1, mlp_w2, e_w_in, e_q_norm, e_k_norm, e_lam_q1, e_lam_k1, e_lam_q2, e_lam_k2, e_subln, e_conv, e_alog_f, e_alog_b, e_dtb_f, e_dtb_b, e_o_norm, e_w_out, o_w_in, o_qa_norm, o_w_uq, o_kva_norm, o_w_ukv, o_qn_norm, o_qr_norm, o_kn_norm, o_kr_norm, o_w_out):
    bsz, n_lat, d = x.shape
    n_ctx = ctx.shape[1]
    rows = n_lat + n_ctx
    depth = ada_w.shape[0]
    assert depth == 2 and bsz < 8
    mix_heads = d // HEAD_DIM
    a_heads = mix_heads // 2
    b_heads = mix_heads - a_heads
    c_heads = d // V_HEAD
    dup = lambda g: jnp.tile(g.reshape(1, -1), (1, 2))

    svec = jnp.concatenate([c, c_ctx[None, :], jnp.zeros((8 - bsz - 1, d), F32)], axis=0)
    mod = _adaln_mod(svec, ada_w, ada_b).reshape(depth, 8, 6, d)

    cos, sin = _rope_tables(n_lat, n_ctx)

    a_qk = a_heads * HEAD_DIM
    main = 7 * a_qk
    n_gate = 4 * b_heads
    w_gate = _pad_cols(e_w_in[0][:, main:], LANES)
    proj, gates = _norm_matmul(x, 0, d, norm1[0], e_w_in.astype(BF16), n=main, mod=mod[0], which=0, x_tail=ctx,
                               n_lat=n_lat, w_side=w_gate.astype(BF16), out_dtype=BF16, name="in_proj0")
    lam_p = jnp.concatenate([e_lam_q1, e_lam_k1, e_lam_q2, e_lam_k2], axis=0)
    o_a, w1_0 = _diff_attn(proj, lam_p, dup(e_q_norm[0]), dup(e_k_norm[0]), e_subln[0].reshape(1, -1), cos, sin,
                           (mlp_w1, 0), n_heads=a_heads, n_lat=n_lat, lam_init=0.8 - 0.6 * math.exp(-0.3 * 0))
    gates = gates[:, :, :n_gate].reshape(bsz, rows // GROUP, GROUP, 4, b_heads)
    gates = gates.transpose(0, 3, 4, 1, 2).reshape(bsz, 4, b_heads, rows // GROUP, 1, GROUP)
    prm = jnp.concatenate([e_alog_f, e_alog_b, e_dtb_f, e_dtb_b], axis=0)
    o_b = _gdn(proj, gates, prm, e_conv[0], e_o_norm[0].reshape(1, -1), n_heads=b_heads,
               col0=3 * a_heads, n_lat=n_lat)
    xs = _out_proj(o_a, 0, o_b, 0, e_w_out.astype(BF16), x, mod[0], gate_row=2, n_lat=n_lat, rows=rows,
                   res_tail=ctx, name="out_proj0")
    d_ff = mlp_w2.shape[1]
    hid, w2 = _norm_matmul(xs, 0, d, norm2[0], w1_0, mod=mod[0], which=1, n_lat=n_lat, relu2=True,
                           cast=(mlp_w2.reshape(depth * d_ff, d), None), out_dtype=BF16, name="mlp_up0")
    w2 = w2.reshape(depth, d_ff, d)
    xs, w1_1 = _down_proj(hid, w2, xs, mod[0], layer=0, cast=(mlp_w1, 1), gate_row=5, n_lat=n_lat,
                          name="mlp_down0")

    w_in1 = jnp.concatenate([o_w_in[0], o_w_in[0][:, Q_LORA + KV_LORA:]], axis=1)
    proj = _norm_matmul(xs, 0, d, norm1[1], w_in1.astype(BF16), mod=mod[1], which=0, n_lat=n_lat,
                        tm_cap=768, tn_cap=w_in1.shape[1], name="in_proj1")
    w_uq = o_w_uq[0].reshape(Q_LORA, c_heads, QK_NOPE + QK_ROPE)
    w_uq = jnp.concatenate([w_uq[:, :, :QK_NOPE].reshape(Q_LORA, -1), w_uq[:, :, QK_NOPE:].reshape(Q_LORA, -1)],
                           axis=1)
    q = _norm_matmul(proj, 0, Q_LORA, o_qa_norm[0], w_uq.astype(BF16), tn_cap=1024, out_dtype=BF16,
                     tm_cap=768, name="q_up")
    kv = _norm_matmul(proj, 1, KV_LORA, o_kva_norm[0], o_w_ukv.astype(BF16), tn_cap=1024, out_dtype=BF16,
                      tm_cap=768, name="kv_up")
    o_c = _mla_attn(q, kv, proj, o_qn_norm[0].reshape(1, -1), dup(o_qr_norm[0]), o_kn_norm[0].reshape(1, -1),
                    dup(o_kr_norm[0]), cos, sin, n_heads=c_heads, n_lat=n_lat,
                    kr_block=(Q_LORA + KV_LORA) // LANES)
    xl = _out_proj(o_c, 0, o_c, 1, o_w_out.astype(BF16), xs, mod[1], gate_row=2, n_lat=n_lat, rows=n_lat,
                   tm_cap=1024, name="out_proj1")
    hid = _norm_matmul(xl, 0, d, norm2[1], w1_1, mod=mod[1], which=1, n_lat=n_lat,
                       relu2=True, out_dtype=BF16, tm_cap=1024, name="mlp_up1")
    return _down_proj(hid, w2, xl, mod[1], layer=1, gate_row=5, n_lat=n_lat, tm_cap=1024, name="mlp_down1")
```

```python
import functools
import math

import numpy as np
import jax
import jax.numpy as jnp
from jax import lax
from jax.experimental import pallas as pl
from jax.experimental.pallas import tpu as pltpu

EPS = 1e-6
ROPE_BASE = 10000.0
GRID_W = 64
HEAD_DIM = 128
A_SUB = 64
CHUNK = 64
CONV_K = 5
QK_NOPE = 128
QK_ROPE = 64
V_HEAD = 128
Q_LORA = 512
KV_LORA = 512
LANES = 128
CONV_PAD = 8
KEY_TILE = 256
Q_SUB = 128
GROUP = 256
NORM_ELEMS = 64 * 1024
NORM_SUB = 128
LOG2E = math.log2(math.e)
VMEM_LIMIT = 56 * 1024 * 1024

F32 = jnp.float32
BF16 = jnp.bfloat16
NT_DIMS = (((1,), (1,)), ((), ()))
TN_DIMS = (((0,), (0,)), ((), ()))


def _params(sem):
    return pltpu.CompilerParams(dimension_semantics=sem, vmem_limit_bytes=VMEM_LIMIT)


def _dot(a, b):
    return jnp.dot(a.astype(BF16), b.astype(BF16), preferred_element_type=F32)


def _dot_nt(a, b):
    return lax.dot_general(a.astype(BF16), b.astype(BF16), NT_DIMS, preferred_element_type=F32)


def _dot_tn(a, b):
    return lax.dot_general(a.astype(BF16), b.astype(BF16), TN_DIMS, preferred_element_type=F32)


def _rms(x, gain):
    return x * lax.rsqrt(jnp.mean(x * x, axis=-1, keepdims=True) + EPS) * gain


def _pick_tile(n, cap):
    t = min(n, cap)
    while n % t or t % 8:
        t -= 8
    return t


def _mod_kernel(s_ref, w_ref, b_ref, o_ref):
    s = s_ref[...]
    s = s * jax.nn.sigmoid(s)
    o_ref[...] = _dot(s, w_ref[...]) + b_ref[...]


def _adaln_mod(svec, ada_w, ada_b):
    depth, d, n = ada_w.shape
    tn = _pick_tile(n, 1024)
    return pl.pallas_call(
        _mod_kernel,
        out_shape=jax.ShapeDtypeStruct((depth, 8, n), F32),
        grid=(depth, n // tn),
        in_specs=[
            pl.BlockSpec((8, d), lambda l, j: (0, 0)),
            pl.BlockSpec((None, d, tn), lambda l, j: (l, 0, j)),
            pl.BlockSpec((None, 1, tn), lambda l, j: (l, 0, j)),
        ],
        out_specs=pl.BlockSpec((None, 8, tn), lambda l, j: (l, 0, j)),
        compiler_params=_params(("parallel", "parallel")),
        name="adaln_mod",
    )(svec, ada_w, ada_b.reshape(depth, 1, n))


def _norm_matmul_kernel(*refs, tm, n_lat, n_tiles, shift_row, scale_row, modulate, relu2, side, cast, tail):
    refs = list(refs)
    h_scr = refs.pop()
    if cast:
        cast_out = refs.pop()
    if side:
        os_ref = refs.pop()
    o_ref = refs.pop()
    if cast:
        cast_out[...] = refs.pop()[...].astype(BF16)
    if side:
        ws_ref = refs.pop()
    w_ref = refs.pop()
    x_ref, g_ref = refs[:2]
    mb_ref, mc_ref = refs[2:4] if modulate else (None, None)
    tail_ref = refs[4] if tail else None

    def project(rows):
        acc = jnp.dot(h_scr[rows, :], w_ref[...], preferred_element_type=F32)
        if relu2:
            acc = jnp.square(jnp.maximum(acc, 0.0))
        o_ref[rows, :] = acc.astype(o_ref.dtype)

    def first_step(tile):
        sub = math.gcd(tm, NORM_SUB)
        chunk = math.gcd(sub, NORM_ELEMS // x_ref.shape[1])
        for s0 in range(0, tm, sub):
            for r0 in range(s0, s0 + sub, chunk):
                if tile is None:
                    is_ctx = pl.program_id(1) * tm + r0 >= n_lat
                    src = x_ref[r0:r0 + chunk, :]
                else:
                    is_ctx = tile * tm + r0 >= n_lat
                    t0 = tile * tm + r0 - n_lat
                    src = tail_ref[t0:t0 + chunk, :] if is_ctx else x_ref[r0:r0 + chunk, :]
                y = _rms(src, g_ref[...])
                if modulate:
                    pick = lambda row: (jnp.where(is_ctx, mc_ref[row:row + 1, :], mb_ref[row:row + 1, :])
                                        if tile is None else (mc_ref if is_ctx else mb_ref)[row:row + 1, :])
                    y = y * (1.0 + pick(scale_row)) + pick(shift_row)
                h_scr[r0:r0 + chunk, :] = y.astype(BF16)
            rows = slice(s0, s0 + sub)
            project(rows)
            if side:
                os_ref[rows, :] = jnp.dot(h_scr[rows, :], ws_ref[...], preferred_element_type=F32)

    first = pl.program_id(2) == 0
    if tail:
        lat_tiles = n_lat // tm
        if lat_tiles:
            pl.when(first & (pl.program_id(1) < lat_tiles))(lambda: first_step(0))
        for t in range(lat_tiles, n_tiles):
            pl.when(first & (pl.program_id(1) == t))(functools.partial(first_step, t))
    else:
        pl.when(first)(lambda: first_step(None))

    @pl.when(pl.program_id(2) != 0)
    def _():
        project(slice(None))


def _weight_spec(w, layer, block, index_map):
    if w.ndim == 2:
        return pl.BlockSpec(block, index_map)
    return pl.BlockSpec((None,) + block, lambda *g: (layer,) + index_map(*g))


def _cast_rider(src, layer, grid):
    n_steps = math.prod(grid)
    rows, cols = src.shape[-2:]
    blk = rows // n_steps
    assert rows % n_steps == 0 and blk % 16 == 0, (src.shape, grid)

    def lin(*g):
        idx = g[0]
        for size, gi in zip(grid[1:], g[1:]):
            idx = idx * size + gi
        return idx

    if layer is None:
        in_spec = pl.BlockSpec((blk, cols), lambda *g: (lin(*g), 0))
    else:
        in_spec = pl.BlockSpec((None, blk, cols), lambda *g: (layer, lin(*g), 0))
    out_spec = pl.BlockSpec((blk, cols), lambda *g: (lin(*g), 0))
    return in_spec, out_spec, jax.ShapeDtypeStruct((rows, cols), BF16)


def _norm_matmul(x, col_block, k, gain, w, *, layer=0, n=None, mod=None, which=0, n_lat=None, relu2=False,
                 w_side=None, cast=None, x_tail=None, out_dtype=F32, tm_cap=1152, tn_cap=1024,
                 name="norm_matmul"):
    bsz, rows, _ = x.shape
    if x_tail is not None:
        rows += x_tail.shape[1]
    if n is None:
        n = w.shape[-1]
    tm = _pick_tile(rows, tm_cap)
    tn = _pick_tile(n, tn_cap)
    modulate = mod is not None
    side = w_side is not None
    if n_lat is None:
        n_lat = rows
    in_specs = [
        pl.BlockSpec((None, tm, k), lambda b, i, j: (b, i, col_block)),
        pl.BlockSpec((1, k), lambda b, i, j: (0, 0)),
    ]
    args = [x, gain.reshape(1, k)]
    if modulate:
        in_specs += [
            pl.BlockSpec((None, 6, k), lambda b, i, j: (b, 0, 0)),
            pl.BlockSpec((None, 6, k), lambda b, i, j: (bsz, 0, 0)),
        ]
        args += [mod, mod]
    if x_tail is not None:
        in_specs.append(pl.BlockSpec((None, x_tail.shape[1], k), lambda b, i, j: (b, 0, col_block)))
        args.append(x_tail)
    in_specs.append(_weight_spec(w, layer, (k, tn), lambda b, i, j: (0, j)))
    args.append(w)
    out_shape = [jax.ShapeDtypeStruct((bsz, rows, n), out_dtype)]
    out_specs = [pl.BlockSpec((None, tm, tn), lambda b, i, j: (b, i, j))]
    if side:
        in_specs.append(pl.BlockSpec((k, LANES), lambda b, i, j: (0, 0)))
        args.append(w_side)
        out_shape.append(jax.ShapeDtypeStruct((bsz, rows, LANES), F32))
        out_specs.append(pl.BlockSpec((None, tm, LANES), lambda b, i, j: (b, i, 0)))
    grid = (bsz, rows // tm, n // tn)
    if cast is not None:
        c_in, c_out, c_shape = _cast_rider(cast[0], cast[1], grid)
        in_specs.append(c_in)
        args.append(cast[0])
        out_shape.append(c_shape)
        out_specs.append(c_out)
    kern = functools.partial(_norm_matmul_kernel, tm=tm, n_lat=n_lat, n_tiles=rows // tm, shift_row=3 * which,
                             scale_row=3 * which + 1, modulate=modulate, relu2=relu2, side=side,
                             cast=cast is not None, tail=x_tail is not None)
    out = pl.pallas_call(
        kern,
        out_shape=out_shape,
        grid=grid,
        in_specs=in_specs,
        out_specs=out_specs,
        scratch_shapes=[pltpu.VMEM((tm, k), BF16)],
        compiler_params=_params(("parallel", "parallel", "arbitrary")),
        name=name,
    )(*args)
    return out if len(out) > 1 else out[0]


def _gate(mb_ref, mc_ref, gate_row, i, tm, n_lat):
    t = i * tm + lax.broadcasted_iota(jnp.int32, (tm, 1), 0)
    return jnp.where(t >= n_lat, mc_ref[gate_row:gate_row + 1, :], mb_ref[gate_row:gate_row + 1, :])


def _out_proj_kernel(a1_ref, a2_ref, w_ref, r_ref, mb_ref, mc_ref, *rest, tm, n_lat, n_tiles, k1, gate_row):
    acc = jnp.dot(a1_ref[...], w_ref[:k1, :], preferred_element_type=F32)
    acc += jnp.dot(a2_ref[...], w_ref[k1:, :], preferred_element_type=F32)
    if len(rest) == 1:
        o_ref, = rest
        g = _gate(mb_ref, mc_ref, gate_row, pl.program_id(1), tm, n_lat)
        o_ref[...] = r_ref[...] + g * acc
        return
    tail_ref, o_ref = rest
    gate_b = mb_ref[gate_row:gate_row + 1, :]
    gate_c = mc_ref[gate_row:gate_row + 1, :]

    def finish(tile):
        n_own = min(max(n_lat - tile * tm, 0), tm)
        if n_own:
            o_ref[:n_own, :] = r_ref[:n_own, :] + gate_b * acc[:n_own]
        if n_own < tm:
            t0 = tile * tm + n_own - n_lat
            o_ref[n_own:, :] = tail_ref[t0:t0 + tm - n_own, :] + gate_c * acc[n_own:]

    lat_tiles = n_lat // tm
    if lat_tiles:
        pl.when(pl.program_id(1) < lat_tiles)(lambda: finish(0))
    for t in range(lat_tiles, n_tiles):
        pl.when(pl.program_id(1) == t)(functools.partial(finish, t))


def _out_proj(a1, c1, a2, c2, w, res, mod, *, gate_row, n_lat, rows, layer=0, res_tail=None, tm_cap=1152,
              tn_cap=1024, name="out_proj"):
    bsz = res.shape[0]
    d = w.shape[-1]
    k1 = w.shape[-2] // 2
    tm = _pick_tile(rows, tm_cap)
    tn = _pick_tile(d, tn_cap)
    kern = functools.partial(_out_proj_kernel, tm=tm, n_lat=n_lat, n_tiles=rows // tm, k1=k1, gate_row=gate_row)
    in_specs = [
        pl.BlockSpec((None, tm, k1), lambda b, i, j: (b, i, c1)),
        pl.BlockSpec((None, tm, k1), lambda b, i, j: (b, i, c2)),
        _weight_spec(w, layer, (2 * k1, tn), lambda b, i, j: (0, j)),
        pl.BlockSpec((None, tm, tn), lambda b, i, j: (b, i, j)),
        pl.BlockSpec((None, 6, tn), lambda b, i, j: (b, 0, j)),
        pl.BlockSpec((None, 6, tn), lambda b, i, j: (bsz, 0, j)),
    ]
    args = [a1, a2, w, res, mod, mod]
    if res_tail is not None:
        in_specs.append(pl.BlockSpec((None, res_tail.shape[1], tn), lambda b, i, j: (b, 0, j)))
        args.append(res_tail)
    return pl.pallas_call(
        kern,
        out_shape=jax.ShapeDtypeStruct((bsz, rows, d), F32),
        grid=(bsz, rows // tm, d // tn),
        in_specs=in_specs,
        out_specs=pl.BlockSpec((None, tm, tn), lambda b, i, j: (b, i, j)),
        compiler_params=_params(("parallel", "parallel", "arbitrary")),
        name=name,
    )(*args)


def _down_proj_kernel(a_ref, w_ref, r_ref, mb_ref, mc_ref, *rest, tm, n_lat, gate_row, cast):
    if cast:
        cast_in, o_ref, cast_out, acc_scr = rest
        cast_out[...] = cast_in[...].astype(BF16)
    else:
        o_ref, acc_scr = rest
    kk = pl.program_id(3)

    @pl.when(kk == 0)
    def _():
        acc_scr[...] = jnp.zeros_like(acc_scr)

    acc_scr[...] += jnp.dot(a_ref[...], w_ref[...], preferred_element_type=F32)

    @pl.when(kk == pl.num_programs(3) - 1)
    def _():
        g = _gate(mb_ref, mc_ref, gate_row, pl.program_id(1), tm, n_lat)
        o_ref[...] = r_ref[...] + g * acc_scr[...]


def _down_proj(a, w, res, mod, *, gate_row, n_lat, layer=0, cast=None, tm_cap=1152, tn_cap=512, tk=4096,
               name="down_proj"):
    bsz, rows, kdim = a.shape
    d = w.shape[-1]
    tm = _pick_tile(rows, tm_cap)
    tn = _pick_tile(d, tn_cap)
    grid = (bsz, rows // tm, d // tn, kdim // tk)
    in_specs = [
        pl.BlockSpec((None, tm, tk), lambda b, i, j, k: (b, i, k)),
        _weight_spec(w, layer, (tk, tn), lambda b, i, j, k: (k, j)),
        pl.BlockSpec((None, tm, tn), lambda b, i, j, k: (b, i, j)),
        pl.BlockSpec((None, 6, tn), lambda b, i, j, k: (b, 0, j)),
        pl.BlockSpec((None, 6, tn), lambda b, i, j, k: (bsz, 0, j)),
    ]
    args = [a, w, res, mod, mod]
    out_shape = [jax.ShapeDtypeStruct((bsz, rows, d), F32)]
    out_specs = [pl.BlockSpec((None, tm, tn), lambda b, i, j, k: (b, i, j))]
    if cast is not None:
        c_in, c_out, c_shape = _cast_rider(cast[0], cast[1], grid)
        in_specs.append(c_in)
        args.append(cast[0])
        out_shape.append(c_shape)
        out_specs.append(c_out)
    kern = functools.partial(_down_proj_kernel, tm=tm, n_lat=n_lat, gate_row=gate_row, cast=cast is not None)
    out = pl.pallas_call(
        kern,
        out_shape=out_shape,
        grid=grid,
        in_specs=in_specs,
        out_specs=out_specs,
        scratch_shapes=[pltpu.VMEM((tm, tn), F32)],
        compiler_params=_params(("parallel", "parallel", "parallel", "arbitrary")),
        name=name,
    )(*args)
    return out if len(out) > 1 else out[0]


def _rope_tables(n_lat, n_ctx):
    t = np.arange(n_lat)
    row = (t // GRID_W).astype(np.float64)
    col = (t % GRID_W).astype(np.float64)
    n_freq = A_SUB // 4
    inv_freq = ROPE_BASE ** (-np.arange(n_freq, dtype=np.float64) / n_freq)
    ang = np.concatenate([row[:, None] * inv_freq, col[:, None] * inv_freq], axis=-1)
    cos = np.concatenate([np.cos(ang), np.ones((n_ctx, A_SUB // 2))], axis=0)
    sin = np.concatenate([np.sin(ang), np.zeros((n_ctx, A_SUB // 2))], axis=0)
    cos128 = np.tile(cos, (1, 4))
    sin128 = np.tile(np.concatenate([-sin, sin], axis=1), (1, 2))
    return jnp.asarray(cos128, F32), jnp.asarray(sin128, F32)


def _rope128(x, cos, sin):
    lane = lax.broadcasted_iota(jnp.int32, x.shape, 1)
    first = (lane % A_SUB) < (A_SUB // 2)
    partner = jnp.where(first, pltpu.roll(x, LANES - A_SUB // 2, 1), pltpu.roll(x, A_SUB // 2, 1))
    return x * cos + partner * sin


def _group_rms(x, gain):
    lane = lax.broadcasted_iota(jnp.int32, x.shape, 1)
    lo = lane < A_SUB
    xx = x * x
    s_lo = jnp.sum(jnp.where(lo, xx, 0.0), axis=-1, keepdims=True)
    s_hi = jnp.sum(jnp.where(lo, 0.0, xx), axis=-1, keepdims=True)
    inv = jnp.where(lo, lax.rsqrt(s_lo / A_SUB + EPS), lax.rsqrt(s_hi / A_SUB + EPS))
    return x * inv * gain


def _pick_q_tile(n, cap):
    return max(t for t in range(Q_SUB, cap + 1, Q_SUB) if n % t == 0)


def _softmax_pv(streams, kb_scr, vb_scr, s_scr):
    n = len(streams)
    n_rows = streams[0][0].shape[0]
    dv = vb_scr.shape[1]
    outs = []
    m_prev = None
    for i in range(n + 1):
        score_spans = list(range(streams[i][1], streams[i][2], KEY_TILE)) if i < n else []
        value_spans = list(range(streams[i - 1][1], streams[i - 1][2], KEY_TILE)) if i > 0 else []
        m_acc = jnp.full((n_rows, LANES), -jnp.inf, F32)
        l_acc = jnp.zeros((n_rows, LANES), F32)
        o = jnp.zeros((n_rows, dv), F32)
        for j in range(max(len(score_spans), len(value_spans))):
            if j < len(score_spans):
                ks = score_spans[j]
                s = lax.dot_general(streams[i][0], kb_scr[ks:ks + KEY_TILE, :], NT_DIMS,
                                    preferred_element_type=F32)
                s_scr[i % 2, :, j * KEY_TILE:(j + 1) * KEY_TILE] = s
                for t in range(KEY_TILE // LANES):
                    m_acc = jnp.maximum(m_acc, s[:, t * LANES:(t + 1) * LANES])
            if j < len(value_spans):
                ks = value_spans[j]
                e = jnp.exp2(s_scr[(i - 1) % 2, :, j * KEY_TILE:(j + 1) * KEY_TILE] - m_prev)
                for t in range(KEY_TILE // LANES):
                    l_acc = l_acc + e[:, t * LANES:(t + 1) * LANES]
                o = o + jnp.dot(e.astype(BF16), vb_scr[ks:ks + KEY_TILE, :], preferred_element_type=F32)
        if i > 0:
            outs.append((o, jnp.sum(l_acc, axis=-1, keepdims=True)))
        if i < n:
            m_prev = jnp.max(m_acc, axis=-1, keepdims=True)
    return outs


def _diff_attn_kernel(lam_ref, qg_ref, kg_ref, sg_ref, cq_ref, sq_ref, ck_ref, sk_ref,
                      q_ref, k_ref, v_ref, cast_in, o_ref, cast_out, kb_scr, vb_scr, s_scr,
                      *, rows, n_lat, lam_init, tq):
    qi = pl.program_id(2)
    cast_out[...] = cast_in[...].astype(BF16)

    @pl.when(qi == 0)
    def _():
        k = _rope128(_group_rms(k_ref[...].astype(F32), kg_ref[...]), ck_ref[...], sk_ref[...])
        kb_scr[...] = k.astype(BF16)
        vb_scr[...] = v_ref[...].astype(BF16)

    lp = lam_ref[...]
    lam = (jnp.exp(jnp.sum(lp[0:1] * lp[1:2], axis=-1, keepdims=True))
           - jnp.exp(jnp.sum(lp[2:3] * lp[3:4], axis=-1, keepdims=True)) + lam_init)

    def attend(first_key):
        streams = []
        for i, k0 in enumerate(first_key):
            sl = slice(i * Q_SUB, (i + 1) * Q_SUB)
            q = _rope128(_group_rms(q_ref[sl, :].astype(F32), qg_ref[...]), cq_ref[sl, :], sq_ref[sl, :])
            q = q * (A_SUB ** -0.5 * LOG2E)
            lane = lax.broadcasted_iota(jnp.int32, q.shape, 1)
            q12 = jnp.concatenate([jnp.where(lane < A_SUB, q, 0.0), jnp.where(lane < A_SUB, 0.0, q)], axis=0)
            streams.append((q12.astype(BF16), k0, rows))
        outs = _softmax_pv(streams, kb_scr, vb_scr, s_scr)
        for i, (o12, l12) in enumerate(outs):
            o = o12[:Q_SUB] * (1.0 / l12[:Q_SUB]) - o12[Q_SUB:] * (lam / l12[Q_SUB:])
            o_ref[i * Q_SUB:(i + 1) * Q_SUB, :] = (_rms(o, sg_ref[...]) * (1.0 - lam_init)).astype(o_ref.dtype)

    patterns = {}
    for t in range(rows // tq):
        pat = tuple(n_lat if t * tq + r0 >= n_lat else 0 for r0 in range(0, tq, Q_SUB))
        patterns.setdefault(pat, []).append(t)
    for pat, tiles in patterns.items():
        @pl.when((qi >= tiles[0]) & (qi <= tiles[-1]))
        def _(pat=pat):
            attend(pat)


def _diff_attn(proj, lam_p, q_gain, k_gain, sub_gain, cos, sin, cast, *, n_heads, n_lat, lam_init, tq_cap=1152):
    bsz, rows, _ = proj.shape
    tq = _pick_q_tile(rows, tq_cap)
    kern = functools.partial(_diff_attn_kernel, rows=rows, n_lat=n_lat, lam_init=lam_init, tq=tq)
    vec = lambda: pl.BlockSpec((1, LANES), lambda b, h, i: (0, 0))
    grid = (bsz, n_heads, rows // tq)
    c_in, c_out, c_shape = _cast_rider(cast[0], cast[1], grid)
    return pl.pallas_call(
        kern,
        out_shape=[jax.ShapeDtypeStruct((bsz, rows, n_heads * HEAD_DIM), BF16), c_shape],
        grid=grid,
        in_specs=[
            pl.BlockSpec((4, A_SUB), lambda b, h, i: (0, 0)),
            vec(), vec(), vec(),
            pl.BlockSpec((tq, LANES), lambda b, h, i: (i, 0)),
            pl.BlockSpec((tq, LANES), lambda b, h, i: (i, 0)),
            pl.BlockSpec((rows, LANES), lambda b, h, i: (0, 0)),
            pl.BlockSpec((rows, LANES), lambda b, h, i: (0, 0)),
            pl.BlockSpec((None, tq, HEAD_DIM), lambda b, h, i: (b, i, h)),
            pl.BlockSpec((None, rows, HEAD_DIM), lambda b, h, i: (b, 0, n_heads + h)),
            pl.BlockSpec((None, rows, HEAD_DIM), lambda b, h, i: (b, 0, 2 * n_heads + h)),
            c_in,
        ],
        out_specs=[pl.BlockSpec((None, tq, HEAD_DIM), lambda b, h, i: (b, i, h)), c_out],
        scratch_shapes=[pltpu.VMEM((rows, HEAD_DIM), BF16), pltpu.VMEM((rows, HEAD_DIM), BF16),
                        pltpu.VMEM((2, 2 * Q_SUB, rows), F32)],
        compiler_params=_params(("parallel", "parallel", "arbitrary")),
        name="diff_attn",
    )(lam_p, q_gain, k_gain, sub_gain, cos, sin, cos, sin, proj, proj, proj, cast[0])


def _col(row_vec, eye):
    return jnp.sum(jnp.where(eye, row_vec, 0.0), axis=-1, keepdims=True)


def _gdn_prepare(x_refs, cw_refs, dsts, pad_scr, cols, *, rows, n_lat, tile):
    n_tiles = rows // tile
    for x_ref, cw_ref, dst, kind in zip(x_refs, cw_refs, dsts, "qkv"):
        pad_scr[CONV_PAD:CONV_PAD + rows, :] = x_ref[:, cols].astype(F32)
        cw = cw_ref[:, cols]
        for r in range(n_tiles):
            r0 = r * tile
            t = r0 + lax.broadcasted_iota(jnp.int32, (tile, 1), 0)
            acc = jnp.zeros((tile, HEAD_DIM), F32)
            for j in range(CONV_K):
                dd = j - CONV_K // 2
                xs = pad_scr[CONV_PAD + r0 + dd:CONV_PAD + r0 + dd + tile, :]
                if dd > 0 and r0 < n_lat <= r0 + tile + dd and n_lat < rows:
                    xs = jnp.where((t < n_lat) & (t + dd >= n_lat), 0.0, xs)
                if dd < 0 and r0 + dd < n_lat <= r0 + tile and n_lat < rows:
                    xs = jnp.where((t >= n_lat) & (t + dd < n_lat), 0.0, xs)
                acc = acc + xs * cw[j:j + 1, :]
            y = acc * jax.nn.sigmoid(acc)
            if kind != "v":
                y = y * lax.rsqrt(jnp.sum(y * y, axis=-1, keepdims=True) + EPS)
            if kind == "q":
                y = y * (HEAD_DIM ** -0.5)
            dst[r0:r0 + tile, :] = y


def _gdn_local(groups, head0, heads, prm_ref, gates_ref, q_scr, k_scr, v_scr,
               w_scr, u_scr, qd_scr, kd_scr, qk_scr, gt_scr):
    per = GROUP // CHUNK
    ii = lax.broadcasted_iota(jnp.int32, (GROUP, GROUP), 0)
    jj = lax.broadcasted_iota(jnp.int32, (GROUP, GROUP), 1)
    eye = ii == jj
    same = (ii // CHUNK) == (jj // CHUNK)
    hds = range(heads)
    chains = [(hh, d) for hh in hds for d in (0, 1)]
    idx = range(len(chains))
    r0s = [pl.multiple_of(groups[d] * GROUP, GROUP) for hh, d in chains]
    rgs = [pl.ds(r0, GROUP) for r0 in r0s]
    kc = [k_scr[hh, rgs[ci], :] for ci, (hh, d) in enumerate(chains)]
    vc = [v_scr[hh, rgs[ci], :] for ci, (hh, d) in enumerate(chains)]
    qc = [q_scr[hh, rgs[ci], :] for ci, (hh, d) in enumerate(chains)]
    kk = [_dot_nt(kc[ci], kc[ci]) for ci in idx]
    qk = [_dot_nt(qc[ci], kc[ci]) for ci in idx]
    yield
    big_g, beta = [], []
    for hh, d in chains:
        alog = jnp.full((1, GROUP), prm_ref[d, head0 + hh], F32)
        dtb = jnp.full((1, GROUP), prm_ref[2 + d, head0 + hh], F32)
        g = -jnp.exp(alog) * jax.nn.softplus(gates_ref[d, hh, groups[d]] + dtb)
        beta.append(jax.nn.sigmoid(gates_ref[2 + d, hh, groups[d]]))
        incl = same & ((ii <= jj) if d == 0 else (ii >= jj))
        big_g.append(jnp.sum(jnp.where(incl, _col(g, eye), 0.0), axis=0, keepdims=True))
    yield
    g_col, beta_col, g_end_col, gamma, p, t_inv = [], [], [], [], [], []
    for ci, (hh, d) in enumerate(chains):
        g_col.append(_col(big_g[ci], eye))
        beta_col.append(_col(beta[ci], eye))
        end_lane = (ii // CHUNK) * CHUNK + (CHUNK - 1 if d == 0 else 0)
        g_end_col.append(jnp.sum(jnp.where(jj == end_lane, big_g[ci], 0.0), axis=-1, keepdims=True))
        after = same & ((ii >= jj) if d == 0 else (ii <= jj))
        strict = same & ((ii > jj) if d == 0 else (ii < jj))
        gamma.append(jnp.exp(jnp.where(after, g_col[ci] - big_g[ci], -jnp.inf)))
        p.append(jnp.where(strict, -(beta_col[ci] * kk[ci] * gamma[ci]), 0.0))
        t_inv.append(jnp.where(eye, 1.0, 0.0) + p[ci])
    yield
    for _ in range(int(math.log2(CHUNK)) - 1):
        p = [_dot(p[ci], p[ci]) for ci in idx]
        yield
        t_inv = [t_inv[ci] + _dot(t_inv[ci], p[ci]) for ci in idx]
        yield
    e_col = [jnp.exp(g_col[ci]) for ci in idx]
    wu = [_dot(t_inv[ci], jnp.concatenate([kc[ci] * (beta_col[ci] * e_col[ci]), vc[ci] * beta_col[ci]], axis=1))
          for ci in idx]
    yield
    for ci, (hh, d) in enumerate(chains):
        w_scr[hh, d, rgs[ci], :] = wu[ci][:, :HEAD_DIM].astype(BF16)
        u_scr[hh, d, rgs[ci], :] = wu[ci][:, HEAD_DIM:]
        qd_scr[hh, d, rgs[ci], :] = (qc[ci] * e_col[ci]).astype(BF16)
        kd_scr[hh, d, rgs[ci], :] = (kc[ci] * jnp.exp(g_end_col[ci] - g_col[ci])).astype(BF16)
        qkg = (qk[ci] * gamma[ci]).astype(BF16)
        for a in range(per):
            lo = a * CHUNK
            qk_scr[hh, d, pl.ds(r0s[ci] + lo, CHUNK), :] = qkg[lo:lo + CHUNK, lo:lo + CHUNK]
            end = lo + (CHUNK - 1 if d == 0 else 0)
            gt_scr[hh, d, groups[d] * per + a] = jnp.broadcast_to(jnp.exp(big_g[ci][:, end:end + 1]),
                                                                   (1, HEAD_DIM))
    yield


def _gdn_scan(states, groups, heads, w_scr, u_scr, qd_scr, kd_scr, qk_scr, gt_scr, oacc_scr):
    per = GROUP // CHUNK
    chains = [(hh, d) for hh in range(heads) for d in (0, 1)]
    for a in range(per):
        cidx = [groups[d] * per + (a if d == 0 else per - 1 - a) for hh, d in chains]
        rc = [pl.ds(pl.multiple_of(c * CHUNK, CHUNK), CHUNK) for c in cidx]
        sb = [st.astype(BF16) for st in states]
        ws = [jnp.dot(w_scr[hh, d, rc[ci], :], sb[ci], preferred_element_type=F32)
              for ci, (hh, d) in enumerate(chains)]
        yield
        vb = [(u_scr[hh, d, rc[ci], :] - ws[ci]).astype(BF16) for ci, (hh, d) in enumerate(chains)]
        new_states = [gt_scr[hh, d, cidx[ci]] * states[ci] + lax.dot_general(
            kd_scr[hh, d, rc[ci], :], vb[ci], TN_DIMS, preferred_element_type=F32)
            for ci, (hh, d) in enumerate(chains)]
        yield
        for ci, (hh, d) in enumerate(chains):
            oacc_scr[hh, d, rc[ci], :] = (jnp.dot(qd_scr[hh, d, rc[ci], :], sb[ci], preferred_element_type=F32)
                                          + jnp.dot(qk_scr[hh, d, rc[ci], :], vb[ci], preferred_element_type=F32))
        states[:] = new_states
        yield


def _interleave(*gens):
    live = list(gens)
    while live:
        for g in list(live):
            try:
                next(g)
            except StopIteration:
                live.remove(g)


def _gdn_kernel(prm_ref, gates_ref, cwq_ref, cwk_ref, cwv_ref, og_ref, xq_ref, xk_ref, xv_ref, z_ref,
                o_ref, pad_scr, q_scr, k_scr, v_scr, w_scr, u_scr, qd_scr, kd_scr, qk_scr, gt_scr, oacc_scr,
                *, rows, n_lat, tile, heads):
    n_chunks = rows // CHUNK
    lat_chunks = n_lat // CHUNK
    head0 = pl.program_id(1) * heads
    chains = [(hh, d) for hh in range(heads) for d in (0, 1)]
    pad_scr[0:CONV_PAD, :] = jnp.zeros((CONV_PAD, HEAD_DIM), F32)
    pad_scr[CONV_PAD + rows:, :] = jnp.zeros((CONV_PAD, HEAD_DIM), F32)

    for hh in range(heads):
        cols = slice(hh * HEAD_DIM, (hh + 1) * HEAD_DIM)
        _gdn_prepare((xq_ref, xk_ref, xv_ref), (cwq_ref, cwk_ref, cwv_ref),
                     (q_scr.at[hh], k_scr.at[hh], v_scr.at[hh]), pad_scr, cols, rows=rows, n_lat=n_lat, tile=tile)

    n_groups = rows // GROUP
    lat_groups = n_lat // GROUP

    def groups_at(t):
        return (lax.rem(t + lat_groups, n_groups), n_groups - 1 - t)

    def local(t):
        return _gdn_local(groups_at(t), head0, heads, prm_ref, gates_ref, q_scr, k_scr, v_scr,
                          w_scr, u_scr, qd_scr, kd_scr, qk_scr, gt_scr)

    def scan(states, t):
        return _gdn_scan(states, groups_at(t), heads, w_scr, u_scr, qd_scr, kd_scr, qk_scr, gt_scr, oacc_scr)

    def step(t, states):
        states = list(states)
        _interleave(local(t), scan(states, t - 1))
        return tuple(states)

    _interleave(local(jnp.int32(0)))
    zero = jnp.zeros((HEAD_DIM, HEAD_DIM), F32)
    states = list(lax.fori_loop(1, n_groups, step, (zero,) * len(chains)))
    _interleave(scan(states, jnp.int32(n_groups - 1)))

    for hh in range(heads):
        cols = slice(hh * HEAD_DIM, (hh + 1) * HEAD_DIM)
        for r0 in range(0, rows, tile):
            o = oacc_scr[hh, 0, r0:r0 + tile, :] + oacc_scr[hh, 1, r0:r0 + tile, :]
            z = z_ref[r0:r0 + tile, cols].astype(F32)
            o_ref[r0:r0 + tile, cols] = (_rms(o, og_ref[...]) * (z * jax.nn.sigmoid(z))).astype(o_ref.dtype)


def _gdn(proj, gates, prm, conv_w, o_gain, *, n_heads, col0, n_lat, tile=256, heads=2):
    bsz, rows, _ = proj.shape
    n_chunks = rows // CHUNK
    wide = heads * HEAD_DIM
    kern = functools.partial(_gdn_kernel, rows=rows, n_lat=n_lat, tile=tile, heads=heads)
    seq = lambda off: pl.BlockSpec((None, rows, wide), lambda b, h: (b, 0, (col0 + off * n_heads) // heads + h))
    cw = lambda off: pl.BlockSpec((CONV_K, wide), lambda b, h: (0, off * n_heads // heads + h))
    per_head = lambda shape, dt: pltpu.VMEM((heads, 2) + shape, dt)
    return pl.pallas_call(
        kern,
        out_shape=jax.ShapeDtypeStruct((bsz, rows, n_heads * HEAD_DIM), BF16),
        grid=(bsz, n_heads // heads),
        in_specs=[
            pl.BlockSpec(memory_space=pltpu.SMEM),
            pl.BlockSpec((None, 4, heads, rows // GROUP, 1, GROUP), lambda b, h: (b, 0, h, 0, 0, 0)),
            cw(0), cw(1), cw(2),
            pl.BlockSpec((1, HEAD_DIM), lambda b, h: (0, 0)),
            seq(0), seq(1), seq(2), seq(3),
        ],
        out_specs=pl.BlockSpec((None, rows, wide), lambda b, h: (b, 0, h)),
        scratch_shapes=[
            pltpu.VMEM((rows + 2 * CONV_PAD, HEAD_DIM), F32),
            pltpu.VMEM((heads, rows, HEAD_DIM), F32),
            pltpu.VMEM((heads, rows, HEAD_DIM), F32),
            pltpu.VMEM((heads, rows, HEAD_DIM), F32),
            per_head((rows, HEAD_DIM), BF16),
            per_head((rows, HEAD_DIM), F32),
            per_head((rows, HEAD_DIM), BF16),
            per_head((rows, HEAD_DIM), BF16),
            per_head((rows, CHUNK), BF16),
            per_head((n_chunks, 1, HEAD_DIM), F32),
            per_head((rows, HEAD_DIM), F32),
        ],
        compiler_params=_params(("parallel", "arbitrary")),
        name="gdn",
    )(prm, gates, conv_w, conv_w, conv_w, o_gain, proj, proj, proj, proj)


def _mla_kernel(qng_ref, qrg_ref, kng_ref, krg_ref, cq_ref, sq_ref, ck_ref, sk_ref,
                qn_ref, qr_ref, kn_ref, v_ref, kr_ref, o_ref, kb_scr, vb_scr, s_scr, *, scale, rows, tq):
    h = pl.program_id(1)

    @pl.when(pl.program_id(2) == 0)
    def _():
        kb_scr[:, :QK_NOPE] = _rms(kn_ref[...].astype(F32), kng_ref[...]).astype(BF16)
        vb_scr[...] = v_ref[...].astype(BF16)

    @pl.when((pl.program_id(2) == 0) & (h == 0))
    def _():
        kr = _rope128(_rms(kr_ref[...], krg_ref[...]), ck_ref[...], sk_ref[...])
        kb_scr[:, QK_NOPE:] = kr.astype(BF16)

    streams = []
    sub = 2 * Q_SUB
    for r0 in range(0, tq, sub):
        sl = slice(r0, r0 + sub)
        qn = _rms(qn_ref[sl, :].astype(F32), qng_ref[...])
        qr = _rope128(_group_rms(qr_ref[sl, :].astype(F32), qrg_ref[...]), cq_ref[sl, :], sq_ref[sl, :])
        lane = lax.broadcasted_iota(jnp.int32, qr.shape, 1)
        mine = (lane // QK_ROPE) == (h % 2)
        q = jnp.concatenate([qn, jnp.where(mine, qr, 0.0)], axis=-1) * (scale * LOG2E)
        streams.append((q.astype(BF16), 0, rows))
    for i, (o, l) in enumerate(_softmax_pv(streams, kb_scr, vb_scr, s_scr)):
        o_ref[i * sub:(i + 1) * sub, :] = (o * (1.0 / l)).astype(o_ref.dtype)


def _mla_attn(q, kv, proj, qn_gain, qr_gain, kn_gain, kr_gain, cos, sin, *, n_heads, n_lat, kr_block,
              tq_cap=1024):
    bsz, rows, _ = kv.shape
    tq = _pick_q_tile(n_lat, tq_cap)
    kern = functools.partial(_mla_kernel, scale=(QK_NOPE + QK_ROPE) ** -0.5, rows=rows, tq=tq)
    vec = lambda: pl.BlockSpec((1, LANES), lambda b, h, i: (0, 0))
    return pl.pallas_call(
        kern,
        out_shape=jax.ShapeDtypeStruct((bsz, n_lat, n_heads * V_HEAD), BF16),
        grid=(bsz, n_heads, n_lat // tq),
        in_specs=[
            vec(), vec(), vec(), vec(),
            pl.BlockSpec((tq, LANES), lambda b, h, i: (i, 0)),
            pl.BlockSpec((tq, LANES), lambda b, h, i: (i, 0)),
            pl.BlockSpec((rows, LANES), lambda b, h, i: (0, 0)),
            pl.BlockSpec((rows, LANES), lambda b, h, i: (0, 0)),
            pl.BlockSpec((None, tq, QK_NOPE), lambda b, h, i: (b, i, h)),
            pl.BlockSpec((None, tq, LANES), lambda b, h, i: (b, i, n_heads + h // 2)),
            pl.BlockSpec((None, rows, QK_NOPE), lambda b, h, i: (b, 0, 2 * h)),
            pl.BlockSpec((None, rows, V_HEAD), lambda b, h, i: (b, 0, 2 * h + 1)),
            pl.BlockSpec((None, rows, LANES), lambda b, h, i: (b, 0, kr_block)),
        ],
        out_specs=pl.BlockSpec((None, tq, V_HEAD), lambda b, h, i: (b, i, h)),
        scratch_shapes=[pltpu.VMEM((rows, QK_NOPE + LANES), BF16), pltpu.VMEM((rows, V_HEAD), BF16),
                        pltpu.VMEM((2, 2 * Q_SUB, rows), F32)],
        compiler_params=_params(("parallel", "arbitrary", "arbitrary")),
        name="mla_attn",
    )(qn_gain, qr_gain, kn_gain, kr_gain, cos, sin, cos, sin, q, q, kv, kv, proj)


def _pad_cols(w, n):
    return jnp.pad(w, ((0, 0), (0, n - w.shape[1])))


def kernel(x, c, ctx, c_ctx, ada_w, ada_b, norm1, norm2, mlp_w1, mlp_w2, e_w_in, e_q_norm, e_k_norm, e_lam_q1, e_lam_k1, e_lam_q2, e_lam_k2, e_subln, e_conv, e_alog_f, e_alog_b, e_dtb_f, e_dtb_b, e_o_norm, e_w_out, o_w_in, o_qa_norm, o_w_uq, o_kva_norm, o_w_ukv, o_qn_norm, o_qr_norm, o_kn_norm, o_kr_norm, o_w_out):
    bsz, n_lat, d = x.shape
    n_ctx = ctx.shape[1]
    rows = n_lat + n_ctx
    depth = ada_w.shape[0]
    assert depth == 2 and bsz < 8
    mix_heads = d // HEAD_DIM
    a_heads = mix_heads // 2
    b_heads = mix_heads - a_heads
    c_heads = d // V_HEAD
    dup = lambda g: jnp.tile(g.reshape(1, -1), (1, 2))

    svec = jnp.concatenate([c, c_ctx[None, :], jnp.zeros((8 - bsz - 1, d), F32)], axis=0)
    mod = _adaln_mod(svec, ada_w, ada_b).reshape(depth, 8, 6, d)

    cos, sin = _rope_tables(n_lat, n_ctx)

    a_qk = a_heads * HEAD_DIM
    main = 7 * a_qk
    n_gate = 4 * b_heads
    w_gate = _pad_cols(e_w_in[0][:, main:], LANES)
    proj, gates = _norm_matmul(x, 0, d, norm1[0], e_w_in.astype(BF16), n=main, mod=mod[0], which=0, x_tail=ctx,
                               n_lat=n_lat, w_side=w_gate.astype(BF16), out_dtype=BF16, name="in_proj0")
    lam_p = jnp.concatenate([e_lam_q1, e_lam_k1, e_lam_q2, e_lam_k2], axis=0)
    o_a, w1_0 = _diff_attn(proj, lam_p, dup(e_q_norm[0]), dup(e_k_norm[0]), e_subln[0].reshape(1, -1), cos, sin,
                           (mlp_w1, 0), n_heads=a_heads, n_lat=n_lat, lam_init=0.8 - 0.6 * math.exp(-0.3 * 0))
    gates = gates[:, :, :n_gate].reshape(bsz, rows // GROUP, GROUP, 4, b_heads)
    gates = gates.transpose(0, 3, 4, 1, 2).reshape(bsz, 4, b_heads, rows // GROUP, 1, GROUP)
    prm = jnp.concatenate([e_alog_f, e_alog_b, e_dtb_f, e_dtb_b], axis=0)
    o_b = _gdn(proj, gates, prm, e_conv[0], e_o_norm[0].reshape(1, -1), n_heads=b_heads,
               col0=3 * a_heads, n_lat=n_lat)
    xs = _out_proj(o_a, 0, o_b, 0, e_w_out.astype(BF16), x, mod[0], gate_row=2, n_lat=n_lat, rows=rows,
                   res_tail=ctx, name="out_proj0")
    d_ff = mlp_w2.shape[1]
    hid, w2 = _norm_matmul(xs, 0, d, norm2[0], w1_0, mod=mod[0], which=1, n_lat=n_lat, relu2=True,
                           cast=(mlp_w2.reshape(depth * d_ff, d), None), out_dtype=BF16, name="mlp_up0")
    w2 = w2.reshape(depth, d_ff, d)
    xs, w1_1 = _down_proj(hid, w2, xs, mod[0], layer=0, cast=(mlp_w1, 1), gate_row=5, n_lat=n_lat,
                          name="mlp_down0")

    w_in1 = jnp.concatenate([o_w_in[0], o_w_in[0][:, Q_LORA + KV_LORA:]], axis=1)
    proj = _norm_matmul(xs, 0, d, norm1[1], w_in1.astype(BF16), mod=mod[1], which=0, n_lat=n_lat,
                        tm_cap=768, tn_cap=w_in1.shape[1], name="in_proj1")
    w_uq = o_w_uq[0].reshape(Q_LORA, c_heads, QK_NOPE + QK_ROPE)
    w_uq = jnp.concatenate([w_uq[:, :, :QK_NOPE].reshape(Q_LORA, -1), w_uq[:, :, QK_NOPE:].reshape(Q_LORA, -1)],
                           axis=1)
    q = _norm_matmul(proj, 0, Q_LORA, o_qa_norm[0], w_uq.astype(BF16), tn_cap=1024, out_dtype=BF16,
                     tm_cap=768, name="q_up")
    kv = _norm_matmul(proj, 1, KV_LORA, o_kva_norm[0], o_w_ukv.astype(BF16), tn_cap=1024, out_dtype=BF16,
                      tm_cap=768, name="kv_up")
    o_c = _mla_attn(q, kv, proj, o_qn_norm[0].reshape(1, -1), dup(o_qr_norm[0]), o_kn_norm[0].reshape(1, -1),
                    dup(o_kr_norm[0]), cos, sin, n_heads=c_heads, n_lat=n_lat,
                    kr_block=(Q_LORA + KV_LORA) // LANES)
    xl = _out_proj(o_c, 0, o_c, 1, o_w_out.astype(BF16), xs, mod[1], gate_row=2, n_lat=n_lat, rows=n_lat,
                   tm_cap=1024, name="out_proj1")
    hid = _norm_matmul(xl, 0, d, norm2[1], w1_1, mod=mod[1], which=1, n_lat=n_lat,
                       relu2=True, out_dtype=BF16, tm_cap=1024, name="mlp_up1")
    return _down_proj(hid, w2, xl, mod[1], layer=1, gate_row=5, n_lat=n_lat, tm_cap=1024, name="mlp_down1")
```
